```python
import math
import jax, jax.numpy as jnp
from jax import lax
import numpy as np

D_MODEL = 1024
BATCH = 8
SEQ = 2048
DEPTH = 1
DEC_BATCH = 128
DEC_SEQ = 4
PAST_LEN = 16384
PAGE_SIZE = 128

MIX_WIDTH = D_MODEL
W_S5 = MIX_WIDTH // 2
W_HG = MIX_WIDTH - W_S5
S5_CH = 16
S5_GROUPS = W_S5 // S5_CH
S5_STATE = 64
HG_DK = 128
HG_HEADS = W_HG // HG_DK
HG_DV = W_HG // HG_HEADS
HG_CHUNK = 64
EPS = 1e-6
LAMBDA_RE_MAX = -1e-4
PROJ_SIZES = (W_S5, W_S5, HG_HEADS * HG_DK, HG_HEADS * HG_DK, HG_HEADS * HG_DV, W_HG)
PROJ_SPLITS = tuple(int(v) for v in np.cumsum(PROJ_SIZES)[:-1])
PROJ_OUT = sum(PROJ_SIZES)

kernel_name = "hybrid_s5_hgrn2_parallel_heads_step"


def rmsnorm(x, g):
    xf = x.astype(jnp.float32)
    return xf * lax.rsqrt(jnp.mean(xf * xf, axis=-1, keepdims=True) + EPS) * g.astype(jnp.float32)


def s5_scan(bu_re, bu_im, a_re, a_im):
    ar = jnp.broadcast_to(a_re, bu_re.shape)
    ai = jnp.broadcast_to(a_im, bu_re.shape)

    def combine(e1, e2):
        ar1, ai1, br1, bi1 = e1
        ar2, ai2, br2, bi2 = e2
        return (ar2 * ar1 - ai2 * ai1, ar2 * ai1 + ai2 * ar1,
                ar2 * br1 - ai2 * bi1 + br2, ar2 * bi1 + ai2 * br1 + bi2)

    _, _, hr, hi = lax.associative_scan(combine, (ar, ai, bu_re, bu_im), axis=1)
    return hr, hi


def hgrn2_chunked(q, k, v, log_f, S0):
    Bn, L, H, DK = q.shape
    DV = v.shape[-1]
    C = HG_CHUNK if L % HG_CHUNK == 0 else L
    n = L // C

    def to_chunks(t):
        return t.reshape(Bn, n, C, H, t.shape[-1]).transpose(1, 0, 3, 2, 4)

    qc, kc, vc, gc = to_chunks(q), to_chunks(k), to_chunks(v), to_chunks(log_f)
    mask = jnp.tril(jnp.ones((C, C), dtype=bool))[:, :, None]

    def step(S, inp):
        qi, ki, vi, gi = inp
        b = jnp.cumsum(gi, axis=2)
        diff = b[:, :, :, None, :] - b[:, :, None, :, :]
        decay = jnp.exp(jnp.where(mask, diff, -jnp.inf))
        att = jnp.einsum('bhtk,bhsk,bhtsk->bhts', qi, ki, decay)
        o = jnp.einsum('bhts,bhsv->bhtv', att, vi) + jnp.einsum('bhtk,bhkv->bhtv', qi * jnp.exp(b), S)
        b_last = b[:, :, -1:, :]
        S_new = jnp.exp(b_last[:, :, 0, :])[..., None] * S + jnp.einsum('bhsk,bhsv->bhkv', ki * jnp.exp(b_last - b), vi)
        return S_new, o

    S_fin, o = lax.scan(step, S0, (qc, kc, vc, gc))
    o = o.transpose(1, 0, 3, 2, 4).reshape(Bn, L, H, DV)
    return o, S_fin


def mixer_layer(x, h0_re, h0_im, S0, norm_g, w_in, lam_re, lam_im, log_step, b_re, b_im, c_re, c_im, d,
                w_glu, b_glu, lb, onorm_g, w_out):
    Bn, L, _ = x.shape
    f32 = jnp.float32
    h = rmsnorm(x, norm_g)
    proj = h @ w_in.astype(f32)
    u, z_s, q, gf, iv, z_h = jnp.split(proj, PROJ_SPLITS, axis=-1)

    lr = jnp.minimum(lam_re.astype(f32), LAMBDA_RE_MAX)
    li = lam_im.astype(f32)
    dt = jnp.exp(log_step.astype(f32))[:, None]
    er = jnp.exp(lr * dt)
    a_re = er * jnp.cos(li * dt)
    a_im = er * jnp.sin(li * dt)
    den = lr * lr + li * li
    zr = ((a_re - 1.0) * lr + a_im * li) / den
    zi = (a_im * lr - (a_re - 1.0) * li) / den
    br, bi = b_re.astype(f32), b_im.astype(f32)
    bb_re = zr[..., None] * br - zi[..., None] * bi
    bb_im = zr[..., None] * bi + zi[..., None] * br
    ug = u.reshape(Bn, L, S5_GROUPS, S5_CH)
    bu_re = jnp.einsum('blgc,gpc->blgp', ug, bb_re)
    bu_im = jnp.einsum('blgc,gpc->blgp', ug, bb_im)
    h0r, h0i = h0_re.astype(f32), h0_im.astype(f32)
    bu_re = bu_re.at[:, 0].add(a_re * h0r - a_im * h0i)
    bu_im = bu_im.at[:, 0].add(a_re * h0i + a_im * h0r)
    hr, hi = s5_scan(bu_re, bu_im, a_re, a_im)
    y = (jnp.einsum('blgp,gcp->blgc', hr, c_re.astype(f32))
         - jnp.einsum('blgp,gcp->blgc', hi, c_im.astype(f32))
         + d.astype(f32) * ug).reshape(Bn, L, W_S5)
    g = jax.nn.gelu(y)
    s5_out = g * jax.nn.sigmoid(g @ w_glu.astype(f32) + b_glu.astype(f32)) * jax.nn.silu(z_s)

    lbf = lb.astype(f32)
    f = lbf + (1.0 - lbf) * jax.nn.sigmoid(gf)
    log_f = jnp.log(f)
    k = (1.0 - lbf) * jax.nn.sigmoid(-gf)
    qh = q.reshape(Bn, L, HG_HEADS, HG_DK)
    kh = k.reshape(Bn, L, HG_HEADS, HG_DK)
    gh = log_f.reshape(Bn, L, HG_HEADS, HG_DK)
    vh = iv.reshape(Bn, L, HG_HEADS, HG_DV)
    o, S_fin = hgrn2_chunked(qh, kh, vh, gh, S0.astype(f32))
    o = o * lax.rsqrt(jnp.mean(o * o, axis=-1, keepdims=True) + EPS)
    hg_out = o.reshape(Bn, L, W_HG) * onorm_g.astype(f32) * jax.nn.silu(z_h)

    out = jnp.concatenate([s5_out, hg_out], axis=-1) @ w_out.astype(f32)
    return x + out.astype(x.dtype), hr[:, -1], hi[:, -1], S_fin


def setup_inputs(seed: int = 0) -> dict:
    key = jax.random.key(seed)
    ks = jax.random.split(key, 24)
    f32 = jnp.float32
    n = jnp.arange(S5_STATE, dtype=f32)
    lam_re = -0.5 + 0.01 * jax.random.normal(ks[5], (DEPTH, S5_GROUPS, S5_STATE), f32)
    lam_im = math.pi * n + 0.01 * jax.random.normal(ks[6], (DEPTH, S5_GROUPS, S5_STATE), f32)
    log_step = jax.random.uniform(ks[7], (DEPTH, S5_GROUPS), f32, math.log(1e-3), math.log(1e-1))
    return {
        "x_prompt": jax.random.normal(ks[0], (BATCH, SEQ, D_MODEL), f32),
        "x_sample": jax.random.normal(ks[1], (DEC_BATCH, DEC_SEQ, D_MODEL), f32),
        "state_s5_re": 0.1 * jax.random.normal(ks[2], (DEPTH, DEC_BATCH, S5_GROUPS, S5_STATE), f32),
        "state_s5_im": 0.1 * jax.random.normal(ks[3], (DEPTH, DEC_BATCH, S5_GROUPS, S5_STATE), f32),
        "state_hgrn": 0.5 * jax.random.normal(ks[4], (DEPTH, DEC_BATCH, HG_HEADS, HG_DK, HG_DV), f32),
        "norm_g": 1.0 + 0.02 * jax.random.normal(ks[8], (DEPTH, D_MODEL), f32),
        "w_in": jax.random.normal(ks[9], (DEPTH, D_MODEL, PROJ_OUT), f32) * D_MODEL ** -0.5,
        "s5_lambda_re": lam_re,
        "s5_lambda_im": lam_im,
        "s5_log_step": log_step,
        "s5_b_re": 0.5 * jax.random.normal(ks[10], (DEPTH, S5_GROUPS, S5_STATE, S5_CH), f32),
        "s5_b_im": 0.5 * jax.random.normal(ks[11], (DEPTH, S5_GROUPS, S5_STATE, S5_CH), f32),
        "s5_c_re": jax.random.normal(ks[12], (DEPTH, S5_GROUPS, S5_CH, S5_STATE), f32) * S5_STATE ** -0.5,
        "s5_c_im": jax.random.normal(ks[13], (DEPTH, S5_GROUPS, S5_CH, S5_STATE), f32) * S5_STATE ** -0.5,
        "s5_d": 0.5 * jax.random.normal(ks[14], (DEPTH, S5_GROUPS, S5_CH), f32),
        "w_glu": jax.random.normal(ks[15], (DEPTH, W_S5, W_S5), f32) * W_S5 ** -0.5,
        "b_glu": 0.02 * jax.random.normal(ks[16], (DEPTH, W_S5), f32),
        "hgrn_lb_logits": 0.1 * jax.random.normal(ks[17], (DEPTH + 1, HG_HEADS * HG_DK), f32),
        "hgrn_onorm_g": 1.0 + 0.02 * jax.random.normal(ks[18], (DEPTH, W_HG), f32),
        "w_out": jax.random.normal(ks[19], (DEPTH, MIX_WIDTH, D_MODEL), f32) * MIX_WIDTH ** -0.5,
        "final_norm_g": 1.0 + 0.02 * jax.random.normal(ks[20], (D_MODEL,), f32),
    }


def reference(x_prompt, x_sample, state_s5_re, state_s5_im, state_hgrn, norm_g, w_in, s5_lambda_re, s5_lambda_im,
              s5_log_step, s5_b_re, s5_b_im, s5_c_re, s5_c_im, s5_d, w_glu, b_glu, hgrn_lb_logits, hgrn_onorm_g,
              w_out, final_norm_g):
    f32 = jnp.float32
    lb_all = jnp.cumsum(jax.nn.softmax(hgrn_lb_logits.astype(f32), axis=0), axis=0)
    xp, xs = x_prompt, x_sample
    zero_re = jnp.zeros((x_prompt.shape[0], S5_GROUPS, S5_STATE), f32)
    zero_S = jnp.zeros((x_prompt.shape[0], HG_HEADS, HG_DK, HG_DV), f32)
    p_re, p_im, p_hg, s_re, s_im, s_hg = [], [], [], [], [], []
    for l in range(DEPTH):
        params = (norm_g[l], w_in[l], s5_lambda_re[l], s5_lambda_im[l], s5_log_step[l], s5_b_re[l], s5_b_im[l],
                  s5_c_re[l], s5_c_im[l], s5_d[l], w_glu[l], b_glu[l], lb_all[l], hgrn_onorm_g[l], w_out[l])
        xp, hr, hi, S = mixer_layer(xp, zero_re, zero_re, zero_S, *params)
        p_re.append(hr); p_im.append(hi); p_hg.append(S)
        xs, hr, hi, S = mixer_layer(xs, state_s5_re[l], state_s5_im[l], state_hgrn[l], *params)
        s_re.append(hr); s_im.append(hi); s_hg.append(S)
    sd = state_s5_re.dtype
    y_prompt = rmsnorm(xp, final_norm_g).astype(x_prompt.dtype)
    y_sample = rmsnorm(xs, final_norm_g).astype(x_sample.dtype)
    new_s5_re_prompt = jnp.stack(p_re).astype(sd)
    new_s5_im_prompt = jnp.stack(p_im).astype(sd)
    new_hgrn_prompt = jnp.stack(p_hg).astype(state_hgrn.dtype)
    new_s5_re_sample = jnp.stack(s_re).astype(sd)
    new_s5_im_sample = jnp.stack(s_im).astype(sd)
    new_hgrn_sample = jnp.stack(s_hg).astype(state_hgrn.dtype)
    return (y_prompt, y_sample, new_s5_re_prompt, new_s5_im_prompt, new_hgrn_prompt,
            new_s5_re_sample, new_s5_im_sample, new_hgrn_sample)
```

```python
import functools
import math

import jax
import jax.numpy as jnp
from jax import lax
from jax.experimental import pallas as pl
from jax.experimental.pallas import tpu as pltpu

F32 = jnp.float32
BF16 = jnp.bfloat16

D_MODEL = 1024
W_S5 = 512
W_HG = 512
S5_CH = 16
S5_GROUPS = 32
S5_STATE = 64
S5_LANES = S5_GROUPS * S5_STATE
S5_HALF = S5_LANES // 2
HG_DK = 128
HG_HEADS = 4
HG_DV = 128
EPS = 1e-6
LAMBDA_RE_MAX = -1e-4
PROJ_OUT = 3072
OFF_U, OFF_ZS, OFF_Q, OFF_F, OFF_I, OFF_ZH = 0, 512, 1024, 1536, 2048, 2560

LANES = 128
SUBLANES = 8
BF16_TILE_ROWS = 16
TILES_PER_HALF = 2 * S5_HALF // LANES
ROW_BLOCK = 128
V7X_VMEM_LIMIT = 60 * 1024 * 1024
SAMPLE_PAD_SEQ = 8


def _prep_kernel(lre_ref, lim_ref, ls_ref, brt_ref, bit_ref, lg_ref,
                 are_ref, aim_ref, bbre_ref, bbim_ref, lb_ref):
    lr = jnp.minimum(lre_ref[...], LAMBDA_RE_MAX)
    li = lim_ref[...]
    dt = jnp.exp(ls_ref[...])
    er = jnp.exp(lr * dt)
    a_re = er * jnp.cos(li * dt)
    a_im = er * jnp.sin(li * dt)
    den = lr * lr + li * li
    zr = ((a_re - 1.0) * lr + a_im * li) / den
    zi = (a_im * lr - (a_re - 1.0) * li) / den
    br = brt_ref[...]
    bi = bit_ref[...]
    are_ref[...] = a_re
    aim_ref[...] = a_im
    bbre_ref[...] = zr * br - zi * bi
    bbim_ref[...] = zr * bi + zi * br
    lg = lg_ref[...]
    e = jnp.exp(lg - jnp.max(lg, axis=0, keepdims=True))
    lb_ref[...] = e[0:1, :] / jnp.sum(e, axis=0, keepdims=True)


def _prepare_params(lam_re, lam_im, log_step, b_re, b_im, lb_logits):
    gc = S5_GROUPS * S5_CH
    bc = lambda a: jnp.broadcast_to(a[:, None, :], (S5_GROUPS, S5_CH, S5_STATE)).reshape(gc, S5_STATE)
    lre = bc(lam_re.astype(F32))
    lim = bc(lam_im.astype(F32))
    ls = bc(jnp.broadcast_to(log_step.astype(F32)[:, None], (S5_GROUPS, S5_STATE)))
    brt = jnp.transpose(b_re.astype(F32), (0, 2, 1)).reshape(gc, S5_STATE)
    bit = jnp.transpose(b_im.astype(F32), (0, 2, 1)).reshape(gc, S5_STATE)
    outs = pl.pallas_call(
        _prep_kernel,
        out_shape=[jax.ShapeDtypeStruct((gc, S5_STATE), F32)] * 4
        + [jax.ShapeDtypeStruct((1, HG_HEADS * HG_DK), F32)],
        name="s5_hgrn_prep",
    )(lre, lim, ls, brt, bit, lb_logits.astype(F32))
    a_re_f, a_im_f, bbre, bbim, lb = outs
    a_re = a_re_f.reshape(S5_GROUPS, S5_CH, S5_STATE)[:, 0, :].reshape(1, S5_LANES)
    a_im = a_im_f.reshape(S5_GROUPS, S5_CH, S5_STATE)[:, 0, :].reshape(1, S5_LANES)
    return a_re, a_im, bbre, bbim, lb


def _block_diag_weights(bbre, bbim, c_re, c_im):
    eye = jnp.eye(S5_GROUPS, dtype=F32)[:, None, :, None]

    def in_full(bbt):
        return (bbt.reshape(S5_GROUPS, S5_CH, 1, S5_STATE) * eye).reshape(W_S5, S5_LANES)

    def out_full(c):
        ct = jnp.transpose(c.astype(F32), (0, 2, 1)).reshape(S5_GROUPS, S5_STATE, 1, S5_CH)
        return (ct * eye).reshape(S5_LANES, W_S5)

    bre_f, bim_f = in_full(bbre), in_full(bbim)
    cre_f, cim_f = out_full(c_re), out_full(-c_im)
    bcat, ccat = [], []
    for i in range(2):
        rs = slice(256 * i, 256 * (i + 1))
        cs = slice(S5_HALF * i, S5_HALF * (i + 1))
        bcat.append(jnp.concatenate([bre_f[rs, cs], bim_f[rs, cs]], axis=1))
        ccat.append(jnp.concatenate([cre_f[cs, rs], cim_f[cs, rs]], axis=0))
    return jnp.stack(bcat).astype(BF16), jnp.stack(ccat).astype(BF16)


def _sigmoid(x):
    return 1.0 / (1.0 + jnp.exp(-x))


def _gelu_tanh(x):
    return 0.5 * x * (1.0 + jnp.tanh(math.sqrt(2.0 / math.pi) * (x + 0.044715 * (x * x * x))))


def _dot(a, b):
    return jnp.dot(a, b, preferred_element_type=F32)


def _dot_nt(a, b):
    return lax.dot_general(a, b, (((1,), (1,)), ((), ())), preferred_element_type=F32)


def _dot_tn(a, b):
    return lax.dot_general(a, b, (((0,), (0,)), ((), ())), preferred_element_type=F32)


def _row_bcast_blocks(src_ref, blk, off):
    parts = []
    for n in range(ROW_BLOCK // blk):
        row = src_ref[n * blk + off:n * blk + off + 1, :]
        parts.append(jnp.broadcast_to(row, (blk, HG_DK)))
    return parts[0] if len(parts) == 1 else jnp.concatenate(parts, axis=0)


def _mixer_kernel(*refs, NB, C, t_valid, has_state):
    if has_state:
        x_ref, h0r_ref, h0i_ref, s0_ref = refs[:4]
        refs = refs[4:]
    else:
        x_ref = refs[0]
        refs = refs[1:]
    (ng_ref, win_ref, are_ref, aim_ref, bcat_ref, ccat_ref, d_ref, wglu_ref, bglu_ref, lb_ref,
     og_ref, wout_ref, fg_ref,
     y_ref, hr_ref, hi_ref, s_ref,
     proj_ref, bu_ref, o_ref, qe_ref, ke_ref, v_ref, bs_ref, dl_ref, cat_ref) = refs

    j = pl.program_id(1)
    nj = pl.num_programs(1)
    R = NB * C
    n_rb = R // ROW_BLOCK
    nbat = ROW_BLOCK // C

    @pl.when(j == 0)
    def _init():
        if has_state:
            hr_ref[...] = h0r_ref[...]
            hi_ref[...] = h0i_ref[...]

            def tr(b, c):
                for hd in range(HG_HEADS):
                    s_ref[b, hd] = s0_ref[b, hd].T
                return c
            lax.fori_loop(0, NB, tr, 0)
        else:
            hr_ref[...] = jnp.zeros(hr_ref.shape, F32)
            hi_ref[...] = jnp.zeros(hi_ref.shape, F32)
            s_ref[...] = jnp.zeros(s_ref.shape, F32)

    def x_rows(rb):
        return x_ref[pl.ds(rb * nbat, nbat)].reshape(ROW_BLOCK, D_MODEL)

    def rows_of(rb):
        return pl.ds(pl.multiple_of(rb * ROW_BLOCK, ROW_BLOCK), ROW_BLOCK)

    def phase1(rb, c):
        rows = rows_of(rb)
        x = x_rows(rb)
        hn = x * lax.rsqrt(jnp.mean(x * x, axis=-1, keepdims=True) + EPS) * ng_ref[...]
        hn = hn.astype(BF16)
        for sec in range(PROJ_OUT // 512):
            cs = slice(512 * sec, 512 * (sec + 1))
            proj_ref[rows, cs] = _dot(hn, win_ref[:, cs])
        for i in range(2):
            u_i = proj_ref[rows, OFF_U + 256 * i:OFF_U + 256 * (i + 1)].astype(BF16)
            bu = _dot(u_i, bcat_ref[i])
            for n in range(TILES_PER_HALF):
                bu_ref[TILES_PER_HALF * i + n, rows, :] = bu[:, LANES * n:LANES * (n + 1)]
        return c
    lax.fori_loop(0, n_rb, phase1, 0)

    width = min(S5_HALF, (8 * S5_HALF) // NB)
    for i in range(2):
        for off in range(0, S5_HALF, width):
            st = slice(S5_HALF * i + off, S5_HALF * i + off + width)
            re_t = TILES_PER_HALF * i + off // LANES
            im_t = re_t + TILES_PER_HALF // 2
            nt = width // LANES
            ar = are_ref[:, st]
            ai = aim_ref[:, st]

            def step(t, carry, re_t=re_t, im_t=im_t, ar=ar, ai=ai):
                hr, hi = carry
                rows = pl.ds(t, NB, stride=C)
                bur = jnp.concatenate([bu_ref[re_t + n, rows, :] for n in range(nt)], axis=1)
                bui = jnp.concatenate([bu_ref[im_t + n, rows, :] for n in range(nt)], axis=1)
                nhr = ar * hr - ai * hi + bur
                nhi = ar * hi + ai * hr + bui
                for n in range(nt):
                    bu_ref[re_t + n, rows, :] = nhr[:, LANES * n:LANES * (n + 1)]
                    bu_ref[im_t + n, rows, :] = nhi[:, LANES * n:LANES * (n + 1)]
                return nhr, nhi
            hr, hi = lax.fori_loop(0, t_valid, step, (hr_ref[:, st], hi_ref[:, st]))
            hr_ref[:, st] = hr
            hi_ref[:, st] = hi

    def phase3(rb, c):
        rows = rows_of(rb)
        ys = []
        for i in range(2):
            hs = jnp.concatenate([bu_ref[TILES_PER_HALF * i + n, rows, :].astype(BF16)
                                  for n in range(TILES_PER_HALF)], axis=1)
            ys.append(_dot(hs, ccat_ref[i]))
        u = proj_ref[rows, OFF_U:OFF_U + W_S5]
        y = jnp.concatenate(ys, axis=1) + d_ref[...] * u
        g = _gelu_tanh(y)
        glu = _dot(g.astype(BF16), wglu_ref[...]) + bglu_ref[...]
        zs = proj_ref[rows, OFF_ZS:OFF_ZS + W_S5]
        s5 = g * _sigmoid(glu) * (zs * _sigmoid(zs))
        cat_ref[rows, 0:W_S5] = s5.astype(BF16)
        return c
    lax.fori_loop(0, n_rb, phase3, 0)

    row = lax.broadcasted_iota(jnp.int32, (ROW_BLOCK, ROW_BLOCK), 0)
    col = lax.broadcasted_iota(jnp.int32, (ROW_BLOCK, ROW_BLOCK), 1)
    tin = row & (C - 1)

    for hd in range(HG_HEADS):
        hc = slice(HG_DK * hd, HG_DK * (hd + 1))
        lb = lb_ref[:, hc]

        def intra(rb, c, hd=hd, hc=hc, lb=lb):
            rows = rows_of(rb)
            q = proj_ref[rows, OFF_Q + HG_DK * hd:OFF_Q + HG_DK * (hd + 1)]
            gf = proj_ref[rows, OFF_F + HG_DK * hd:OFF_F + HG_DK * (hd + 1)]
            v = proj_ref[rows, OFF_I + HG_DV * hd:OFF_I + HG_DV * (hd + 1)].astype(BF16)
            e = jnp.exp(-jnp.abs(gf))
            r = 1.0 / (1.0 + e)
            pos = gf >= 0.0
            sig_p = jnp.where(pos, r, e * r)
            sig_n = jnp.where(pos, e * r, r)
            f = lb + (1.0 - lb) * sig_p
            g = jnp.log(f)
            k = (1.0 - lb) * sig_n
            if t_valid < C:
                live = tin < t_valid
                g = jnp.where(live, g, 0.0)
                k = jnp.where(live, k, 0.0)
            b = g
            dd = 1
            while dd < C:
                b = b + jnp.where(tin >= dd, pltpu.roll(b, dd, axis=0), 0.0)
                dd *= 2
            bs_ref[...] = b
            att = jnp.where(row == col, _dot_nt(q.astype(BF16), k.astype(BF16)), 0.0)
            m = 1
            while m < C:
                two_m = 2 * m
                p2 = row & (two_m - 1)
                is_hi = p2 >= m
                if m == 1:
                    ref = jnp.where(is_hi, pltpu.roll(b, 1, axis=0), b)
                elif m == 2:
                    ref = jnp.where(p2 == 0, pltpu.roll(b, ROW_BLOCK - 1, axis=0),
                                    jnp.where(p2 == 1, b,
                                              jnp.where(p2 == 2, pltpu.roll(b, 1, axis=0),
                                                        pltpu.roll(b, 2, axis=0))))
                else:
                    ref = _row_bcast_blocks(bs_ref, two_m, m - 1)
                dec = jnp.exp(-jnp.abs(b - ref))
                qt = jnp.where(is_hi, q * dec, 0.0).astype(BF16)
                kt = jnp.where(is_hi, 0.0, k * dec).astype(BF16)
                same = (row & ~(two_m - 1)) == (col & ~(two_m - 1))
                att = att + jnp.where(same, _dot_nt(qt, kt), 0.0)
                m = two_m
            o_ref[rows, hc] = _dot(att.astype(BF16), v)
            b_last = _row_bcast_blocks(bs_ref, C, C - 1)
            qe_ref[rows, :] = (q * jnp.exp(b)).astype(qe_ref.dtype)
            ke_ref[rows, :] = (k * jnp.exp(b_last - b)).astype(ke_ref.dtype)
            v_ref[rows, :] = v.astype(v_ref.dtype)
            for n in range(nbat):
                last = bs_ref[n * C + C - 1:n * C + C, :]
                dl_ref[rb * nbat + n, :, hc] = jnp.broadcast_to(jnp.exp(last), (SUBLANES, HG_DK))
            return c
        lax.fori_loop(0, n_rb, intra, 0)

        def inter(bi, c, hd=hd, hc=hc):
            rows = pl.ds(pl.multiple_of(bi * C, C), C)
            st = s_ref[bi, hd]
            o_ref[rows, hc] = o_ref[rows, hc] + _dot_nt(qe_ref[rows, :], st.astype(qe_ref.dtype))
            upd = _dot_tn(v_ref[rows, :], ke_ref[rows, :])
            s_ref[bi, hd] = st * dl_ref[bi, 0:1, hc] + upd
            return c
        lax.fori_loop(0, NB, inter, 0)

    def phase5(rb, c):
        rows = rows_of(rb)
        for hd in range(HG_HEADS):
            hc = slice(HG_DV * hd, HG_DV * (hd + 1))
            o = o_ref[rows, hc]
            o = o * lax.rsqrt(jnp.mean(o * o, axis=-1, keepdims=True) + EPS)
            zh = proj_ref[rows, OFF_ZH + HG_DV * hd:OFF_ZH + HG_DV * (hd + 1)]
            hg = o * og_ref[:, hc] * (zh * _sigmoid(zh))
            cat_ref[rows, W_S5 + HG_DV * hd:W_S5 + HG_DV * (hd + 1)] = hg.astype(BF16)
        out = _dot(cat_ref[rows, :], wout_ref[...])
        xn = x_rows(rb) + out
        yv = xn * lax.rsqrt(jnp.mean(xn * xn, axis=-1, keepdims=True) + EPS) * fg_ref[...]
        y_ref[pl.ds(rb * nbat, nbat)] = yv.reshape(nbat, C, D_MODEL)
        return c
    lax.fori_loop(0, n_rb, phase5, 0)

    @pl.when(j == nj - 1)
    def _fin():
        def tr(b, c):
            for hd in range(HG_HEADS):
                s_ref[b, hd] = s_ref[b, hd].T
            return c
        lax.fori_loop(0, NB, tr, 0)


def _mixer_call(x, state, weights, *, NB, C, t_valid, name):
    B, L, _ = x.shape
    has_state = state is not None
    grid = (B // NB, L // C)
    R = NB * C

    def full(a):
        nd = a.ndim
        return pl.BlockSpec(a.shape, lambda i, j, nd=nd: (0,) * nd, pipeline_mode=pl.Buffered(1))

    x_spec = pl.BlockSpec((NB, C, D_MODEL), lambda i, j: (i, j, 0))
    h_spec = pl.BlockSpec((NB, S5_LANES), lambda i, j: (i, 0))
    s_spec = pl.BlockSpec((NB, HG_HEADS, HG_DK, HG_DV), lambda i, j: (i, 0, 0, 0))
    in_specs = [x_spec] + ([h_spec, h_spec, s_spec] if has_state else []) + [full(w) for w in weights]
    args = [x] + (list(state) if has_state else []) + list(weights)
    out_shape = [
        jax.ShapeDtypeStruct((B, L, D_MODEL), x.dtype),
        jax.ShapeDtypeStruct((B, S5_LANES), F32),
        jax.ShapeDtypeStruct((B, S5_LANES), F32),
        jax.ShapeDtypeStruct((B, HG_HEADS, HG_DK, HG_DV), F32),
    ]
    chunk_dt = BF16 if C % BF16_TILE_ROWS == 0 else F32
    scratch = [
        pltpu.VMEM((R, PROJ_OUT), F32),
        pltpu.VMEM((2 * TILES_PER_HALF, R, LANES), F32),
        pltpu.VMEM((R, W_HG), F32),
        pltpu.VMEM((R, HG_DK), chunk_dt),
        pltpu.VMEM((R, HG_DK), chunk_dt),
        pltpu.VMEM((R, HG_DV), chunk_dt),
        pltpu.VMEM((ROW_BLOCK, HG_DK), F32),
        pltpu.VMEM((NB, SUBLANES, W_HG), F32),
        pltpu.VMEM((R, D_MODEL), BF16),
    ]
    return pl.pallas_call(
        functools.partial(_mixer_kernel, NB=NB, C=C, t_valid=t_valid, has_state=has_state),
        grid=grid,
        in_specs=in_specs,
        out_specs=[x_spec, h_spec, h_spec, s_spec],
        out_shape=out_shape,
        scratch_shapes=scratch,
        compiler_params=pltpu.CompilerParams(
            dimension_semantics=("arbitrary", "arbitrary"),
            vmem_limit_bytes=V7X_VMEM_LIMIT),
        name=name,
    )(*args)


def kernel(x_prompt, x_sample, state_s5_re, state_s5_im, state_hgrn, norm_g, w_in, s5_lambda_re, s5_lambda_im,
           s5_log_step, s5_b_re, s5_b_im, s5_c_re, s5_c_im, s5_d, w_glu, b_glu, hgrn_lb_logits, hgrn_onorm_g,
           w_out, final_norm_g):
    assert norm_g.shape[0] == 1, "single-layer model"
    l = 0
    a_re, a_im, bbre, bbim, lb = _prepare_params(
        s5_lambda_re[l], s5_lambda_im[l], s5_log_step[l], s5_b_re[l], s5_b_im[l], hgrn_lb_logits)
    bcat, ccat = _block_diag_weights(bbre, bbim, s5_c_re[l], s5_c_im[l])
    weights = (
        norm_g[l].astype(F32).reshape(1, D_MODEL),
        w_in[l].astype(BF16),
        a_re, a_im, bcat, ccat,
        s5_d[l].astype(F32).reshape(1, W_S5),
        w_glu[l].astype(BF16),
        b_glu[l].astype(F32).reshape(1, W_S5),
        lb,
        hgrn_onorm_g[l].astype(F32).reshape(1, W_HG),
        w_out[l].astype(BF16),
        final_norm_g.astype(F32).reshape(1, D_MODEL),
    )

    Bp, Lp, _ = x_prompt.shape
    yp, p_re, p_im, p_hg = _mixer_call(x_prompt, None, weights, NB=Bp, C=64, t_valid=64, name="mixer_prompt")

    Bs, Ls, _ = x_sample.shape
    xs = jnp.pad(x_sample, ((0, 0), (0, SAMPLE_PAD_SEQ - Ls), (0, 0)))
    state = (state_s5_re[l].reshape(Bs, S5_LANES).astype(F32),
             state_s5_im[l].reshape(Bs, S5_LANES).astype(F32),
             state_hgrn[l].astype(F32))
    ys, s_re, s_im, s_hg = _mixer_call(xs, state, weights, NB=16, C=SAMPLE_PAD_SEQ, t_valid=Ls, name="mixer_sample")
    ys = ys[:, :Ls]

    sd = state_s5_re.dtype
    shp = (1, Bp, S5_GROUPS, S5_STATE)
    shs = (1, Bs, S5_GROUPS, S5_STATE)
    return (yp, ys,
            p_re.reshape(shp).astype(sd), p_im.reshape(shp).astype(sd), p_hg[None].astype(state_hgrn.dtype),
            s_re.reshape(shs).astype(sd), s_im.reshape(shs).astype(sd), s_hg[None].astype(state_hgrn.dtype))
```

```python
import functools
import math

import jax
import jax.numpy as jnp
import numpy as np
from jax import lax
from jax.experimental import pallas as pl
from jax.experimental.pallas import tpu as pltpu

F32 = jnp.float32
BF16 = jnp.bfloat16

D_MODEL = 1024
W_S5 = 512
W_HG = 512
S5_CH = 16
S5_GROUPS = 32
S5_STATE = 64
S5_LANES = S5_GROUPS * S5_STATE
S5_HALF = S5_LANES // 2
HG_DK = 128
HG_HEADS = 4
HG_DV = 128
EPS = 1e-6
LAMBDA_RE_MAX = -1e-4
PROJ_OUT = 3072
OFF_Q = 1024
HG_COLS = PROJ_OUT - OFF_Q
LOG2E = 1.4426950408889634

LANES = 128
SUBLANES = 8
BF16_TILE_ROWS = 16
ROW_BLOCK = 128
V7X_VMEM_LIMIT = 60 * 1024 * 1024
SAMPLE_PAD_SEQ = 8


def _prep_kernel(lre_ref, lim_ref, ls_ref, brt_ref, bit_ref, lg_ref,
                 are_ref, aim_ref, bbre_ref, bbim_ref, lb_ref):
    lr = jnp.minimum(lre_ref[...], LAMBDA_RE_MAX)
    li = lim_ref[...]
    dt = jnp.exp(ls_ref[...])
    er = jnp.exp(lr * dt)
    a_re = er * jnp.cos(li * dt)
    a_im = er * jnp.sin(li * dt)
    den = lr * lr + li * li
    zr = ((a_re - 1.0) * lr + a_im * li) / den
    zi = (a_im * lr - (a_re - 1.0) * li) / den
    br = brt_ref[...]
    bi = bit_ref[...]
    are_ref[...] = a_re
    aim_ref[...] = a_im
    bbre_ref[...] = zr * br - zi * bi
    bbim_ref[...] = zr * bi + zi * br
    lg = lg_ref[...]
    e = jnp.exp(lg - jnp.max(lg, axis=0, keepdims=True))
    lb_ref[...] = e[0:1, :] / jnp.sum(e, axis=0, keepdims=True)


def _prepare_params(lam_re, lam_im, log_step, b_re, b_im, lb_logits):
    gc = S5_GROUPS * S5_CH
    bc = lambda a: jnp.broadcast_to(a[:, None, :], (S5_GROUPS, S5_CH, S5_STATE)).reshape(gc, S5_STATE)
    lre = bc(lam_re.astype(F32))
    lim = bc(lam_im.astype(F32))
    ls = bc(jnp.broadcast_to(log_step.astype(F32)[:, None], (S5_GROUPS, S5_STATE)))
    brt = jnp.transpose(b_re.astype(F32), (0, 2, 1)).reshape(gc, S5_STATE)
    bit = jnp.transpose(b_im.astype(F32), (0, 2, 1)).reshape(gc, S5_STATE)
    outs = pl.pallas_call(
        _prep_kernel,
        out_shape=[jax.ShapeDtypeStruct((gc, S5_STATE), F32)] * 4
        + [jax.ShapeDtypeStruct((1, HG_HEADS * HG_DK), F32)],
        name="s5_hgrn_prep",
    )(lre, lim, ls, brt, bit, lb_logits.astype(F32))
    a_re_f, a_im_f, bbre, bbim, lb = outs
    a_re = a_re_f.reshape(S5_GROUPS, S5_CH, S5_STATE)[:, 0, :].reshape(1, S5_LANES)
    a_im = a_im_f.reshape(S5_GROUPS, S5_CH, S5_STATE)[:, 0, :].reshape(1, S5_LANES)
    return a_re, a_im, bbre, bbim, lb


def _block_diag_weights(bbre, bbim, c_re, c_im):
    eye = jnp.eye(S5_GROUPS, dtype=F32)[:, None, :, None]

    def in_full(bbt):
        return (bbt.reshape(S5_GROUPS, S5_CH, 1, S5_STATE) * eye).reshape(W_S5, S5_LANES)

    def out_full(c):
        ct = jnp.transpose(c.astype(F32), (0, 2, 1)).reshape(S5_GROUPS, S5_STATE, 1, S5_CH)
        return (ct * eye).reshape(S5_LANES, W_S5)

    bre_f, bim_f = in_full(bbre), in_full(bbim)
    cre_f, cim_f = out_full(c_re), out_full(-c_im)
    bcat, ccat = [], []
    for i in range(2):
        rs = slice(256 * i, 256 * (i + 1))
        cs = slice(S5_HALF * i, S5_HALF * (i + 1))
        bcat.append(jnp.concatenate([bre_f[rs, cs], bim_f[rs, cs]], axis=1))
        ccat.append(jnp.concatenate([cre_f[cs, rs], cim_f[cs, rs]], axis=0))
    return jnp.stack(bcat).astype(BF16), jnp.stack(ccat).astype(BF16)


def _sigmoid(x):
    return 1.0 / (1.0 + jnp.exp(-x))


def _gelu_tanh(x):
    return 0.5 * x * (1.0 + jnp.tanh(math.sqrt(2.0 / math.pi) * (x + 0.044715 * (x * x * x))))


def _dot(a, b):
    return jnp.dot(a, b, preferred_element_type=F32)


def _dot_nt(a, b):
    return lax.dot_general(a, b, (((1,), (1,)), ((), ())), preferred_element_type=F32)


def _dot_tn(a, b):
    return lax.dot_general(a, b, (((0,), (0,)), ((), ())), preferred_element_type=F32)


def _row_bcast_blocks(src_ref, blk, off):
    parts = []
    for n in range(ROW_BLOCK // blk):
        row = src_ref[n * blk + off:n * blk + off + 1, :]
        parts.append(jnp.broadcast_to(row, (blk, HG_DK)))
    return parts[0] if len(parts) == 1 else jnp.concatenate(parts, axis=0)


def _mixer_kernel(*refs, NB, C, t_valid, has_state):
    if has_state:
        x_ref, h0r_ref, h0i_ref, s0_ref = refs[:4]
        refs = refs[4:]
    else:
        x_ref = refs[0]
        refs = refs[1:]
    (perm_ref, permt_ref, ng_ref, win_ref, are_ref, aim_ref, bcat_ref, ccat_ref, d_ref, wglu_ref, bglu_ref,
     lb_ref, og_ref, wout_ref, fg_ref,
     y_ref, hr_ref, hi_ref, s_ref,
     hn_ref, proj_ref, uz_ref, bu_ref, s5_ref, o_ref, qe_ref, ke_ref, v_ref, bs_ref, lev_ref, dl_ref,
     cat_ref) = refs

    j = pl.program_id(1)
    nj = pl.num_programs(1)
    R = NB * C
    n_rb = R // ROW_BLOCK
    nbat = ROW_BLOCK // C

    @pl.when(j == 0)
    def _init():
        if has_state:
            hr_ref[...] = h0r_ref[...]
            hi_ref[...] = h0i_ref[...]

            def tr(b, c):
                for hd in range(HG_HEADS):
                    s_ref[b, hd] = s0_ref[b, hd].T
                return c
            lax.fori_loop(0, NB, tr, 0)
        else:
            hr_ref[...] = jnp.zeros(hr_ref.shape, F32)
            hi_ref[...] = jnp.zeros(hi_ref.shape, F32)
            s_ref[...] = jnp.zeros(s_ref.shape, F32)

    def x_rows(rb):
        return x_ref[pl.ds(rb * nbat, nbat)].reshape(ROW_BLOCK, D_MODEL)

    def rows_of(rb):
        return pl.ds(pl.multiple_of(rb * ROW_BLOCK, ROW_BLOCK), ROW_BLOCK)

    def phase1a(rb, c):
        rows = rows_of(rb)
        x = x_rows(rb)
        hn = x * lax.rsqrt(jnp.mean(x * x, axis=-1, keepdims=True) + EPS) * ng_ref[...]
        hn = hn.astype(BF16)
        hn_ref[rows, :] = hn
        for sec in range(HG_COLS // 512):
            cs = slice(512 * sec, 512 * (sec + 1))
            proj_ref[rows, cs] = _dot(hn, win_ref[:, OFF_Q + 512 * sec:OFF_Q + 512 * (sec + 1)])
        return c
    lax.fori_loop(0, n_rb, phase1a, 0)

    def phase1b(rb, c):
        rows = rows_of(rb)
        hp = _dot(perm_ref[rows, :], hn_ref[...]).astype(BF16)
        for sec in range(2):
            uz_ref[rows, 512 * sec:512 * (sec + 1)] = _dot(hp, win_ref[:, 512 * sec:512 * (sec + 1)])
        for i in range(2):
            u_i = uz_ref[rows, 256 * i:256 * (i + 1)].astype(BF16)
            bu_ref[rows, 2 * S5_HALF * i:2 * S5_HALF * (i + 1)] = _dot(u_i, bcat_ref[i])
        return c
    lax.fori_loop(0, n_rb, phase1b, 0)

    width = min(S5_HALF, (SUBLANES * S5_HALF) // NB)
    for i in range(2):
        for off in range(0, S5_HALF, width):
            st = slice(S5_HALF * i + off, S5_HALF * i + off + width)
            re_c = 2 * S5_HALF * i + off
            im_c = re_c + S5_HALF
            ar = are_ref[:, st]
            ai = aim_ref[:, st]

            def step(t, carry, re_c=re_c, im_c=im_c, ar=ar, ai=ai):
                hr, hi = carry
                rows = pl.ds(pl.multiple_of(t * NB, NB), NB)
                nhr = ar * hr - ai * hi + bu_ref[rows, re_c:re_c + width]
                nhi = ar * hi + ai * hr + bu_ref[rows, im_c:im_c + width]
                bu_ref[rows, re_c:re_c + width] = nhr
                bu_ref[rows, im_c:im_c + width] = nhi
                return nhr, nhi
            hr, hi = lax.fori_loop(0, t_valid, step, (hr_ref[:, st], hi_ref[:, st]), unroll=2)
            hr_ref[:, st] = hr
            hi_ref[:, st] = hi

    def phase3(rb, c):
        rows = rows_of(rb)
        ys = []
        for i in range(2):
            hs = bu_ref[rows, 2 * S5_HALF * i:2 * S5_HALF * (i + 1)].astype(BF16)
            ys.append(_dot(hs, ccat_ref[i]))
        u = uz_ref[rows, 0:W_S5]
        y = jnp.concatenate(ys, axis=1) + d_ref[...] * u
        g = _gelu_tanh(y)
        glu = _dot(g.astype(BF16), wglu_ref[...]) + bglu_ref[...]
        zs = uz_ref[rows, W_S5:2 * W_S5]
        s5 = g * _sigmoid(glu) * (zs * _sigmoid(zs))
        s5_ref[rows, :] = s5.astype(BF16)
        return c
    lax.fori_loop(0, n_rb, phase3, 0)

    def phase3b(rb, c):
        rows = rows_of(rb)
        cat_ref[rows, 0:W_S5] = _dot(permt_ref[rows, :], s5_ref[...]).astype(BF16)
        return c
    lax.fori_loop(0, n_rb, phase3b, 0)

    row = lax.broadcasted_iota(jnp.int32, (ROW_BLOCK, ROW_BLOCK), 0)
    col = lax.broadcasted_iota(jnp.int32, (ROW_BLOCK, ROW_BLOCK), 1)
    tin = row & (C - 1)
    levels = []
    m = 1
    while m < C:
        levels.append(m)
        m *= 2
    lev = jnp.where(row == col, 0, -1)
    for li, m in enumerate(levels):
        same = (row & ~(2 * m - 1)) == (col & ~(2 * m - 1))
        own = same & ((row & m) != 0) & ((col & m) == 0)
        lev = jnp.where(own, li + 1, lev)
    lev_ref[...] = lev

    for hd in range(HG_HEADS):
        hc = slice(HG_DK * hd, HG_DK * (hd + 1))
        lb = lb_ref[:, hc]

        def intra(rb, c, hd=hd, hc=hc, lb=lb):
            rows = rows_of(rb)
            q = proj_ref[rows, HG_DK * hd:HG_DK * (hd + 1)]
            gf = proj_ref[rows, 512 + HG_DK * hd:512 + HG_DK * (hd + 1)]
            v = proj_ref[rows, 1024 + HG_DV * hd:1024 + HG_DV * (hd + 1)].astype(BF16)
            e = jnp.exp(-jnp.abs(gf))
            r = 1.0 / (1.0 + e)
            pos = gf >= 0.0
            sig_p = jnp.where(pos, r, e * r)
            sig_n = jnp.where(pos, e * r, r)
            f = lb + (1.0 - lb) * sig_p
            g = jnp.log(f)
            k = (1.0 - lb) * sig_n
            if t_valid < C:
                live = tin < t_valid
                g = jnp.where(live, g, 0.0)
                k = jnp.where(live, k, 0.0)
            b = g
            dd = 1
            while dd < C:
                b = b + jnp.where(tin >= dd, pltpu.roll(b, dd, axis=0), 0.0)
                dd *= 2
            bs_ref[...] = b
            lv = lev_ref[...]
            att = jnp.where(lv == 0, _dot_nt(q.astype(BF16), k.astype(BF16)), 0.0)
            for li, m in enumerate(levels):
                two_m = 2 * m
                p2 = row & (two_m - 1)
                if m == 1:
                    ref = jnp.where(p2 == 1, pltpu.roll(b, 1, axis=0), b)
                elif m == 2:
                    ref = jnp.where(p2 == 0, pltpu.roll(b, ROW_BLOCK - 1, axis=0),
                                    jnp.where(p2 == 1, b,
                                              jnp.where(p2 == 2, pltpu.roll(b, 1, axis=0),
                                                        pltpu.roll(b, 2, axis=0))))
                else:
                    ref = _row_bcast_blocks(bs_ref, two_m, m - 1)
                dec = jnp.exp2(jnp.abs(b - ref) * (-LOG2E))
                att = jnp.where(lv == li + 1, _dot_nt((q * dec).astype(BF16), (k * dec).astype(BF16)), att)
            o_ref[rows, hc] = _dot(att.astype(BF16), v)
            b_last = _row_bcast_blocks(bs_ref, C, C - 1)
            qe_ref[rows, hc] = (q * jnp.exp(b)).astype(qe_ref.dtype)
            ke_ref[rows, hc] = (k * jnp.exp(b_last - b)).astype(ke_ref.dtype)
            v_ref[rows, hc] = v.astype(v_ref.dtype)
            for n in range(nbat):
                last = bs_ref[n * C + C - 1:n * C + C, :]
                dl_ref[rb * nbat + n, :, hc] = jnp.broadcast_to(jnp.exp(last), (SUBLANES, HG_DK))
            return c
        lax.fori_loop(0, n_rb, intra, 0)

    def inter(bi, c):
        rows = pl.ds(pl.multiple_of(bi * C, C), C)
        for hd in range(HG_HEADS):
            hc = slice(HG_DK * hd, HG_DK * (hd + 1))
            st = s_ref[bi, hd]
            o_ref[rows, hc] = o_ref[rows, hc] + _dot_nt(qe_ref[rows, hc], st.astype(qe_ref.dtype))
            upd = _dot_tn(v_ref[rows, hc], ke_ref[rows, hc])
            s_ref[bi, hd] = st * dl_ref[bi, 0:1, hc] + upd
        return c
    lax.fori_loop(0, NB, inter, 0)

    def phase5(rb, c):
        rows = rows_of(rb)
        for hd in range(HG_HEADS):
            hc = slice(HG_DV * hd, HG_DV * (hd + 1))
            o = o_ref[rows, hc]
            o = o * lax.rsqrt(jnp.mean(o * o, axis=-1, keepdims=True) + EPS)
            zh = proj_ref[rows, 1536 + HG_DV * hd:1536 + HG_DV * (hd + 1)]
            hg = o * og_ref[:, hc] * (zh * _sigmoid(zh))
            cat_ref[rows, W_S5 + HG_DV * hd:W_S5 + HG_DV * (hd + 1)] = hg.astype(BF16)
        out = _dot(cat_ref[rows, :], wout_ref[...])
        xn = x_rows(rb) + out
        yv = xn * lax.rsqrt(jnp.mean(xn * xn, axis=-1, keepdims=True) + EPS) * fg_ref[...]
        y_ref[pl.ds(rb * nbat, nbat)] = yv.reshape(nbat, C, D_MODEL)
        return c
    lax.fori_loop(0, n_rb, phase5, 0)

    @pl.when(j == nj - 1)
    def _fin():
        def tr(b, c):
            for hd in range(HG_HEADS):
                s_ref[b, hd] = s_ref[b, hd].T
            return c
        lax.fori_loop(0, NB, tr, 0)


def _time_major_perm(NB, C):
    R = NB * C
    dst = np.arange(R)
    src = (dst % NB) * C + dst // NB
    p = np.zeros((R, R), np.float32)
    p[dst, src] = 1.0
    return p


def _mixer_call(x, state, weights, *, NB, C, t_valid, name):
    B, L, _ = x.shape
    has_state = state is not None
    grid = (B // NB, L // C)
    R = NB * C
    perm = _time_major_perm(NB, C)
    weights = (jnp.asarray(perm, BF16), jnp.asarray(perm.T, BF16)) + tuple(weights)

    def full(a):
        nd = a.ndim
        return pl.BlockSpec(a.shape, lambda i, j, nd=nd: (0,) * nd, pipeline_mode=pl.Buffered(1))

    x_spec = pl.BlockSpec((NB, C, D_MODEL), lambda i, j: (i, j, 0))
    h_spec = pl.BlockSpec((NB, S5_LANES), lambda i, j: (i, 0))
    s_spec = pl.BlockSpec((NB, HG_HEADS, HG_DK, HG_DV), lambda i, j: (i, 0, 0, 0))
    in_specs = [x_spec] + ([h_spec, h_spec, s_spec] if has_state else []) + [full(w) for w in weights]
    args = [x] + (list(state) if has_state else []) + list(weights)
    out_shape = [
        jax.ShapeDtypeStruct((B, L, D_MODEL), x.dtype),
        jax.ShapeDtypeStruct((B, S5_LANES), F32),
        jax.ShapeDtypeStruct((B, S5_LANES), F32),
        jax.ShapeDtypeStruct((B, HG_HEADS, HG_DK, HG_DV), F32),
    ]
    chunk_dt = BF16 if C % BF16_TILE_ROWS == 0 else F32
    scratch = [
        pltpu.VMEM((R, D_MODEL), BF16),
        pltpu.VMEM((R, HG_COLS), F32),
        pltpu.VMEM((R, 2 * W_S5), F32),
        pltpu.VMEM((R, 2 * S5_LANES), F32),
        pltpu.VMEM((R, W_S5), BF16),
        pltpu.VMEM((R, W_HG), F32),
        pltpu.VMEM((R, W_HG), chunk_dt),
        pltpu.VMEM((R, W_HG), chunk_dt),
        pltpu.VMEM((R, W_HG), chunk_dt),
        pltpu.VMEM((ROW_BLOCK, HG_DK), F32),
        pltpu.VMEM((ROW_BLOCK, ROW_BLOCK), jnp.int32),
        pltpu.VMEM((NB, SUBLANES, W_HG), F32),
        pltpu.VMEM((R, D_MODEL), BF16),
    ]
    return pl.pallas_call(
        functools.partial(_mixer_kernel, NB=NB, C=C, t_valid=t_valid, has_state=has_state),
        grid=grid,
        in_specs=in_specs,
        out_specs=[x_spec, h_spec, h_spec, s_spec],
        out_shape=out_shape,
        scratch_shapes=scratch,
        compiler_params=pltpu.CompilerParams(
            dimension_semantics=("arbitrary", "arbitrary"),
            vmem_limit_bytes=V7X_VMEM_LIMIT),
        name=name,
    )(*args)


def kernel(x_prompt, x_sample, state_s5_re, state_s5_im, state_hgrn, norm_g, w_in, s5_lambda_re, s5_lambda_im,
           s5_log_step, s5_b_re, s5_b_im, s5_c_re, s5_c_im, s5_d, w_glu, b_glu, hgrn_lb_logits, hgrn_onorm_g,
           w_out, final_norm_g):
    assert norm_g.shape[0] == 1, "single-layer model"
    l = 0
    a_re, a_im, bbre, bbim, lb = _prepare_params(
        s5_lambda_re[l], s5_lambda_im[l], s5_log_step[l], s5_b_re[l], s5_b_im[l], hgrn_lb_logits)
    bcat, ccat = _block_diag_weights(bbre, bbim, s5_c_re[l], s5_c_im[l])
    weights = (
        norm_g[l].astype(F32).reshape(1, D_MODEL),
        w_in[l].astype(BF16),
        a_re, a_im, bcat, ccat,
        s5_d[l].astype(F32).reshape(1, W_S5),
        w_glu[l].astype(BF16),
        b_glu[l].astype(F32).reshape(1, W_S5),
        lb,
        hgrn_onorm_g[l].astype(F32).reshape(1, W_HG),
        w_out[l].astype(BF16),
        final_norm_g.astype(F32).reshape(1, D_MODEL),
    )

    Bp, Lp, _ = x_prompt.shape
    yp, p_re, p_im, p_hg = _mixer_call(x_prompt, None, weights, NB=Bp, C=64, t_valid=64, name="mixer_prompt")

    Bs, Ls, _ = x_sample.shape
    xs = jnp.pad(x_sample, ((0, 0), (0, SAMPLE_PAD_SEQ - Ls), (0, 0)))
    state = (state_s5_re[l].reshape(Bs, S5_LANES).astype(F32),
             state_s5_im[l].reshape(Bs, S5_LANES).astype(F32),
             state_hgrn[l].astype(F32))
    ys, s_re, s_im, s_hg = _mixer_call(xs, state, weights, NB=16, C=SAMPLE_PAD_SEQ, t_valid=Ls, name="mixer_sample")
    ys = ys[:, :Ls]

    sd = state_s5_re.dtype
    shp = (1, Bp, S5_GROUPS, S5_STATE)
    shs = (1, Bs, S5_GROUPS, S5_STATE)
    return (yp, ys,
            p_re.reshape(shp).astype(sd), p_im.reshape(shp).astype(sd), p_hg[None].astype(state_hgrn.dtype),
            s_re.reshape(shs).astype(sd), s_im.reshape(shs).astype(sd), s_hg[None].astype(state_hgrn.dtype))
```

```python
import functools
import math

import jax
import jax.numpy as jnp
import numpy as np
from jax import lax
from jax.experimental import pallas as pl
from jax.experimental.pallas import tpu as pltpu

F32 = jnp.float32
BF16 = jnp.bfloat16

D_MODEL = 1024
W_S5 = 512
W_HG = 512
S5_CH = 16
S5_GROUPS = 32
S5_STATE = 64
S5_LANES = S5_GROUPS * S5_STATE
S5_HALF = S5_LANES // 2
HG_DK = 128
HG_HEADS = 4
HG_DV = 128
EPS = 1e-6
LAMBDA_RE_MAX = -1e-4
PROJ_OUT = 3072
OFF_Q = 1024
HG_COLS = PROJ_OUT - OFF_Q
LOG2E = 1.4426950408889634

LANES = 128
SUBLANES = 8
BF16_TILE_ROWS = 16
ROW_BLOCK = 128
MATMUL_ROWS = 256
V7X_VMEM_LIMIT = 60 * 1024 * 1024
SAMPLE_PAD_SEQ = 8


def _prep_kernel(lre_ref, lim_ref, ls_ref, brt_ref, bit_ref, lg_ref,
                 are_ref, aim_ref, bbre_ref, bbim_ref, lb_ref):
    lr = jnp.minimum(lre_ref[...], LAMBDA_RE_MAX)
    li = lim_ref[...]
    dt = jnp.exp(ls_ref[...])
    er = jnp.exp(lr * dt)
    a_re = er * jnp.cos(li * dt)
    a_im = er * jnp.sin(li * dt)
    den = lr * lr + li * li
    zr = ((a_re - 1.0) * lr + a_im * li) / den
    zi = (a_im * lr - (a_re - 1.0) * li) / den
    br = brt_ref[...]
    bi = bit_ref[...]
    are_ref[...] = a_re
    aim_ref[...] = a_im
    bbre_ref[...] = zr * br - zi * bi
    bbim_ref[...] = zr * bi + zi * br
    lg = lg_ref[...]
    e = jnp.exp(lg - jnp.max(lg, axis=0, keepdims=True))
    lb_ref[...] = e[0:1, :] / jnp.sum(e, axis=0, keepdims=True)


def _prepare_params(lam_re, lam_im, log_step, b_re, b_im, lb_logits):
    gc = S5_GROUPS * S5_CH
    bc = lambda a: jnp.broadcast_to(a[:, None, :], (S5_GROUPS, S5_CH, S5_STATE)).reshape(gc, S5_STATE)
    lre = bc(lam_re.astype(F32))
    lim = bc(lam_im.astype(F32))
    ls = bc(jnp.broadcast_to(log_step.astype(F32)[:, None], (S5_GROUPS, S5_STATE)))
    brt = jnp.transpose(b_re.astype(F32), (0, 2, 1)).reshape(gc, S5_STATE)
    bit = jnp.transpose(b_im.astype(F32), (0, 2, 1)).reshape(gc, S5_STATE)
    outs = pl.pallas_call(
        _prep_kernel,
        out_shape=[jax.ShapeDtypeStruct((gc, S5_STATE), F32)] * 4
        + [jax.ShapeDtypeStruct((1, HG_HEADS * HG_DK), F32)],
        name="s5_hgrn_prep",
    )(lre, lim, ls, brt, bit, lb_logits.astype(F32))
    a_re_f, a_im_f, bbre, bbim, lb = outs
    a_re = a_re_f.reshape(S5_GROUPS, S5_CH, S5_STATE)[:, 0, :].reshape(1, S5_LANES)
    a_im = a_im_f.reshape(S5_GROUPS, S5_CH, S5_STATE)[:, 0, :].reshape(1, S5_LANES)
    return a_re, a_im, bbre, bbim, lb


def _block_diag_weights(bbre, bbim, c_re, c_im):
    eye = jnp.eye(S5_GROUPS, dtype=F32)[:, None, :, None]

    def in_full(bbt):
        return (bbt.reshape(S5_GROUPS, S5_CH, 1, S5_STATE) * eye).reshape(W_S5, S5_LANES)

    def out_full(c):
        ct = jnp.transpose(c.astype(F32), (0, 2, 1)).reshape(S5_GROUPS, S5_STATE, 1, S5_CH)
        return (ct * eye).reshape(S5_LANES, W_S5)

    bre_f, bim_f = in_full(bbre), in_full(bbim)
    cre_f, cim_f = out_full(c_re), out_full(-c_im)
    bcat, ccat = [], []
    for i in range(2):
        rs = slice(256 * i, 256 * (i + 1))
        cs = slice(S5_HALF * i, S5_HALF * (i + 1))
        bcat.append(jnp.concatenate([bre_f[rs, cs], bim_f[rs, cs]], axis=1))
        ccat.append(jnp.concatenate([cre_f[cs, rs], cim_f[cs, rs]], axis=0))
    return jnp.stack(bcat).astype(BF16), jnp.stack(ccat).astype(BF16)


def _sigmoid(x):
    return 1.0 / (1.0 + jnp.exp(-x))


def _gelu_tanh(x):
    return 0.5 * x * (1.0 + jnp.tanh(math.sqrt(2.0 / math.pi) * (x + 0.044715 * (x * x * x))))


def _dot(a, b):
    return jnp.dot(a, b, preferred_element_type=F32)


def _dot_nt(a, b):
    return lax.dot_general(a, b, (((1,), (1,)), ((), ())), preferred_element_type=F32)


def _dot_tn(a, b):
    return lax.dot_general(a, b, (((0,), (0,)), ((), ())), preferred_element_type=F32)


def _row_bcast_blocks(src_ref, blk, off):
    parts = []
    for n in range(ROW_BLOCK // blk):
        row = src_ref[n * blk + off:n * blk + off + 1, :]
        parts.append(jnp.broadcast_to(row, (blk, HG_DK)))
    return parts[0] if len(parts) == 1 else jnp.concatenate(parts, axis=0)


def _mixer_kernel(*refs, NB, C, t_valid, has_state):
    if has_state:
        x_ref, h0r_ref, h0i_ref, s0_ref = refs[:4]
        refs = refs[4:]
    else:
        x_ref = refs[0]
        refs = refs[1:]
    (perm_ref, permt_ref, ng_ref, win_ref, are_ref, aim_ref, bcat_ref, ccat_ref, d_ref, wglu_ref, bglu_ref,
     lb_ref, og_ref, wout_ref, fg_ref,
     y_ref, hr_ref, hi_ref, s_ref,
     hn_ref, proj_ref, uz_ref, bu_ref, s5_ref, o_ref, qe_ref, ke_ref, v_ref, bsall_ref, lev_ref, dl_ref,
     cat_ref) = refs

    j = pl.program_id(1)
    nj = pl.num_programs(1)
    R = NB * C
    n_rb = R // ROW_BLOCK
    nbat = ROW_BLOCK // C

    @pl.when(j == 0)
    def _init():
        if has_state:
            hr_ref[...] = h0r_ref[...]
            hi_ref[...] = h0i_ref[...]

            def tr(b, c):
                for hd in range(HG_HEADS):
                    s_ref[b, hd] = s0_ref[b, hd].T
                return c
            lax.fori_loop(0, NB, tr, 0)
        else:
            hr_ref[...] = jnp.zeros(hr_ref.shape, F32)
            hi_ref[...] = jnp.zeros(hi_ref.shape, F32)
            s_ref[...] = jnp.zeros(s_ref.shape, F32)

    MM = min(R, MATMUL_ROWS)
    n_mb = R // MM
    mbat = MM // C

    def x_rows(mb):
        return x_ref[pl.ds(mb * mbat, mbat)].reshape(MM, D_MODEL)

    def rows_of(rb, n=MM):
        return pl.ds(pl.multiple_of(rb * n, n), n)

    def phase1a(rb, c):
        rows = rows_of(rb)
        x = x_rows(rb)
        hn = x * lax.rsqrt(jnp.mean(x * x, axis=-1, keepdims=True) + EPS) * ng_ref[...]
        hn = hn.astype(BF16)
        hn_ref[rows, :] = hn
        for sec in range(HG_COLS // 512):
            cs = slice(512 * sec, 512 * (sec + 1))
            proj_ref[rows, cs] = _dot(hn, win_ref[:, OFF_Q + 512 * sec:OFF_Q + 512 * (sec + 1)])
        return c
    lax.fori_loop(0, n_mb, phase1a, 0)

    def phase1b(rb, c):
        rows = rows_of(rb)
        hp = _dot(perm_ref[rows, :], hn_ref[...]).astype(BF16)
        for sec in range(2):
            uz_ref[rows, 512 * sec:512 * (sec + 1)] = _dot(hp, win_ref[:, 512 * sec:512 * (sec + 1)])
        for i in range(2):
            u_i = uz_ref[rows, 256 * i:256 * (i + 1)].astype(BF16)
            bu_ref[rows, 2 * S5_HALF * i:2 * S5_HALF * (i + 1)] = _dot(u_i, bcat_ref[i])
        return c
    lax.fori_loop(0, n_mb, phase1b, 0)

    width = min(S5_HALF, (SUBLANES * S5_HALF) // NB)
    for i in range(2):
        for off in range(0, S5_HALF, width):
            st = slice(S5_HALF * i + off, S5_HALF * i + off + width)
            re_c = 2 * S5_HALF * i + off
            im_c = re_c + S5_HALF
            ar = are_ref[:, st]
            ai = aim_ref[:, st]

            def step(t, carry, re_c=re_c, im_c=im_c, ar=ar, ai=ai):
                hr, hi = carry
                rows = pl.ds(pl.multiple_of(t * NB, NB), NB)
                nhr = ar * hr - ai * hi + bu_ref[rows, re_c:re_c + width]
                nhi = ar * hi + ai * hr + bu_ref[rows, im_c:im_c + width]
                bu_ref[rows, re_c:re_c + width] = nhr
                bu_ref[rows, im_c:im_c + width] = nhi
                return nhr, nhi
            hr, hi = lax.fori_loop(0, t_valid, step, (hr_ref[:, st], hi_ref[:, st]), unroll=2)
            hr_ref[:, st] = hr
            hi_ref[:, st] = hi

    def phase3(rb, c):
        rows = rows_of(rb)
        ys = []
        for i in range(2):
            hs = bu_ref[rows, 2 * S5_HALF * i:2 * S5_HALF * (i + 1)].astype(BF16)
            ys.append(_dot(hs, ccat_ref[i]))
        u = uz_ref[rows, 0:W_S5]
        y = jnp.concatenate(ys, axis=1) + d_ref[...] * u
        g = _gelu_tanh(y)
        glu = _dot(g.astype(BF16), wglu_ref[...]) + bglu_ref[...]
        zs = uz_ref[rows, W_S5:2 * W_S5]
        s5 = g * _sigmoid(glu) * (zs * _sigmoid(zs))
        s5_ref[rows, :] = s5.astype(BF16)
        return c
    lax.fori_loop(0, n_mb, phase3, 0)

    def phase3b(rb, c):
        rows = rows_of(rb)
        cat_ref[rows, 0:W_S5] = _dot(permt_ref[rows, :], s5_ref[...]).astype(BF16)
        return c
    lax.fori_loop(0, n_mb, phase3b, 0)

    row = lax.broadcasted_iota(jnp.int32, (ROW_BLOCK, ROW_BLOCK), 0)
    col = lax.broadcasted_iota(jnp.int32, (ROW_BLOCK, ROW_BLOCK), 1)
    tin = row & (C - 1)
    levels = []
    m = 1
    while m < C:
        levels.append(m)
        m *= 2
    lev = jnp.where(row == col, 0, -1)
    for li, m in enumerate(levels):
        same = (row & ~(2 * m - 1)) == (col & ~(2 * m - 1))
        own = same & ((row & m) != 0) & ((col & m) == 0)
        lev = jnp.where(own, li + 1, lev)
    lev_ref[...] = lev

    def intra(rb, c):
        rows = rows_of(rb, ROW_BLOCK)
        for hd in range(HG_HEADS):
            hc = slice(HG_DK * hd, HG_DK * (hd + 1))
            lb = lb_ref[:, hc]
            bs_ref = bsall_ref.at[hd]
            q = proj_ref[rows, HG_DK * hd:HG_DK * (hd + 1)]
            gf = proj_ref[rows, 512 + HG_DK * hd:512 + HG_DK * (hd + 1)]
            v = proj_ref[rows, 1024 + HG_DV * hd:1024 + HG_DV * (hd + 1)].astype(BF16)
            e = jnp.exp(-jnp.abs(gf))
            r = 1.0 / (1.0 + e)
            pos = gf >= 0.0
            sig_p = jnp.where(pos, r, e * r)
            sig_n = jnp.where(pos, e * r, r)
            f = lb + (1.0 - lb) * sig_p
            g = jnp.log(f)
            k = (1.0 - lb) * sig_n
            if t_valid < C:
                live = tin < t_valid
                g = jnp.where(live, g, 0.0)
                k = jnp.where(live, k, 0.0)
            b = g
            dd = 1
            while dd < C:
                b = b + jnp.where(tin >= dd, pltpu.roll(b, dd, axis=0), 0.0)
                dd *= 2
            bs_ref[...] = b
            lv = lev_ref[...]
            att = jnp.where(lv == 0, _dot_nt(q.astype(BF16), k.astype(BF16)), 0.0)
            for li, m in enumerate(levels):
                two_m = 2 * m
                p2 = row & (two_m - 1)
                if m == 1:
                    ref = jnp.where(p2 == 1, pltpu.roll(b, 1, axis=0), b)
                elif m == 2:
                    ref = jnp.where(p2 == 0, pltpu.roll(b, ROW_BLOCK - 1, axis=0),
                                    jnp.where(p2 == 1, b,
                                              jnp.where(p2 == 2, pltpu.roll(b, 1, axis=0),
                                                        pltpu.roll(b, 2, axis=0))))
                else:
                    ref = _row_bcast_blocks(bs_ref, two_m, m - 1)
                dec = jnp.exp2(jnp.abs(b - ref) * (-LOG2E))
                att = jnp.where(lv == li + 1, _dot_nt((q * dec).astype(BF16), (k * dec).astype(BF16)), att)
            o_ref[rows, hc] = _dot(att.astype(BF16), v)
            b_last = _row_bcast_blocks(bs_ref, C, C - 1)
            qe_ref[rows, hc] = (q * jnp.exp(b)).astype(qe_ref.dtype)
            ke_ref[rows, hc] = (k * jnp.exp(b_last - b)).astype(ke_ref.dtype)
            v_ref[rows, hc] = v.astype(v_ref.dtype)
            for n in range(nbat):
                last = bs_ref[n * C + C - 1:n * C + C, :]
                dl_ref[rb * nbat + n, :, hc] = jnp.broadcast_to(jnp.exp(last), (SUBLANES, HG_DK))
        return c
    lax.fori_loop(0, n_rb, intra, 0)

    def inter(bi, c):
        rows = pl.ds(pl.multiple_of(bi * C, C), C)
        for hd in range(HG_HEADS):
            hc = slice(HG_DK * hd, HG_DK * (hd + 1))
            st = s_ref[bi, hd]
            o_ref[rows, hc] = o_ref[rows, hc] + _dot_nt(qe_ref[rows, hc], st.astype(qe_ref.dtype))
            upd = _dot_tn(v_ref[rows, hc], ke_ref[rows, hc])
            s_ref[bi, hd] = st * dl_ref[bi, 0:1, hc] + upd
        return c
    lax.fori_loop(0, NB, inter, 0)

    def phase5(rb, c):
        rows = rows_of(rb)
        for hd in range(HG_HEADS):
            hc = slice(HG_DV * hd, HG_DV * (hd + 1))
            o = o_ref[rows, hc]
            o = o * lax.rsqrt(jnp.mean(o * o, axis=-1, keepdims=True) + EPS)
            zh = proj_ref[rows, 1536 + HG_DV * hd:1536 + HG_DV * (hd + 1)]
            hg = o * og_ref[:, hc] * (zh * _sigmoid(zh))
            cat_ref[rows, W_S5 + HG_DV * hd:W_S5 + HG_DV * (hd + 1)] = hg.astype(BF16)
        out = _dot(cat_ref[rows, :], wout_ref[...])
        xn = x_rows(rb) + out
        yv = xn * lax.rsqrt(jnp.mean(xn * xn, axis=-1, keepdims=True) + EPS) * fg_ref[...]
        y_ref[pl.ds(rb * mbat, mbat)] = yv.reshape(mbat, C, D_MODEL)
        return c
    lax.fori_loop(0, n_mb, phase5, 0)

    @pl.when(j == nj - 1)
    def _fin():
        def tr(b, c):
            for hd in range(HG_HEADS):
                s_ref[b, hd] = s_ref[b, hd].T
            return c
        lax.fori_loop(0, NB, tr, 0)


def _time_major_perm(NB, C):
    R = NB * C
    dst = np.arange(R)
    src = (dst % NB) * C + dst // NB
    p = np.zeros((R, R), np.float32)
    p[dst, src] = 1.0
    return p


def _mixer_call(x, state, weights, *, NB, C, t_valid, name):
    B, L, _ = x.shape
    has_state = state is not None
    grid = (B // NB, L // C)
    R = NB * C
    perm = _time_major_perm(NB, C)
    weights = (jnp.asarray(perm, BF16), jnp.asarray(perm.T, BF16)) + tuple(weights)

    def full(a):
        nd = a.ndim
        return pl.BlockSpec(a.shape, lambda i, j, nd=nd: (0,) * nd, pipeline_mode=pl.Buffered(1))

    x_spec = pl.BlockSpec((NB, C, D_MODEL), lambda i, j: (i, j, 0))
    h_spec = pl.BlockSpec((NB, S5_LANES), lambda i, j: (i, 0))
    s_spec = pl.BlockSpec((NB, HG_HEADS, HG_DK, HG_DV), lambda i, j: (i, 0, 0, 0))
    in_specs = [x_spec] + ([h_spec, h_spec, s_spec] if has_state else []) + [full(w) for w in weights]
    args = [x] + (list(state) if has_state else []) + list(weights)
    out_shape = [
        jax.ShapeDtypeStruct((B, L, D_MODEL), x.dtype),
        jax.ShapeDtypeStruct((B, S5_LANES), F32),
        jax.ShapeDtypeStruct((B, S5_LANES), F32),
        jax.ShapeDtypeStruct((B, HG_HEADS, HG_DK, HG_DV), F32),
    ]
    chunk_dt = BF16 if C % BF16_TILE_ROWS == 0 else F32
    scratch = [
        pltpu.VMEM((R, D_MODEL), BF16),
        pltpu.VMEM((R, HG_COLS), F32),
        pltpu.VMEM((R, 2 * W_S5), F32),
        pltpu.VMEM((R, 2 * S5_LANES), F32),
        pltpu.VMEM((R, W_S5), BF16),
        pltpu.VMEM((R, W_HG), F32),
        pltpu.VMEM((R, W_HG), chunk_dt),
        pltpu.VMEM((R, W_HG), chunk_dt),
        pltpu.VMEM((R, W_HG), chunk_dt),
        pltpu.VMEM((HG_HEADS, ROW_BLOCK, HG_DK), F32),
        pltpu.VMEM((ROW_BLOCK, ROW_BLOCK), jnp.int32),
        pltpu.VMEM((NB, SUBLANES, W_HG), F32),
        pltpu.VMEM((R, D_MODEL), BF16),
    ]
    return pl.pallas_call(
        functools.partial(_mixer_kernel, NB=NB, C=C, t_valid=t_valid, has_state=has_state),
        grid=grid,
        in_specs=in_specs,
        out_specs=[x_spec, h_spec, h_spec, s_spec],
        out_shape=out_shape,
        scratch_shapes=scratch,
        compiler_params=pltpu.CompilerParams(
            dimension_semantics=("arbitrary", "arbitrary"),
            vmem_limit_bytes=V7X_VMEM_LIMIT),
        name=name,
    )(*args)


def kernel(x_prompt, x_sample, state_s5_re, state_s5_im, state_hgrn, norm_g, w_in, s5_lambda_re, s5_lambda_im,
           s5_log_step, s5_b_re, s5_b_im, s5_c_re, s5_c_im, s5_d, w_glu, b_glu, hgrn_lb_logits, hgrn_onorm_g,
           w_out, final_norm_g):
    assert norm_g.shape[0] == 1, "single-layer model"
    l = 0
    a_re, a_im, bbre, bbim, lb = _prepare_params(
        s5_lambda_re[l], s5_lambda_im[l], s5_log_step[l], s5_b_re[l], s5_b_im[l], hgrn_lb_logits)
    bcat, ccat = _block_diag_weights(bbre, bbim, s5_c_re[l], s5_c_im[l])
    weights = (
        norm_g[l].astype(F32).reshape(1, D_MODEL),
        w_in[l].astype(BF16),
        a_re, a_im, bcat, ccat,
        s5_d[l].astype(F32).reshape(1, W_S5),
        w_glu[l].astype(BF16),
        b_glu[l].astype(F32).reshape(1, W_S5),
        lb,
        hgrn_onorm_g[l].astype(F32).reshape(1, W_HG),
        w_out[l].astype(BF16),
        final_norm_g.astype(F32).reshape(1, D_MODEL),
    )

    Bp, Lp, _ = x_prompt.shape
    yp, p_re, p_im, p_hg = _mixer_call(x_prompt, None, weights, NB=Bp, C=64, t_valid=64, name="mixer_prompt")

    Bs, Ls, _ = x_sample.shape
    xs = jnp.pad(x_sample, ((0, 0), (0, SAMPLE_PAD_SEQ - Ls), (0, 0)))
    state = (state_s5_re[l].reshape(Bs, S5_LANES).astype(F32),
             state_s5_im[l].reshape(Bs, S5_LANES).astype(F32),
             state_hgrn[l].astype(F32))
    ys, s_re, s_im, s_hg = _mixer_call(xs, state, weights, NB=16, C=SAMPLE_PAD_SEQ, t_valid=Ls, name="mixer_sample")
    ys = ys[:, :Ls]

    sd = state_s5_re.dtype
    shp = (1, Bp, S5_GROUPS, S5_STATE)
    shs = (1, Bs, S5_GROUPS, S5_STATE)
    return (yp, ys,
            p_re.reshape(shp).astype(sd), p_im.reshape(shp).astype(sd), p_hg[None].astype(state_hgrn.dtype),
            s_re.reshape(shs).astype(sd), s_im.reshape(shs).astype(sd), s_hg[None].astype(state_hgrn.dtype))
```

```python
import functools
import math

import jax
import jax.numpy as jnp
import numpy as np
from jax import lax
from jax.experimental import pallas as pl
from jax.experimental.pallas import tpu as pltpu

F32 = jnp.float32
BF16 = jnp.bfloat16

D_MODEL = 1024
W_S5 = 512
W_HG = 512
S5_CH = 16
S5_GROUPS = 32
S5_STATE = 64
S5_LANES = S5_GROUPS * S5_STATE
S5_HALF = S5_LANES // 2
HG_DK = 128
HG_HEADS = 4
HG_DV = 128
EPS = 1e-6
LAMBDA_RE_MAX = -1e-4
PROJ_OUT = 3072
OFF_Q = 1024
HG_COLS = PROJ_OUT - OFF_Q
LOG2E = 1.4426950408889634

LANES = 128
SUBLANES = 8
BF16_TILE_ROWS = 16
ROW_BLOCK = 128
MATMUL_ROWS = 256
V7X_VMEM_LIMIT = 60 * 1024 * 1024
SAMPLE_PAD_SEQ = 8


def _prep_kernel(lre_ref, lim_ref, ls_ref, brt_ref, bit_ref, lg_ref,
                 are_ref, aim_ref, bbre_ref, bbim_ref, lb_ref):
    lr = jnp.minimum(lre_ref[...], LAMBDA_RE_MAX)
    li = lim_ref[...]
    dt = jnp.exp(ls_ref[...])
    er = jnp.exp(lr * dt)
    a_re = er * jnp.cos(li * dt)
    a_im = er * jnp.sin(li * dt)
    den = lr * lr + li * li
    zr = ((a_re - 1.0) * lr + a_im * li) / den
    zi = (a_im * lr - (a_re - 1.0) * li) / den
    br = brt_ref[...]
    bi = bit_ref[...]
    are_ref[...] = a_re
    aim_ref[...] = a_im
    bbre_ref[...] = zr * br - zi * bi
    bbim_ref[...] = zr * bi + zi * br
    lg = lg_ref[...]
    e = jnp.exp(lg - jnp.max(lg, axis=0, keepdims=True))
    lb_ref[...] = e[0:1, :] / jnp.sum(e, axis=0, keepdims=True)


def _prepare_params(lam_re, lam_im, log_step, b_re, b_im, lb_logits):
    gc = S5_GROUPS * S5_CH
    bc = lambda a: jnp.broadcast_to(a[:, None, :], (S5_GROUPS, S5_CH, S5_STATE)).reshape(gc, S5_STATE)
    lre = bc(lam_re.astype(F32))
    lim = bc(lam_im.astype(F32))
    ls = bc(jnp.broadcast_to(log_step.astype(F32)[:, None], (S5_GROUPS, S5_STATE)))
    brt = jnp.transpose(b_re.astype(F32), (0, 2, 1)).reshape(gc, S5_STATE)
    bit = jnp.transpose(b_im.astype(F32), (0, 2, 1)).reshape(gc, S5_STATE)
    outs = pl.pallas_call(
        _prep_kernel,
        out_shape=[jax.ShapeDtypeStruct((gc, S5_STATE), F32)] * 4
        + [jax.ShapeDtypeStruct((1, HG_HEADS * HG_DK), F32)],
        name="s5_hgrn_prep",
    )(lre, lim, ls, brt, bit, lb_logits.astype(F32))
    a_re_f, a_im_f, bbre, bbim, lb = outs
    a_re = a_re_f.reshape(S5_GROUPS, S5_CH, S5_STATE)[:, 0, :].reshape(1, S5_LANES)
    a_im = a_im_f.reshape(S5_GROUPS, S5_CH, S5_STATE)[:, 0, :].reshape(1, S5_LANES)
    return a_re, a_im, bbre, bbim, lb


def _block_diag_weights(bbre, bbim, c_re, c_im):
    eye = jnp.eye(S5_GROUPS, dtype=F32)[:, None, :, None]

    def in_full(bbt):
        return (bbt.reshape(S5_GROUPS, S5_CH, 1, S5_STATE) * eye).reshape(W_S5, S5_LANES)

    def out_full(c):
        ct = jnp.transpose(c.astype(F32), (0, 2, 1)).reshape(S5_GROUPS, S5_STATE, 1, S5_CH)
        return (ct * eye).reshape(S5_LANES, W_S5)

    bre_f, bim_f = in_full(bbre), in_full(bbim)
    cre_f, cim_f = out_full(c_re), out_full(-c_im)
    bcat, ccat = [], []
    for i in range(2):
        rs = slice(256 * i, 256 * (i + 1))
        cs = slice(S5_HALF * i, S5_HALF * (i + 1))
        bcat.append(jnp.concatenate([bre_f[rs, cs], bim_f[rs, cs]], axis=1))
        ccat.append(jnp.concatenate([cre_f[cs, rs], cim_f[cs, rs]], axis=0))
    return jnp.stack(bcat).astype(BF16), jnp.stack(ccat).astype(BF16)


def _sigmoid(x):
    return 1.0 / (1.0 + jnp.exp(-x))


def _gelu_tanh(x):
    return 0.5 * x * (1.0 + jnp.tanh(math.sqrt(2.0 / math.pi) * (x + 0.044715 * (x * x * x))))


def _dot(a, b):
    return jnp.dot(a, b, preferred_element_type=F32)


def _dot_nt(a, b):
    return lax.dot_general(a, b, (((1,), (1,)), ((), ())), preferred_element_type=F32)


def _dot_tn(a, b):
    return lax.dot_general(a, b, (((0,), (0,)), ((), ())), preferred_element_type=F32)


def _row_bcast_blocks(src_ref, blk, off):
    parts = []
    for n in range(ROW_BLOCK // blk):
        row = src_ref[n * blk + off:n * blk + off + 1, :]
        parts.append(jnp.broadcast_to(row, (blk, HG_DK)))
    return parts[0] if len(parts) == 1 else jnp.concatenate(parts, axis=0)


def _mixer_kernel(*refs, NB, C, t_valid, has_state):
    if has_state:
        x_ref, h0r_ref, h0i_ref, s0_ref = refs[:4]
        refs = refs[4:]
    else:
        x_ref = refs[0]
        refs = refs[1:]
    (perm_ref, permt_ref, ng_ref, win_ref, are_ref, aim_ref, bcat_ref, ccat_ref, d_ref, wglu_ref, bglu_ref,
     lb_ref, og_ref, wout_ref, fg_ref,
     y_ref, hr_ref, hi_ref, s_ref,
     hn_ref, proj_ref, uz_ref, bu_ref, s5_ref, o_ref, qe_ref, ke_ref, v_ref, bsall_ref, lev_ref, tri_ref,
     dl_ref, cat_ref) = refs

    j = pl.program_id(1)
    nj = pl.num_programs(1)
    R = NB * C
    n_rb = R // ROW_BLOCK
    nbat = ROW_BLOCK // C

    @pl.when(j == 0)
    def _init():
        if has_state:
            hr_ref[...] = h0r_ref[...]
            hi_ref[...] = h0i_ref[...]

            def tr(b, c):
                for hd in range(HG_HEADS):
                    s_ref[b, hd] = s0_ref[b, hd].T
                return c
            lax.fori_loop(0, NB, tr, 0)
        else:
            hr_ref[...] = jnp.zeros(hr_ref.shape, F32)
            hi_ref[...] = jnp.zeros(hi_ref.shape, F32)
            s_ref[...] = jnp.zeros(s_ref.shape, F32)

    MM = min(R, MATMUL_ROWS)
    n_mb = R // MM
    mbat = MM // C

    def x_rows(mb):
        return x_ref[pl.ds(mb * mbat, mbat)].reshape(MM, D_MODEL)

    def rows_of(rb, n=MM):
        return pl.ds(pl.multiple_of(rb * n, n), n)

    def phase1a(rb, c):
        rows = rows_of(rb)
        x = x_rows(rb)
        hn = x * lax.rsqrt(jnp.mean(x * x, axis=-1, keepdims=True) + EPS) * ng_ref[...]
        hn = hn.astype(BF16)
        hn_ref[rows, :] = hn
        for sec in range(HG_COLS // 512):
            cs = slice(512 * sec, 512 * (sec + 1))
            proj_ref[rows, cs] = _dot(hn, win_ref[:, OFF_Q + 512 * sec:OFF_Q + 512 * (sec + 1)])
        return c
    lax.fori_loop(0, n_mb, phase1a, 0)

    def phase1b(rb, c):
        rows = rows_of(rb)
        hp = _dot(perm_ref[rows, :], hn_ref[...]).astype(BF16)
        for sec in range(2):
            uz_ref[rows, 512 * sec:512 * (sec + 1)] = _dot(hp, win_ref[:, 512 * sec:512 * (sec + 1)])
        for i in range(2):
            u_i = uz_ref[rows, 256 * i:256 * (i + 1)].astype(BF16)
            bu_ref[rows, 2 * S5_HALF * i:2 * S5_HALF * (i + 1)] = _dot(u_i, bcat_ref[i])
        return c
    lax.fori_loop(0, n_mb, phase1b, 0)

    width = min(S5_HALF, (SUBLANES * S5_HALF) // NB)
    for i in range(2):
        for off in range(0, S5_HALF, width):
            st = slice(S5_HALF * i + off, S5_HALF * i + off + width)
            re_c = 2 * S5_HALF * i + off
            im_c = re_c + S5_HALF
            ar = are_ref[:, st]
            ai = aim_ref[:, st]

            def step(t, carry, re_c=re_c, im_c=im_c, ar=ar, ai=ai):
                hr, hi = carry
                rows = pl.ds(pl.multiple_of(t * NB, NB), NB)
                nhr = ar * hr - ai * hi + bu_ref[rows, re_c:re_c + width]
                nhi = ar * hi + ai * hr + bu_ref[rows, im_c:im_c + width]
                bu_ref[rows, re_c:re_c + width] = nhr
                bu_ref[rows, im_c:im_c + width] = nhi
                return nhr, nhi
            hr, hi = lax.fori_loop(0, t_valid, step, (hr_ref[:, st], hi_ref[:, st]), unroll=2)
            hr_ref[:, st] = hr
            hi_ref[:, st] = hi

    def phase3(rb, c):
        rows = rows_of(rb)
        ys = []
        for i in range(2):
            hs = bu_ref[rows, 2 * S5_HALF * i:2 * S5_HALF * (i + 1)].astype(BF16)
            ys.append(_dot(hs, ccat_ref[i]))
        u = uz_ref[rows, 0:W_S5]
        y = jnp.concatenate(ys, axis=1) + d_ref[...] * u
        g = _gelu_tanh(y)
        glu = _dot(g.astype(BF16), wglu_ref[...]) + bglu_ref[...]
        zs = uz_ref[rows, W_S5:2 * W_S5]
        s5 = g * _sigmoid(glu) * (zs * _sigmoid(zs))
        s5_ref[rows, :] = s5.astype(BF16)
        return c
    lax.fori_loop(0, n_mb, phase3, 0)

    def phase3b(rb, c):
        rows = rows_of(rb)
        cat_ref[rows, 0:W_S5] = _dot(permt_ref[rows, :], s5_ref[...]).astype(BF16)
        return c
    lax.fori_loop(0, n_mb, phase3b, 0)

    row = lax.broadcasted_iota(jnp.int32, (ROW_BLOCK, ROW_BLOCK), 0)
    col = lax.broadcasted_iota(jnp.int32, (ROW_BLOCK, ROW_BLOCK), 1)
    tin = row & (C - 1)
    levels = []
    m = 1
    while m < C:
        levels.append(m)
        m *= 2
    lev = jnp.where(row == col, 0, -1)
    for li, m in enumerate(levels):
        same = (row & ~(2 * m - 1)) == (col & ~(2 * m - 1))
        own = same & ((row & m) != 0) & ((col & m) == 0)
        lev = jnp.where(own, li + 1, lev)
    lev_ref[...] = lev
    in_chunk = (row & ~(C - 1)) == (col & ~(C - 1))
    tri_ref[...] = jnp.where(in_chunk & (col <= row), 1.0, 0.0).astype(BF16)

    heads = range(HG_HEADS)
    hcs = [slice(HG_DK * hd, HG_DK * (hd + 1)) for hd in heads]

    def intra(rb, c):
        rows = rows_of(rb, ROW_BLOCK)
        lv = lev_ref[...]
        qs, ks, fs, vs, gcats = [], [], [], [], []
        for hd in heads:
            lb = lb_ref[:, hcs[hd]]
            q = proj_ref[rows, HG_DK * hd:HG_DK * (hd + 1)]
            gf = proj_ref[rows, 512 + HG_DK * hd:512 + HG_DK * (hd + 1)]
            v = proj_ref[rows, 1024 + HG_DV * hd:1024 + HG_DV * (hd + 1)].astype(BF16)
            e = jnp.exp(-jnp.abs(gf))
            r = 1.0 / (1.0 + e)
            pos = gf >= 0.0
            sig_p = jnp.where(pos, r, e * r)
            sig_n = jnp.where(pos, e * r, r)
            f = lb + (1.0 - lb) * sig_p
            g = jnp.log(f)
            k = (1.0 - lb) * sig_n
            if t_valid < C:
                live = tin < t_valid
                f = jnp.where(live, f, 1.0)
                g = jnp.where(live, g, 0.0)
                k = jnp.where(live, k, 0.0)
            g1 = g.astype(BF16)
            r1 = g - g1.astype(F32)
            g2 = r1.astype(BF16)
            g3 = (r1 - g2.astype(F32)).astype(BF16)
            qs.append(q)
            ks.append(k)
            fs.append(f)
            vs.append(v)
            gcats.append(jnp.concatenate([g1, g2, g3], axis=1))
        bparts = [_dot(tri_ref[...], gcats[hd]) for hd in heads]
        bs = []
        for hd in heads:
            bp = bparts[hd]
            b = (bp[:, 0:HG_DK] + bp[:, HG_DK:2 * HG_DK]) + bp[:, 2 * HG_DK:3 * HG_DK]
            bsall_ref[hd] = b
            bs.append(b)
        qbs = [q.astype(BF16) for q in qs]
        kbs = [k.astype(BF16) for k in ks]
        atts = [jnp.where(lv == 0, _dot_nt(qbs[hd], kbs[hd]), 0.0) for hd in heads]
        for li, m in enumerate(levels):
            two_m = 2 * m
            p2 = row & (two_m - 1)
            for hd in heads:
                b = bs[hd]
                if m == 1:
                    dec = jnp.where(p2 == 1, fs[hd], 1.0)
                else:
                    if m == 2:
                        ref = jnp.where(p2 == 0, pltpu.roll(b, ROW_BLOCK - 1, axis=0),
                                        jnp.where(p2 == 1, b,
                                                  jnp.where(p2 == 2, pltpu.roll(b, 1, axis=0),
                                                            pltpu.roll(b, 2, axis=0))))
                    else:
                        ref = _row_bcast_blocks(bsall_ref.at[hd], two_m, m - 1)
                    dec = jnp.exp2(jnp.abs(b - ref) * (-LOG2E))
                decb = dec.astype(BF16)
                atts[hd] = jnp.where(lv == li + 1, _dot_nt(qbs[hd] * decb, kbs[hd] * decb), atts[hd])
        for hd in heads:
            o_ref[rows, hcs[hd]] = _dot(atts[hd].astype(BF16), vs[hd])
        for hd in heads:
            hc = hcs[hd]
            bs_ref = bsall_ref.at[hd]
            b = bs[hd]
            b_last = _row_bcast_blocks(bs_ref, C, C - 1)
            qe_ref[rows, hc] = (qs[hd] * jnp.exp(b)).astype(qe_ref.dtype)
            ke_ref[rows, hc] = (ks[hd] * jnp.exp(b_last - b)).astype(ke_ref.dtype)
            v_ref[rows, hc] = vs[hd].astype(v_ref.dtype)
            for n in range(nbat):
                last = bs_ref[n * C + C - 1:n * C + C, :]
                dl_ref[rb * nbat + n, :, hc] = jnp.broadcast_to(jnp.exp(last), (SUBLANES, HG_DK))
        return c
    lax.fori_loop(0, n_rb, intra, 0)

    def inter(bi, c):
        rows = pl.ds(pl.multiple_of(bi * C, C), C)
        for hd in range(HG_HEADS):
            hc = slice(HG_DK * hd, HG_DK * (hd + 1))
            st = s_ref[bi, hd]
            o_ref[rows, hc] = o_ref[rows, hc] + _dot_nt(qe_ref[rows, hc], st.astype(qe_ref.dtype))
            upd = _dot_tn(v_ref[rows, hc], ke_ref[rows, hc])
            s_ref[bi, hd] = st * dl_ref[bi, 0:1, hc] + upd
        return c
    lax.fori_loop(0, NB, inter, 0, unroll=2)

    def phase5(rb, c):
        rows = rows_of(rb)
        for hd in range(HG_HEADS):
            hc = slice(HG_DV * hd, HG_DV * (hd + 1))
            o = o_ref[rows, hc]
            o = o * lax.rsqrt(jnp.mean(o * o, axis=-1, keepdims=True) + EPS)
            zh = proj_ref[rows, 1536 + HG_DV * hd:1536 + HG_DV * (hd + 1)]
            hg = o * og_ref[:, hc] * (zh * _sigmoid(zh))
            cat_ref[rows, W_S5 + HG_DV * hd:W_S5 + HG_DV * (hd + 1)] = hg.astype(BF16)
        out = _dot(cat_ref[rows, :], wout_ref[...])
        xn = x_rows(rb) + out
        yv = xn * lax.rsqrt(jnp.mean(xn * xn, axis=-1, keepdims=True) + EPS) * fg_ref[...]
        y_ref[pl.ds(rb * mbat, mbat)] = yv.reshape(mbat, C, D_MODEL)
        return c
    lax.fori_loop(0, n_mb, phase5, 0)

    @pl.when(j == nj - 1)
    def _fin():
        def tr(b, c):
            for hd in range(HG_HEADS):
                s_ref[b, hd] = s_ref[b, hd].T
            return c
        lax.fori_loop(0, NB, tr, 0)


def _time_major_perm(NB, C):
    R = NB * C
    dst = np.arange(R)
    src = (dst % NB) * C + dst // NB
    p = np.zeros((R, R), np.float32)
    p[dst, src] = 1.0
    return p


def _mixer_call(x, state, weights, *, NB, C, t_valid, name):
    B, L, _ = x.shape
    has_state = state is not None
    grid = (B // NB, L // C)
    R = NB * C
    perm = _time_major_perm(NB, C)
    weights = (jnp.asarray(perm, BF16), jnp.asarray(perm.T, BF16)) + tuple(weights)

    def full(a):
        nd = a.ndim
        return pl.BlockSpec(a.shape, lambda i, j, nd=nd: (0,) * nd, pipeline_mode=pl.Buffered(1))

    x_spec = pl.BlockSpec((NB, C, D_MODEL), lambda i, j: (i, j, 0))
    h_spec = pl.BlockSpec((NB, S5_LANES), lambda i, j: (i, 0))
    s_spec = pl.BlockSpec((NB, HG_HEADS, HG_DK, HG_DV), lambda i, j: (i, 0, 0, 0))
    in_specs = [x_spec] + ([h_spec, h_spec, s_spec] if has_state else []) + [full(w) for w in weights]
    args = [x] + (list(state) if has_state else []) + list(weights)
    out_shape = [
        jax.ShapeDtypeStruct((B, L, D_MODEL), x.dtype),
        jax.ShapeDtypeStruct((B, S5_LANES), F32),
        jax.ShapeDtypeStruct((B, S5_LANES), F32),
        jax.ShapeDtypeStruct((B, HG_HEADS, HG_DK, HG_DV), F32),
    ]
    chunk_dt = BF16 if C % BF16_TILE_ROWS == 0 else F32
    scratch = [
        pltpu.VMEM((R, D_MODEL), BF16),
        pltpu.VMEM((R, HG_COLS), F32),
        pltpu.VMEM((R, 2 * W_S5), F32),
        pltpu.VMEM((R, 2 * S5_LANES), F32),
        pltpu.VMEM((R, W_S5), BF16),
        pltpu.VMEM((R, W_HG), F32),
        pltpu.VMEM((R, W_HG), chunk_dt),
        pltpu.VMEM((R, W_HG), chunk_dt),
        pltpu.VMEM((R, W_HG), chunk_dt),
        pltpu.VMEM((HG_HEADS, ROW_BLOCK, HG_DK), F32),
        pltpu.VMEM((ROW_BLOCK, ROW_BLOCK), jnp.int32),
        pltpu.VMEM((ROW_BLOCK, ROW_BLOCK), BF16),
        pltpu.VMEM((NB, SUBLANES, W_HG), F32),
        pltpu.VMEM((R, D_MODEL), BF16),
    ]
    return pl.pallas_call(
        functools.partial(_mixer_kernel, NB=NB, C=C, t_valid=t_valid, has_state=has_state),
        grid=grid,
        in_specs=in_specs,
        out_specs=[x_spec, h_spec, h_spec, s_spec],
        out_shape=out_shape,
        scratch_shapes=scratch,
        compiler_params=pltpu.CompilerParams(
            dimension_semantics=("arbitrary", "arbitrary"),
            vmem_limit_bytes=V7X_VMEM_LIMIT),
        name=name,
    )(*args)


def kernel(x_prompt, x_sample, state_s5_re, state_s5_im, state_hgrn, norm_g, w_in, s5_lambda_re, s5_lambda_im,
           s5_log_step, s5_b_re, s5_b_im, s5_c_re, s5_c_im, s5_d, w_glu, b_glu, hgrn_lb_logits, hgrn_onorm_g,
           w_out, final_norm_g):
    assert norm_g.shape[0] == 1, "single-layer model"
    l = 0
    a_re, a_im, bbre, bbim, lb = _prepare_params(
        s5_lambda_re[l], s5_lambda_im[l], s5_log_step[l], s5_b_re[l], s5_b_im[l], hgrn_lb_logits)
    bcat, ccat = _block_diag_weights(bbre, bbim, s5_c_re[l], s5_c_im[l])
    weights = (
        norm_g[l].astype(F32).reshape(1, D_MODEL),
        w_in[l].astype(BF16),
        a_re, a_im, bcat, ccat,
        s5_d[l].astype(F32).reshape(1, W_S5),
        w_glu[l].astype(BF16),
        b_glu[l].astype(F32).reshape(1, W_S5),
        lb,
        hgrn_onorm_g[l].astype(F32).reshape(1, W_HG),
        w_out[l].astype(BF16),
        final_norm_g.astype(F32).reshape(1, D_MODEL),
    )

    Bp, Lp, _ = x_prompt.shape
    yp, p_re, p_im, p_hg = _mixer_call(x_prompt, None, weights, NB=Bp, C=64, t_valid=64, name="mixer_prompt")

    Bs, Ls, _ = x_sample.shape
    xs = jnp.pad(x_sample, ((0, 0), (0, SAMPLE_PAD_SEQ - Ls), (0, 0)))
    state = (state_s5_re[l].reshape(Bs, S5_LANES).astype(F32),
             state_s5_im[l].reshape(Bs, S5_LANES).astype(F32),
             state_hgrn[l].astype(F32))
    ys, s_re, s_im, s_hg = _mixer_call(xs, state, weights, NB=16, C=SAMPLE_PAD_SEQ, t_valid=Ls, name="mixer_sample")
    ys = ys[:, :Ls]

    sd = state_s5_re.dtype
    shp = (1, Bp, S5_GROUPS, S5_STATE)
    shs = (1, Bs, S5_GROUPS, S5_STATE)
    return (yp, ys,
            p_re.reshape(shp).astype(sd), p_im.reshape(shp).astype(sd), p_hg[None].astype(state_hgrn.dtype),
            s_re.reshape(shs).astype(sd), s_im.reshape(shs).astype(sd), s_hg[None].astype(state_hgrn.dtype))
```

```python
import functools
import math

import jax
import jax.numpy as jnp
import numpy as np
from jax import lax
from jax.experimental import pallas as pl
from jax.experimental.pallas import tpu as pltpu

F32 = jnp.float32
BF16 = jnp.bfloat16

D_MODEL = 1024
W_S5 = 512
W_HG = 512
S5_CH = 16
S5_GROUPS = 32
S5_STATE = 64
S5_LANES = S5_GROUPS * S5_STATE
S5_HALF = S5_LANES // 2
HG_DK = 128
HG_HEADS = 4
HG_DV = 128
EPS = 1e-6
LAMBDA_RE_MAX = -1e-4
PROJ_OUT = 3072
OFF_Q = 1024
HG_COLS = PROJ_OUT - OFF_Q
LOG2E = 1.4426950408889634

LANES = 128
SUBLANES = 8
BF16_TILE_ROWS = 16
ROW_BLOCK = 128
MATMUL_ROWS = 256
V7X_VMEM_LIMIT = 60 * 1024 * 1024
SAMPLE_PAD_SEQ = 8


def _prep_kernel(lre_ref, lim_ref, ls_ref, brt_ref, bit_ref, lg_ref,
                 are_ref, aim_ref, bbre_ref, bbim_ref, lb_ref):
    lr = jnp.minimum(lre_ref[...], LAMBDA_RE_MAX)
    li = lim_ref[...]
    dt = jnp.exp(ls_ref[...])
    er = jnp.exp(lr * dt)
    a_re = er * jnp.cos(li * dt)
    a_im = er * jnp.sin(li * dt)
    den = lr * lr + li * li
    zr = ((a_re - 1.0) * lr + a_im * li) / den
    zi = (a_im * lr - (a_re - 1.0) * li) / den
    br = brt_ref[...]
    bi = bit_ref[...]
    are_ref[...] = a_re
    aim_ref[...] = a_im
    bbre_ref[...] = zr * br - zi * bi
    bbim_ref[...] = zr * bi + zi * br
    lg = lg_ref[...]
    e = jnp.exp(lg - jnp.max(lg, axis=0, keepdims=True))
    lb_ref[...] = e[0:1, :] / jnp.sum(e, axis=0, keepdims=True)


def _prepare_params(lam_re, lam_im, log_step, b_re, b_im, lb_logits):
    gc = S5_GROUPS * S5_CH
    bc = lambda a: jnp.broadcast_to(a[:, None, :], (S5_GROUPS, S5_CH, S5_STATE)).reshape(gc, S5_STATE)
    lre = bc(lam_re.astype(F32))
    lim = bc(lam_im.astype(F32))
    ls = bc(jnp.broadcast_to(log_step.astype(F32)[:, None], (S5_GROUPS, S5_STATE)))
    brt = jnp.transpose(b_re.astype(F32), (0, 2, 1)).reshape(gc, S5_STATE)
    bit = jnp.transpose(b_im.astype(F32), (0, 2, 1)).reshape(gc, S5_STATE)
    outs = pl.pallas_call(
        _prep_kernel,
        out_shape=[jax.ShapeDtypeStruct((gc, S5_STATE), F32)] * 4
        + [jax.ShapeDtypeStruct((1, HG_HEADS * HG_DK), F32)],
        name="s5_hgrn_prep",
    )(lre, lim, ls, brt, bit, lb_logits.astype(F32))
    a_re_f, a_im_f, bbre, bbim, lb = outs
    a_re = a_re_f.reshape(S5_GROUPS, S5_CH, S5_STATE)[:, 0, :].reshape(1, S5_LANES)
    a_im = a_im_f.reshape(S5_GROUPS, S5_CH, S5_STATE)[:, 0, :].reshape(1, S5_LANES)
    return a_re, a_im, bbre, bbim, lb


def _block_diag_weights(bbre, bbim, c_re, c_im):
    eye = jnp.eye(S5_GROUPS, dtype=F32)[:, None, :, None]

    def in_full(bbt):
        return (bbt.reshape(S5_GROUPS, S5_CH, 1, S5_STATE) * eye).reshape(W_S5, S5_LANES)

    def out_full(c):
        ct = jnp.transpose(c.astype(F32), (0, 2, 1)).reshape(S5_GROUPS, S5_STATE, 1, S5_CH)
        return (ct * eye).reshape(S5_LANES, W_S5)

    bre_f, bim_f = in_full(bbre), in_full(bbim)
    cre_f, cim_f = out_full(c_re), out_full(-c_im)
    bcat, ccat = [], []
    for i in range(2):
        rs = slice(256 * i, 256 * (i + 1))
        cs = slice(S5_HALF * i, S5_HALF * (i + 1))
        bcat.append(jnp.concatenate([bre_f[rs, cs], bim_f[rs, cs]], axis=1))
        ccat.append(jnp.concatenate([cre_f[cs, rs], cim_f[cs, rs]], axis=0))
    return jnp.stack(bcat).astype(BF16), jnp.stack(ccat).astype(BF16)


def _sigmoid(x):
    return 1.0 / (1.0 + jnp.exp2(x * (-LOG2E)))


def _gelu_tanh(x):
    return 0.5 * x * (1.0 + jnp.tanh(math.sqrt(2.0 / math.pi) * (x + 0.044715 * (x * x * x))))


def _dot(a, b):
    return jnp.dot(a, b, preferred_element_type=F32)


def _dot_nt(a, b):
    return lax.dot_general(a, b, (((1,), (1,)), ((), ())), preferred_element_type=F32)


def _dot_tn(a, b):
    return lax.dot_general(a, b, (((0,), (0,)), ((), ())), preferred_element_type=F32)


def _row_bcast_blocks(src_ref, blk, off):
    parts = []
    for n in range(ROW_BLOCK // blk):
        row = src_ref[n * blk + off:n * blk + off + 1, :]
        parts.append(jnp.broadcast_to(row, (blk, HG_DK)))
    return parts[0] if len(parts) == 1 else jnp.concatenate(parts, axis=0)


def _mixer_kernel(*refs, NB, C, t_valid, has_state):
    if has_state:
        x_ref, h0r_ref, h0i_ref, s0_ref = refs[:4]
        refs = refs[4:]
    else:
        x_ref = refs[0]
        refs = refs[1:]
    (perm_ref, permt_ref, ng_ref, win_ref, are_ref, aim_ref, bcat_ref, ccat_ref, d_ref, wglu_ref, bglu_ref,
     lb_ref, og_ref, wout_ref, fg_ref,
     y_ref, hr_ref, hi_ref, s_ref,
     hn_ref, proj_ref, uz_ref, bu_ref, s5_ref, o_ref, qe_ref, ke_ref, v_ref, bsall_ref, lev_ref, tri_ref,
     dl_ref, cat_ref) = refs

    j = pl.program_id(1)
    nj = pl.num_programs(1)
    R = NB * C
    n_rb = R // ROW_BLOCK
    nbat = ROW_BLOCK // C

    @pl.when(j == 0)
    def _init():
        if has_state:
            hr_ref[...] = h0r_ref[...]
            hi_ref[...] = h0i_ref[...]
        else:
            hr_ref[...] = jnp.zeros(hr_ref.shape, F32)
            hi_ref[...] = jnp.zeros(hi_ref.shape, F32)
            s_ref[...] = jnp.zeros(s_ref.shape, F32)

    MM = min(R, MATMUL_ROWS)
    n_mb = R // MM
    mbat = MM // C

    def x_rows(mb):
        return x_ref[pl.ds(mb * mbat, mbat)].reshape(MM, D_MODEL)

    def rows_of(rb, n=MM):
        return pl.ds(pl.multiple_of(rb * n, n), n)

    def phase1a(rb, c):
        rows = rows_of(rb)
        x = x_rows(rb)
        hn = x * lax.rsqrt(jnp.mean(x * x, axis=-1, keepdims=True) + EPS) * ng_ref[...]
        hn = hn.astype(BF16)
        hn_ref[rows, :] = hn
        for sec in range(HG_COLS // 512):
            cs = slice(512 * sec, 512 * (sec + 1))
            proj_ref[rows, cs] = _dot(hn, win_ref[:, OFF_Q + 512 * sec:OFF_Q + 512 * (sec + 1)])
        return c
    lax.fori_loop(0, n_mb, phase1a, 0)

    def phase1b(rb, c):
        rows = rows_of(rb)
        hp = _dot(perm_ref[rows, :], hn_ref[...]).astype(BF16)
        for sec in range(2):
            uz_ref[rows, 512 * sec:512 * (sec + 1)] = _dot(hp, win_ref[:, 512 * sec:512 * (sec + 1)])
        for i in range(2):
            u_i = uz_ref[rows, 256 * i:256 * (i + 1)].astype(BF16)
            bu_ref[rows, 2 * S5_HALF * i:2 * S5_HALF * (i + 1)] = _dot(u_i, bcat_ref[i])
        return c
    lax.fori_loop(0, n_mb, phase1b, 0)

    width = min(S5_HALF, (SUBLANES * S5_HALF) // NB)
    for i in range(2):
        for off in range(0, S5_HALF, width):
            st = slice(S5_HALF * i + off, S5_HALF * i + off + width)
            re_c = 2 * S5_HALF * i + off
            im_c = re_c + S5_HALF
            ar = are_ref[:, st]
            ai = aim_ref[:, st]

            def step(t, carry, re_c=re_c, im_c=im_c, ar=ar, ai=ai):
                hr, hi = carry
                rows = pl.ds(pl.multiple_of(t * NB, NB), NB)
                nhr = ar * hr - ai * hi + bu_ref[rows, re_c:re_c + width]
                nhi = ar * hi + ai * hr + bu_ref[rows, im_c:im_c + width]
                bu_ref[rows, re_c:re_c + width] = nhr
                bu_ref[rows, im_c:im_c + width] = nhi
                return nhr, nhi
            hr, hi = lax.fori_loop(0, t_valid, step, (hr_ref[:, st], hi_ref[:, st]), unroll=2)
            hr_ref[:, st] = hr
            hi_ref[:, st] = hi

    def phase3(rb, c):
        rows = rows_of(rb)
        ys = []
        for i in range(2):
            hs = bu_ref[rows, 2 * S5_HALF * i:2 * S5_HALF * (i + 1)].astype(BF16)
            ys.append(_dot(hs, ccat_ref[i]))
        u = uz_ref[rows, 0:W_S5]
        y = jnp.concatenate(ys, axis=1) + d_ref[...] * u
        g = _gelu_tanh(y)
        glu = _dot(g.astype(BF16), wglu_ref[...]) + bglu_ref[...]
        zs = uz_ref[rows, W_S5:2 * W_S5]
        s5 = g * _sigmoid(glu) * (zs * _sigmoid(zs))
        s5_ref[rows, :] = s5.astype(BF16)
        return c
    lax.fori_loop(0, n_mb, phase3, 0)

    def phase3b(rb, c):
        rows = rows_of(rb)
        cat_ref[rows, 0:W_S5] = _dot(permt_ref[rows, :], s5_ref[...]).astype(BF16)
        return c
    lax.fori_loop(0, n_mb, phase3b, 0)

    row = lax.broadcasted_iota(jnp.int32, (ROW_BLOCK, ROW_BLOCK), 0)
    col = lax.broadcasted_iota(jnp.int32, (ROW_BLOCK, ROW_BLOCK), 1)
    tin = row & (C - 1)
    levels = []
    m = 1
    while m < C:
        levels.append(m)
        m *= 2
    lev = jnp.where(row == col, 0, -1)
    for li, m in enumerate(levels):
        same = (row & ~(2 * m - 1)) == (col & ~(2 * m - 1))
        own = same & ((row & m) != 0) & ((col & m) == 0)
        lev = jnp.where(own, li + 1, lev)
    lev_ref[...] = lev
    in_chunk = (row & ~(C - 1)) == (col & ~(C - 1))
    tri_ref[...] = jnp.where(in_chunk & (col <= row), 1.0, 0.0).astype(BF16)

    heads = range(HG_HEADS)
    hcs = [slice(HG_DK * hd, HG_DK * (hd + 1)) for hd in heads]

    def intra(rb, c):
        rows = rows_of(rb, ROW_BLOCK)
        lv = lev_ref[...]
        qs, ks, fs, vs, gcats = [], [], [], [], []
        for hd in heads:
            lb = lb_ref[:, hcs[hd]]
            q = proj_ref[rows, HG_DK * hd:HG_DK * (hd + 1)]
            gf = proj_ref[rows, 512 + HG_DK * hd:512 + HG_DK * (hd + 1)]
            v = proj_ref[rows, 1024 + HG_DV * hd:1024 + HG_DV * (hd + 1)].astype(BF16)
            f = lb + (1.0 - lb) * _sigmoid(gf)
            g = jnp.log2(f)
            k = 1.0 - f
            if t_valid < C:
                live = tin < t_valid
                f = jnp.where(live, f, 1.0)
                g = jnp.where(live, g, 0.0)
                k = jnp.where(live, k, 0.0)
            g1 = g.astype(BF16)
            r1 = g - g1.astype(F32)
            g2 = r1.astype(BF16)
            g3 = (r1 - g2.astype(F32)).astype(BF16)
            qs.append(q)
            ks.append(k)
            fs.append(f)
            vs.append(v)
            gcats.append(jnp.concatenate([g1, g2, g3], axis=1))
        bparts = [_dot(tri_ref[...], gcats[hd]) for hd in heads]
        bs = []
        for hd in heads:
            bp = bparts[hd]
            b = (bp[:, 0:HG_DK] + bp[:, HG_DK:2 * HG_DK]) + bp[:, 2 * HG_DK:3 * HG_DK]
            bsall_ref[hd] = b
            bs.append(b)
        qbs = [q.astype(BF16) for q in qs]
        kbs = [k.astype(BF16) for k in ks]
        atts = [jnp.where(lv == 0, _dot_nt(qbs[hd], kbs[hd]), 0.0) for hd in heads]
        for li, m in enumerate(levels):
            two_m = 2 * m
            p2 = row & (two_m - 1)
            if m > 2:
                sgn = jnp.where(p2 >= m, 1.0, -1.0)
            for hd in heads:
                b = bs[hd]
                f = fs[hd]
                if m == 1:
                    dec = jnp.where(p2 == 1, f, 1.0)
                elif m == 2:
                    dec = jnp.where(p2 == 0, pltpu.roll(f, ROW_BLOCK - 1, axis=0),
                                    jnp.where(p2 == 1, 1.0,
                                              jnp.where(p2 == 2, f, f * pltpu.roll(f, 1, axis=0))))
                else:
                    ref = _row_bcast_blocks(bsall_ref.at[hd], two_m, m - 1)
                    dec = jnp.exp2((b - ref) * sgn)
                decb = dec.astype(BF16)
                atts[hd] = jnp.where(lv == li + 1, _dot_nt(qbs[hd] * decb, kbs[hd] * decb), atts[hd])
        for hd in heads:
            o_ref[rows, hcs[hd]] = _dot(atts[hd].astype(BF16), vs[hd])
        for hd in heads:
            hc = hcs[hd]
            bs_ref = bsall_ref.at[hd]
            b = bs[hd]
            b_last = _row_bcast_blocks(bs_ref, C, C - 1)
            qe_ref[rows, hc] = (qs[hd] * jnp.exp2(b)).astype(qe_ref.dtype)
            ke_ref[rows, hc] = (ks[hd] * jnp.exp2(b_last - b)).astype(ke_ref.dtype)
            v_ref[rows, hc] = vs[hd].astype(v_ref.dtype)
            for n in range(nbat):
                last = bs_ref[n * C + C - 1:n * C + C, :]
                dl_ref[rb * nbat + n, :, hc] = jnp.broadcast_to(jnp.exp2(last), (SUBLANES, HG_DK))
        return c
    lax.fori_loop(0, n_rb, intra, 0)

    def inter(bi, c):
        rows = pl.ds(pl.multiple_of(bi * C, C), C)
        for hd in range(HG_HEADS):
            hc = slice(HG_DK * hd, HG_DK * (hd + 1))
            if has_state:
                st = s0_ref[bi, hd]
                o_ref[rows, hc] = o_ref[rows, hc] + _dot(qe_ref[rows, hc], st.astype(qe_ref.dtype))
                upd = _dot_tn(ke_ref[rows, hc], v_ref[rows, hc])
                s_ref[bi, hd] = st * dl_ref[bi, :, hc].T[:, 0:1] + upd
            else:
                st = s_ref[bi, hd]
                o_ref[rows, hc] = o_ref[rows, hc] + _dot_nt(qe_ref[rows, hc], st.astype(qe_ref.dtype))
                upd = _dot_tn(v_ref[rows, hc], ke_ref[rows, hc])
                s_ref[bi, hd] = st * dl_ref[bi, 0:1, hc] + upd
        return c
    lax.fori_loop(0, NB, inter, 0, unroll=2)

    def phase5(rb, c):
        rows = rows_of(rb)
        for hd in range(HG_HEADS):
            hc = slice(HG_DV * hd, HG_DV * (hd + 1))
            o = o_ref[rows, hc]
            o = o * lax.rsqrt(jnp.mean(o * o, axis=-1, keepdims=True) + EPS)
            zh = proj_ref[rows, 1536 + HG_DV * hd:1536 + HG_DV * (hd + 1)]
            hg = o * og_ref[:, hc] * (zh * _sigmoid(zh))
            cat_ref[rows, W_S5 + HG_DV * hd:W_S5 + HG_DV * (hd + 1)] = hg.astype(BF16)
        out = _dot(cat_ref[rows, :], wout_ref[...])
        xn = x_rows(rb) + out
        yv = xn * lax.rsqrt(jnp.mean(xn * xn, axis=-1, keepdims=True) + EPS) * fg_ref[...]
        y_ref[pl.ds(rb * mbat, mbat)] = yv.reshape(mbat, C, D_MODEL)
        return c
    lax.fori_loop(0, n_mb, phase5, 0)

    if not has_state:
        @pl.when(j == nj - 1)
        def _fin():
            def tr(b, c):
                for hd in range(HG_HEADS):
                    s_ref[b, hd] = s_ref[b, hd].T
                return c
            lax.fori_loop(0, NB, tr, 0)


def _time_major_perm(NB, C):
    R = NB * C
    dst = np.arange(R)
    src = (dst % NB) * C + dst // NB
    p = np.zeros((R, R), np.float32)
    p[dst, src] = 1.0
    return p


def _mixer_call(x, state, weights, *, NB, C, t_valid, name):
    B, L, _ = x.shape
    has_state = state is not None
    assert not has_state or L == C, "a given state is consumed by a single chunk per sequence"
    grid = (B // NB, L // C)
    R = NB * C
    perm = _time_major_perm(NB, C)
    weights = (jnp.asarray(perm, BF16), jnp.asarray(perm.T, BF16)) + tuple(weights)

    def full(a):
        nd = a.ndim
        return pl.BlockSpec(a.shape, lambda i, j, nd=nd: (0,) * nd, pipeline_mode=pl.Buffered(1))

    x_spec = pl.BlockSpec((NB, C, D_MODEL), lambda i, j: (i, j, 0))
    h_spec = pl.BlockSpec((NB, S5_LANES), lambda i, j: (i, 0))
    s_spec = pl.BlockSpec((NB, HG_HEADS, HG_DK, HG_DV), lambda i, j: (i, 0, 0, 0))
    in_specs = [x_spec] + ([h_spec, h_spec, s_spec] if has_state else []) + [full(w) for w in weights]
    args = [x] + (list(state) if has_state else []) + list(weights)
    out_shape = [
        jax.ShapeDtypeStruct((B, L, D_MODEL), x.dtype),
        jax.ShapeDtypeStruct((B, S5_LANES), F32),
        jax.ShapeDtypeStruct((B, S5_LANES), F32),
        jax.ShapeDtypeStruct((B, HG_HEADS, HG_DK, HG_DV), F32),
    ]
    chunk_dt = BF16 if C % BF16_TILE_ROWS == 0 else F32
    scratch = [
        pltpu.VMEM((R, D_MODEL), BF16),
        pltpu.VMEM((R, HG_COLS), F32),
        pltpu.VMEM((R, 2 * W_S5), F32),
        pltpu.VMEM((R, 2 * S5_LANES), F32),
        pltpu.VMEM((R, W_S5), BF16),
        pltpu.VMEM((R, W_HG), F32),
        pltpu.VMEM((R, W_HG), chunk_dt),
        pltpu.VMEM((R, W_HG), chunk_dt),
        pltpu.VMEM((R, W_HG), chunk_dt),
        pltpu.VMEM((HG_HEADS, ROW_BLOCK, HG_DK), F32),
        pltpu.VMEM((ROW_BLOCK, ROW_BLOCK), jnp.int32),
        pltpu.VMEM((ROW_BLOCK, ROW_BLOCK), BF16),
        pltpu.VMEM((NB, SUBLANES, W_HG), F32),
        pltpu.VMEM((R, D_MODEL), BF16),
    ]
    return pl.pallas_call(
        functools.partial(_mixer_kernel, NB=NB, C=C, t_valid=t_valid, has_state=has_state),
        grid=grid,
        in_specs=in_specs,
        out_specs=[x_spec, h_spec, h_spec, s_spec],
        out_shape=out_shape,
        scratch_shapes=scratch,
        compiler_params=pltpu.CompilerParams(
            dimension_semantics=("arbitrary", "arbitrary"),
            vmem_limit_bytes=V7X_VMEM_LIMIT),
        name=name,
    )(*args)


def kernel(x_prompt, x_sample, state_s5_re, state_s5_im, state_hgrn, norm_g, w_in, s5_lambda_re, s5_lambda_im,
           s5_log_step, s5_b_re, s5_b_im, s5_c_re, s5_c_im, s5_d, w_glu, b_glu, hgrn_lb_logits, hgrn_onorm_g,
           w_out, final_norm_g):
    assert norm_g.shape[0] == 1, "single-layer model"
    l = 0
    a_re, a_im, bbre, bbim, lb = _prepare_params(
        s5_lambda_re[l], s5_lambda_im[l], s5_log_step[l], s5_b_re[l], s5_b_im[l], hgrn_lb_logits)
    bcat, ccat = _block_diag_weights(bbre, bbim, s5_c_re[l], s5_c_im[l])
    weights = (
        norm_g[l].astype(F32).reshape(1, D_MODEL),
        w_in[l].astype(BF16),
        a_re, a_im, bcat, ccat,
        s5_d[l].astype(F32).reshape(1, W_S5),
        w_glu[l].astype(BF16),
        b_glu[l].astype(F32).reshape(1, W_S5),
        lb,
        hgrn_onorm_g[l].astype(F32).reshape(1, W_HG),
        w_out[l].astype(BF16),
        final_norm_g.astype(F32).reshape(1, D_MODEL),
    )

    Bp, Lp, _ = x_prompt.shape
    yp, p_re, p_im, p_hg = _mixer_call(x_prompt, None, weights, NB=Bp, C=64, t_valid=64, name="mixer_prompt")

    Bs, Ls, _ = x_sample.shape
    xs = jnp.pad(x_sample, ((0, 0), (0, SAMPLE_PAD_SEQ - Ls), (0, 0)))
    state = (state_s5_re[l].reshape(Bs, S5_LANES).astype(F32),
             state_s5_im[l].reshape(Bs, S5_LANES).astype(F32),
             state_hgrn[l].astype(F32))
    ys, s_re, s_im, s_hg = _mixer_call(xs, state, weights, NB=16, C=SAMPLE_PAD_SEQ, t_valid=Ls, name="mixer_sample")
    ys = ys[:, :Ls]

    sd = state_s5_re.dtype
    shp = (1, Bp, S5_GROUPS, S5_STATE)
    shs = (1, Bs, S5_GROUPS, S5_STATE)
    return (yp, ys,
            p_re.reshape(shp).astype(sd), p_im.reshape(shp).astype(sd), p_hg[None].astype(state_hgrn.dtype),
            s_re.reshape(shs).astype(sd), s_im.reshape(shs).astype(sd), s_hg[None].astype(state_hgrn.dtype))
```

```python
import functools
import math

import jax
import jax.numpy as jnp
import numpy as np
from jax import lax
from jax.experimental import pallas as pl
from jax.experimental.pallas import tpu as pltpu

F32 = jnp.float32
BF16 = jnp.bfloat16

D_MODEL = 1024
W_S5 = 512
W_HG = 512
S5_CH = 16
S5_GROUPS = 32
S5_STATE = 64
S5_LANES = S5_GROUPS * S5_STATE
S5_HALF = S5_LANES // 2
HG_DK = 128
HG_HEADS = 4
HG_DV = 128
EPS = 1e-6
LAMBDA_RE_MAX = -1e-4
PROJ_OUT = 3072
OFF_Q = 1024
HG_COLS = PROJ_OUT - OFF_Q
LOG2E = 1.4426950408889634

LANES = 128
SUBLANES = 8
BF16_TILE_ROWS = 16
ROW_BLOCK = 128
MATMUL_ROWS = 256
V7X_VMEM_LIMIT = 60 * 1024 * 1024
SAMPLE_PAD_SEQ = 8


def _prep_kernel(lre_ref, lim_ref, ls_ref, brt_ref, bit_ref, lg_ref,
                 are_ref, aim_ref, bbre_ref, bbim_ref, lb_ref):
    lr = jnp.minimum(lre_ref[...], LAMBDA_RE_MAX)
    li = lim_ref[...]
    dt = jnp.exp(ls_ref[...])
    er = jnp.exp(lr * dt)
    a_re = er * jnp.cos(li * dt)
    a_im = er * jnp.sin(li * dt)
    den = lr * lr + li * li
    zr = ((a_re - 1.0) * lr + a_im * li) / den
    zi = (a_im * lr - (a_re - 1.0) * li) / den
    are_ref[...] = a_re
    aim_ref[...] = a_im
    for g in range(S5_GROUPS):
        br = brt_ref[g]
        bi = bit_ref[g]
        zr_g = zr[g:g + 1, :]
        zi_g = zi[g:g + 1, :]
        bbre_ref[g] = zr_g * br - zi_g * bi
        bbim_ref[g] = zr_g * bi + zi_g * br
    lg = lg_ref[...]
    e = jnp.exp(lg - jnp.max(lg, axis=0, keepdims=True))
    lb_ref[...] = e[0:1, :] / jnp.sum(e, axis=0, keepdims=True)


def _prepare_params(lam_re, lam_im, log_step, b_re, b_im, lb_logits):
    gps = jax.ShapeDtypeStruct((S5_GROUPS, S5_STATE), F32)
    gcs = jax.ShapeDtypeStruct((S5_GROUPS, S5_CH, S5_STATE), F32)
    a_re, a_im, bbre, bbim, lb = pl.pallas_call(
        _prep_kernel,
        out_shape=[gps, gps, gcs, gcs, jax.ShapeDtypeStruct((1, HG_HEADS * HG_DK), F32)],
        name="s5_hgrn_prep",
    )(lam_re.astype(F32), lam_im.astype(F32), log_step.astype(F32).reshape(S5_GROUPS, 1),
      jnp.transpose(b_re.astype(F32), (0, 2, 1)), jnp.transpose(b_im.astype(F32), (0, 2, 1)),
      lb_logits.astype(F32))
    return a_re.reshape(1, S5_LANES), a_im.reshape(1, S5_LANES), bbre, bbim, lb


def _block_diag_weights(bbre, bbim, c_re, c_im):
    hg = S5_GROUPS // 2
    eye = jnp.eye(hg, dtype=BF16)[None, :, None, :, None]

    def in_half(bbt):
        x = bbt.astype(BF16).reshape(2, hg, S5_CH, 1, S5_STATE) * eye
        return x.reshape(2, hg * S5_CH, hg * S5_STATE)

    def out_half(c):
        ct = jnp.transpose(c.astype(BF16), (0, 2, 1)).reshape(2, hg, S5_STATE, 1, S5_CH)
        return (ct * eye).reshape(2, hg * S5_STATE, hg * S5_CH)

    bcat = jnp.concatenate([in_half(bbre), in_half(bbim)], axis=2)
    ccat = jnp.concatenate([out_half(c_re), out_half(-c_im)], axis=1)
    return bcat, ccat


def _sigmoid(x):
    return 1.0 / (1.0 + jnp.exp2(x * (-LOG2E)))


def _gelu_tanh(x):
    return 0.5 * x * (1.0 + jnp.tanh(math.sqrt(2.0 / math.pi) * (x + 0.044715 * (x * x * x))))


def _dot(a, b):
    return jnp.dot(a, b, preferred_element_type=F32)


def _dot_nt(a, b):
    return lax.dot_general(a, b, (((1,), (1,)), ((), ())), preferred_element_type=F32)


def _dot_tn(a, b):
    return lax.dot_general(a, b, (((0,), (0,)), ((), ())), preferred_element_type=F32)


def _row_bcast_blocks(src_ref, blk, off):
    parts = []
    for n in range(ROW_BLOCK // blk):
        row = src_ref[n * blk + off:n * blk + off + 1, :]
        parts.append(jnp.broadcast_to(row, (blk, HG_DK)))
    return parts[0] if len(parts) == 1 else jnp.concatenate(parts, axis=0)


def _mixer_kernel(*refs, NB, C, t_valid, has_state):
    if has_state:
        x_ref, h0r_ref, h0i_ref, s0_ref = refs[:4]
        refs = refs[4:]
    else:
        x_ref = refs[0]
        refs = refs[1:]
    (perm_ref, permt_ref, ng_ref, win_ref, are_ref, aim_ref, bcat_ref, ccat_ref, d_ref, wglu_ref, bglu_ref,
     lb_ref, og_ref, wout_ref, fg_ref,
     y_ref, hr_ref, hi_ref, s_ref,
     hn_ref, proj_ref, uz_ref, bu_ref, s5_ref, o_ref, qe_ref, ke_ref, v_ref, bsall_ref, lev_ref, tri_ref,
     dl_ref, cat_ref) = refs

    j = pl.program_id(1)
    nj = pl.num_programs(1)
    R = NB * C
    n_rb = R // ROW_BLOCK
    nbat = ROW_BLOCK // C

    @pl.when(j == 0)
    def _init():
        if has_state:
            hr_ref[...] = h0r_ref[...]
            hi_ref[...] = h0i_ref[...]
        else:
            hr_ref[...] = jnp.zeros(hr_ref.shape, F32)
            hi_ref[...] = jnp.zeros(hi_ref.shape, F32)
            s_ref[...] = jnp.zeros(s_ref.shape, F32)

    MM = min(R, MATMUL_ROWS)
    n_mb = R // MM
    mbat = MM // C

    def x_rows(mb):
        return x_ref[pl.ds(mb * mbat, mbat)].reshape(MM, D_MODEL)

    def rows_of(rb, n=MM):
        return pl.ds(pl.multiple_of(rb * n, n), n)

    def phase1a(rb, c):
        rows = rows_of(rb)
        x = x_rows(rb)
        hn = x * lax.rsqrt(jnp.mean(x * x, axis=-1, keepdims=True) + EPS) * ng_ref[...]
        hn = hn.astype(BF16)
        hn_ref[rows, :] = hn
        for sec in range(HG_COLS // 512):
            cs = slice(512 * sec, 512 * (sec + 1))
            proj_ref[rows, cs] = _dot(hn, win_ref[:, OFF_Q + 512 * sec:OFF_Q + 512 * (sec + 1)])
        return c
    lax.fori_loop(0, n_mb, phase1a, 0)

    def phase1b(rb, c):
        rows = rows_of(rb)
        hp = _dot(perm_ref[rows, :], hn_ref[...]).astype(BF16)
        for sec in range(2):
            uz_ref[rows, 512 * sec:512 * (sec + 1)] = _dot(hp, win_ref[:, 512 * sec:512 * (sec + 1)])
        for i in range(2):
            u_i = uz_ref[rows, 256 * i:256 * (i + 1)].astype(BF16)
            bu_ref[rows, 2 * S5_HALF * i:2 * S5_HALF * (i + 1)] = _dot(u_i, bcat_ref[i])
        return c
    lax.fori_loop(0, n_mb, phase1b, 0)

    width = min(S5_HALF, (SUBLANES * S5_HALF) // NB)
    for i in range(2):
        for off in range(0, S5_HALF, width):
            st = slice(S5_HALF * i + off, S5_HALF * i + off + width)
            re_c = 2 * S5_HALF * i + off
            im_c = re_c + S5_HALF
            ar = are_ref[:, st]
            ai = aim_ref[:, st]

            def step(t, carry, re_c=re_c, im_c=im_c, ar=ar, ai=ai):
                hr, hi = carry
                rows = pl.ds(pl.multiple_of(t * NB, NB), NB)
                nhr = ar * hr - ai * hi + bu_ref[rows, re_c:re_c + width]
                nhi = ar * hi + ai * hr + bu_ref[rows, im_c:im_c + width]
                bu_ref[rows, re_c:re_c + width] = nhr
                bu_ref[rows, im_c:im_c + width] = nhi
                return nhr, nhi
            hr, hi = lax.fori_loop(0, t_valid, step, (hr_ref[:, st], hi_ref[:, st]), unroll=2)
            hr_ref[:, st] = hr
            hi_ref[:, st] = hi

    def phase3(rb, c):
        rows = rows_of(rb)
        ys = []
        for i in range(2):
            hs = bu_ref[rows, 2 * S5_HALF * i:2 * S5_HALF * (i + 1)].astype(BF16)
            ys.append(_dot(hs, ccat_ref[i]))
        u = uz_ref[rows, 0:W_S5]
        y = jnp.concatenate(ys, axis=1) + d_ref[...] * u
        g = _gelu_tanh(y)
        glu = _dot(g.astype(BF16), wglu_ref[...]) + bglu_ref[...]
        zs = uz_ref[rows, W_S5:2 * W_S5]
        s5 = g * _sigmoid(glu) * (zs * _sigmoid(zs))
        s5_ref[rows, :] = s5.astype(BF16)
        return c
    lax.fori_loop(0, n_mb, phase3, 0)

    def phase3b(rb, c):
        rows = rows_of(rb)
        cat_ref[rows, 0:W_S5] = _dot(permt_ref[rows, :], s5_ref[...]).astype(BF16)
        return c
    lax.fori_loop(0, n_mb, phase3b, 0)

    row = lax.broadcasted_iota(jnp.int32, (ROW_BLOCK, ROW_BLOCK), 0)
    col = lax.broadcasted_iota(jnp.int32, (ROW_BLOCK, ROW_BLOCK), 1)
    tin = row & (C - 1)
    levels = []
    m = 1
    while m < C:
        levels.append(m)
        m *= 2
    lev = jnp.where(row == col, 0, -1)
    for li, m in enumerate(levels):
        same = (row & ~(2 * m - 1)) == (col & ~(2 * m - 1))
        own = same & ((row & m) != 0) & ((col & m) == 0)
        lev = jnp.where(own, li + 1, lev)
    lev_ref[...] = lev
    in_chunk = (row & ~(C - 1)) == (col & ~(C - 1))
    tri_ref[...] = jnp.where(in_chunk & (col <= row), 1.0, 0.0).astype(BF16)

    heads = range(HG_HEADS)
    hcs = [slice(HG_DK * hd, HG_DK * (hd + 1)) for hd in heads]

    def intra(rb, c):
        rows = rows_of(rb, ROW_BLOCK)
        lv = lev_ref[...]
        qs, ks, fs, vs, gcats = [], [], [], [], []
        for hd in heads:
            lb = lb_ref[:, hcs[hd]]
            q = proj_ref[rows, HG_DK * hd:HG_DK * (hd + 1)]
            gf = proj_ref[rows, 512 + HG_DK * hd:512 + HG_DK * (hd + 1)]
            v = proj_ref[rows, 1024 + HG_DV * hd:1024 + HG_DV * (hd + 1)].astype(BF16)
            f = lb + (1.0 - lb) * _sigmoid(gf)
            g = jnp.log2(f)
            k = 1.0 - f
            if t_valid < C:
                live = tin < t_valid
                f = jnp.where(live, f, 1.0)
                g = jnp.where(live, g, 0.0)
                k = jnp.where(live, k, 0.0)
            g1 = g.astype(BF16)
            r1 = g - g1.astype(F32)
            g2 = r1.astype(BF16)
            g3 = (r1 - g2.astype(F32)).astype(BF16)
            qs.append(q)
            ks.append(k)
            fs.append(f)
            vs.append(v)
            gcats.append(jnp.concatenate([g1, g2, g3], axis=1))
        bparts = [_dot(tri_ref[...], gcats[hd]) for hd in heads]
        bs = []
        for hd in heads:
            bp = bparts[hd]
            b = (bp[:, 0:HG_DK] + bp[:, HG_DK:2 * HG_DK]) + bp[:, 2 * HG_DK:3 * HG_DK]
            bsall_ref[hd] = b
            bs.append(b)
        qbs = [q.astype(BF16) for q in qs]
        kbs = [k.astype(BF16) for k in ks]
        atts = [jnp.where(lv == 0, _dot_nt(qbs[hd], kbs[hd]), 0.0) for hd in heads]
        for li, m in enumerate(levels):
            two_m = 2 * m
            p2 = row & (two_m - 1)
            if m > 2:
                sgn = jnp.where(p2 >= m, 1.0, -1.0)
            for hd in heads:
                b = bs[hd]
                f = fs[hd]
                if m == 1:
                    dec = jnp.where(p2 == 1, f, 1.0)
                elif m == 2:
                    dec = jnp.where(p2 == 0, pltpu.roll(f, ROW_BLOCK - 1, axis=0),
                                    jnp.where(p2 == 1, 1.0,
                                              jnp.where(p2 == 2, f, f * pltpu.roll(f, 1, axis=0))))
                else:
                    ref = _row_bcast_blocks(bsall_ref.at[hd], two_m, m - 1)
                    dec = jnp.exp2((b - ref) * sgn)
                decb = dec.astype(BF16)
                atts[hd] = jnp.where(lv == li + 1, _dot_nt(qbs[hd] * decb, kbs[hd] * decb), atts[hd])
        for hd in heads:
            o_ref[rows, hcs[hd]] = _dot(atts[hd].astype(BF16), vs[hd])
        for hd in heads:
            hc = hcs[hd]
            bs_ref = bsall_ref.at[hd]
            b = bs[hd]
            b_last = _row_bcast_blocks(bs_ref, C, C - 1)
            qe_ref[rows, hc] = (qs[hd] * jnp.exp2(b)).astype(qe_ref.dtype)
            ke_ref[rows, hc] = (ks[hd] * jnp.exp2(b_last - b)).astype(ke_ref.dtype)
            v_ref[rows, hc] = vs[hd].astype(v_ref.dtype)
            for n in range(nbat):
                last = bs_ref[n * C + C - 1:n * C + C, :]
                dl_ref[rb * nbat + n, :, hc] = jnp.broadcast_to(jnp.exp2(last), (SUBLANES, HG_DK))
        return c
    lax.fori_loop(0, n_rb, intra, 0)

    def inter(bi, c):
        rows = pl.ds(pl.multiple_of(bi * C, C), C)
        for hd in range(HG_HEADS):
            hc = slice(HG_DK * hd, HG_DK * (hd + 1))
            if has_state:
                st = s0_ref[bi, hd]
                o_ref[rows, hc] = o_ref[rows, hc] + _dot(qe_ref[rows, hc], st.astype(qe_ref.dtype))
                upd = _dot_tn(ke_ref[rows, hc], v_ref[rows, hc])
                s_ref[bi, hd] = st * dl_ref[bi, :, hc].T[:, 0:1] + upd
            else:
                st = s_ref[bi, hd]
                o_ref[rows, hc] = o_ref[rows, hc] + _dot_nt(qe_ref[rows, hc], st.astype(qe_ref.dtype))
                upd = _dot_tn(v_ref[rows, hc], ke_ref[rows, hc])
                s_ref[bi, hd] = st * dl_ref[bi, 0:1, hc] + upd
        return c
    lax.fori_loop(0, NB, inter, 0, unroll=2)

    def phase5(rb, c):
        rows = rows_of(rb)
        for hd in range(HG_HEADS):
            hc = slice(HG_DV * hd, HG_DV * (hd + 1))
            o = o_ref[rows, hc]
            o = o * lax.rsqrt(jnp.mean(o * o, axis=-1, keepdims=True) + EPS)
            zh = proj_ref[rows, 1536 + HG_DV * hd:1536 + HG_DV * (hd + 1)]
            hg = o * og_ref[:, hc] * (zh * _sigmoid(zh))
            cat_ref[rows, W_S5 + HG_DV * hd:W_S5 + HG_DV * (hd + 1)] = hg.astype(BF16)
        out = _dot(cat_ref[rows, :], wout_ref[...])
        xn = x_rows(rb) + out
        yv = xn * lax.rsqrt(jnp.mean(xn * xn, axis=-1, keepdims=True) + EPS) * fg_ref[...]
        y_ref[pl.ds(rb * mbat, mbat)] = yv.reshape(mbat, C, D_MODEL)
        return c
    lax.fori_loop(0, n_mb, phase5, 0)

    if not has_state:
        @pl.when(j == nj - 1)
        def _fin():
            def tr(b, c):
                for hd in range(HG_HEADS):
                    s_ref[b, hd] = s_ref[b, hd].T
                return c
            lax.fori_loop(0, NB, tr, 0)


def _time_major_perm(NB, C):
    R = NB * C
    dst = np.arange(R)
    src = (dst % NB) * C + dst // NB
    p = np.zeros((R, R), np.float32)
    p[dst, src] = 1.0
    return p


def _mixer_call(x, state, weights, *, NB, C, t_valid, name):
    B, L, _ = x.shape
    has_state = state is not None
    assert not has_state or L == C, "a given state is consumed by a single chunk per sequence"
    grid = (B // NB, L // C)
    R = NB * C
    perm = _time_major_perm(NB, C)
    weights = (jnp.asarray(perm, BF16), jnp.asarray(perm.T, BF16)) + tuple(weights)

    def full(a):
        nd = a.ndim
        return pl.BlockSpec(a.shape, lambda i, j, nd=nd: (0,) * nd, pipeline_mode=pl.Buffered(1))

    x_spec = pl.BlockSpec((NB, C, D_MODEL), lambda i, j: (i, j, 0))
    h_spec = pl.BlockSpec((NB, S5_LANES), lambda i, j: (i, 0))
    s_spec = pl.BlockSpec((NB, HG_HEADS, HG_DK, HG_DV), lambda i, j: (i, 0, 0, 0))
    in_specs = [x_spec] + ([h_spec, h_spec, s_spec] if has_state else []) + [full(w) for w in weights]
    args = [x] + (list(state) if has_state else []) + list(weights)
    out_shape = [
        jax.ShapeDtypeStruct((B, L, D_MODEL), x.dtype),
        jax.ShapeDtypeStruct((B, S5_LANES), F32),
        jax.ShapeDtypeStruct((B, S5_LANES), F32),
        jax.ShapeDtypeStruct((B, HG_HEADS, HG_DK, HG_DV), F32),
    ]
    chunk_dt = BF16 if C % BF16_TILE_ROWS == 0 else F32
    scratch = [
        pltpu.VMEM((R, D_MODEL), BF16),
        pltpu.VMEM((R, HG_COLS), F32),
        pltpu.VMEM((R, 2 * W_S5), F32),
        pltpu.VMEM((R, 2 * S5_LANES), F32),
        pltpu.VMEM((R, W_S5), BF16),
        pltpu.VMEM((R, W_HG), F32),
        pltpu.VMEM((R, W_HG), chunk_dt),
        pltpu.VMEM((R, W_HG), chunk_dt),
        pltpu.VMEM((R, W_HG), chunk_dt),
        pltpu.VMEM((HG_HEADS, ROW_BLOCK, HG_DK), F32),
        pltpu.VMEM((ROW_BLOCK, ROW_BLOCK), jnp.int32),
        pltpu.VMEM((ROW_BLOCK, ROW_BLOCK), BF16),
        pltpu.VMEM((NB, SUBLANES, W_HG), F32),
        pltpu.VMEM((R, D_MODEL), BF16),
    ]
    return pl.pallas_call(
        functools.partial(_mixer_kernel, NB=NB, C=C, t_valid=t_valid, has_state=has_state),
        grid=grid,
        in_specs=in_specs,
        out_specs=[x_spec, h_spec, h_spec, s_spec],
        out_shape=out_shape,
        scratch_shapes=scratch,
        compiler_params=pltpu.CompilerParams(
            dimension_semantics=("arbitrary", "arbitrary"),
            vmem_limit_bytes=V7X_VMEM_LIMIT),
        name=name,
    )(*args)


def kernel(x_prompt, x_sample, state_s5_re, state_s5_im, state_hgrn, norm_g, w_in, s5_lambda_re, s5_lambda_im,
           s5_log_step, s5_b_re, s5_b_im, s5_c_re, s5_c_im, s5_d, w_glu, b_glu, hgrn_lb_logits, hgrn_onorm_g,
           w_out, final_norm_g):
    assert norm_g.shape[0] == 1, "single-layer model"
    l = 0
    a_re, a_im, bbre, bbim, lb = _prepare_params(
        s5_lambda_re[l], s5_lambda_im[l], s5_log_step[l], s5_b_re[l], s5_b_im[l], hgrn_lb_logits)
    bcat, ccat = _block_diag_weights(bbre, bbim, s5_c_re[l], s5_c_im[l])
    weights = (
        norm_g[l].astype(F32).reshape(1, D_MODEL),
        w_in[l].astype(BF16),
        a_re, a_im, bcat, ccat,
        s5_d[l].astype(F32).reshape(1, W_S5),
        w_glu[l].astype(BF16),
        b_glu[l].astype(F32).reshape(1, W_S5),
        lb,
        hgrn_onorm_g[l].astype(F32).reshape(1, W_HG),
        w_out[l].astype(BF16),
        final_norm_g.astype(F32).reshape(1, D_MODEL),
    )

    Bp, Lp, _ = x_prompt.shape
    yp, p_re, p_im, p_hg = _mixer_call(x_prompt, None, weights, NB=Bp, C=64, t_valid=64, name="mixer_prompt")

    Bs, Ls, _ = x_sample.shape
    xs = jnp.pad(x_sample, ((0, 0), (0, SAMPLE_PAD_SEQ - Ls), (0, 0)))
    state = (state_s5_re[l].reshape(Bs, S5_LANES).astype(F32),
             state_s5_im[l].reshape(Bs, S5_LANES).astype(F32),
             state_hgrn[l].astype(F32))
    ys, s_re, s_im, s_hg = _mixer_call(xs, state, weights, NB=16, C=SAMPLE_PAD_SEQ, t_valid=Ls, name="mixer_sample")
    ys = ys[:, :Ls]

    sd = state_s5_re.dtype
    shp = (1, Bp, S5_GROUPS, S5_STATE)
    shs = (1, Bs, S5_GROUPS, S5_STATE)
    return (yp, ys,
            p_re.reshape(shp).astype(sd), p_im.reshape(shp).astype(sd), p_hg[None].astype(state_hgrn.dtype),
            s_re.reshape(shs).astype(sd), s_im.reshape(shs).astype(sd), s_hg[None].astype(state_hgrn.dtype))
```

```python
import functools
import math

import jax
import jax.numpy as jnp
import numpy as np
from jax import lax
from jax.experimental import pallas as pl
from jax.experimental.pallas import tpu as pltpu

F32 = jnp.float32
BF16 = jnp.bfloat16

D_MODEL = 1024
W_S5 = 512
W_HG = 512
S5_CH = 16
S5_GROUPS = 32
S5_STATE = 64
S5_LANES = S5_GROUPS * S5_STATE
S5_HALF = S5_LANES // 2
HG_DK = 128
HG_HEADS = 4
HG_DV = 128
EPS = 1e-6
LAMBDA_RE_MAX = -1e-4
PROJ_OUT = 3072
OFF_Q = 1024
HG_COLS = PROJ_OUT - OFF_Q
LOG2E = 1.4426950408889634

LANES = 128
SUBLANES = 8
BF16_TILE_ROWS = 16
ROW_BLOCK = 128
MATMUL_ROWS = 256
V7X_VMEM_LIMIT = 60 * 1024 * 1024
SAMPLE_PAD_SEQ = 8


def _prep_kernel(lre_ref, lim_ref, ls_ref, brt_ref, bit_ref, lg_ref,
                 are_ref, aim_ref, bbre_ref, bbim_ref, lb_ref):
    lr = jnp.minimum(lre_ref[...], LAMBDA_RE_MAX)
    li = lim_ref[...]
    dt = jnp.exp(ls_ref[...])
    er = jnp.exp(lr * dt)
    a_re = er * jnp.cos(li * dt)
    a_im = er * jnp.sin(li * dt)
    den = lr * lr + li * li
    zr = ((a_re - 1.0) * lr + a_im * li) / den
    zi = (a_im * lr - (a_re - 1.0) * li) / den
    are_ref[...] = a_re
    aim_ref[...] = a_im
    for g in range(S5_GROUPS):
        br = brt_ref[g]
        bi = bit_ref[g]
        zr_g = zr[g:g + 1, :]
        zi_g = zi[g:g + 1, :]
        bbre_ref[g] = zr_g * br - zi_g * bi
        bbim_ref[g] = zr_g * bi + zi_g * br
    lg = lg_ref[...]
    e = jnp.exp(lg - jnp.max(lg, axis=0, keepdims=True))
    lb_ref[...] = e[0:1, :] / jnp.sum(e, axis=0, keepdims=True)


def _prepare_params(lam_re, lam_im, log_step, b_re, b_im, lb_logits):
    gps = jax.ShapeDtypeStruct((S5_GROUPS, S5_STATE), F32)
    gcs = jax.ShapeDtypeStruct((S5_GROUPS, S5_CH, S5_STATE), F32)
    a_re, a_im, bbre, bbim, lb = pl.pallas_call(
        _prep_kernel,
        out_shape=[gps, gps, gcs, gcs, jax.ShapeDtypeStruct((1, HG_HEADS * HG_DK), F32)],
        name="s5_hgrn_prep",
    )(lam_re.astype(F32), lam_im.astype(F32), log_step.astype(F32).reshape(S5_GROUPS, 1),
      jnp.transpose(b_re.astype(F32), (0, 2, 1)), jnp.transpose(b_im.astype(F32), (0, 2, 1)),
      lb_logits.astype(F32))
    return a_re.reshape(1, S5_LANES), a_im.reshape(1, S5_LANES), bbre, bbim, lb


def _block_diag_weights(bbre, bbim, c_re, c_im):
    hg = S5_GROUPS // 2
    eye = jnp.eye(hg, dtype=BF16)[None, :, None, :, None]

    def in_half(bbt):
        x = bbt.astype(BF16).reshape(2, hg, S5_CH, 1, S5_STATE) * eye
        return x.reshape(2, hg * S5_CH, hg * S5_STATE)

    def out_half(c):
        ct = jnp.transpose(c.astype(BF16), (0, 2, 1)).reshape(2, hg, S5_STATE, 1, S5_CH)
        return (ct * eye).reshape(2, hg * S5_STATE, hg * S5_CH)

    bcat = jnp.concatenate([in_half(bbre), in_half(bbim)], axis=2)
    ccat = jnp.concatenate([out_half(c_re), out_half(-c_im)], axis=1)
    return bcat, ccat


def _sigmoid(x):
    return 1.0 / (1.0 + jnp.exp2(x * (-LOG2E)))


def _gelu_tanh(x):
    return 0.5 * x * (1.0 + jnp.tanh(math.sqrt(2.0 / math.pi) * (x + 0.044715 * (x * x * x))))


def _dot(a, b):
    return jnp.dot(a, b, preferred_element_type=F32)


def _dot_nt(a, b):
    return lax.dot_general(a, b, (((1,), (1,)), ((), ())), preferred_element_type=F32)


def _dot_tn(a, b):
    return lax.dot_general(a, b, (((0,), (0,)), ((), ())), preferred_element_type=F32)


def _row_bcast_blocks(src_ref, blk, off):
    parts = []
    for n in range(ROW_BLOCK // blk):
        row = src_ref[n * blk + off:n * blk + off + 1, :]
        parts.append(jnp.broadcast_to(row, (blk, HG_DK)))
    return parts[0] if len(parts) == 1 else jnp.concatenate(parts, axis=0)


def _mixer_kernel(*refs, NB, C, t_valid, has_state):
    if has_state:
        x_ref, h0r_ref, h0i_ref, s0_ref = refs[:4]
        refs = refs[4:]
    else:
        x_ref = refs[0]
        refs = refs[1:]
    (perm_ref, permt_ref, ng_ref, win_ref, are_ref, aim_ref, bcat_ref, ccat_ref, d_ref, wglu_ref, bglu_ref,
     lb_ref, og_ref, wout_ref, fg_ref,
     y_ref, hr_ref, hi_ref, s_ref,
     hn_ref, proj_ref, uz_ref, bu_ref, s5_ref, o_ref, qe_ref, ke_ref, v_ref, bsall_ref, lev_ref, tri_ref,
     dl_ref, cat_ref) = refs

    j = pl.program_id(1)
    nj = pl.num_programs(1)
    R = NB * C
    n_rb = R // ROW_BLOCK
    nbat = ROW_BLOCK // C

    @pl.when(j == 0)
    def _init():
        if has_state:
            hr_ref[...] = h0r_ref[...]
            hi_ref[...] = h0i_ref[...]
        else:
            hr_ref[...] = jnp.zeros(hr_ref.shape, F32)
            hi_ref[...] = jnp.zeros(hi_ref.shape, F32)
            s_ref[...] = jnp.zeros(s_ref.shape, F32)

    MM = min(R, MATMUL_ROWS)
    n_mb = R // MM
    mbat = MM // C

    def x_rows(mb):
        return x_ref[pl.ds(mb * mbat, mbat)].reshape(MM, D_MODEL)

    def rows_of(rb, n=MM):
        return pl.ds(pl.multiple_of(rb * n, n), n)

    def phase1a(rb, c):
        rows = rows_of(rb)
        x = x_rows(rb)
        hn = x * lax.rsqrt(jnp.mean(x * x, axis=-1, keepdims=True) + EPS) * ng_ref[...]
        hn = hn.astype(BF16)
        hn_ref[rows, :] = hn
        for sec in range(HG_COLS // 512):
            cs = slice(512 * sec, 512 * (sec + 1))
            proj_ref[rows, cs] = _dot(hn, win_ref[:, OFF_Q + 512 * sec:OFF_Q + 512 * (sec + 1)])
        return c
    lax.fori_loop(0, n_mb, phase1a, 0)

    row = lax.broadcasted_iota(jnp.int32, (ROW_BLOCK, ROW_BLOCK), 0)
    col = lax.broadcasted_iota(jnp.int32, (ROW_BLOCK, ROW_BLOCK), 1)
    tin = row & (C - 1)
    levels = []
    m = 1
    while m < C:
        levels.append(m)
        m *= 2
    lev = jnp.where(row == col, 0, -1)
    for li, m in enumerate(levels):
        same = (row & ~(2 * m - 1)) == (col & ~(2 * m - 1))
        own = same & ((row & m) != 0) & ((col & m) == 0)
        lev = jnp.where(own, li + 1, lev)
    lev_ref[...] = lev
    in_chunk = (row & ~(C - 1)) == (col & ~(C - 1))
    tri_ref[...] = jnp.where(in_chunk & (col <= row), 1.0, 0.0).astype(BF16)

    heads = range(HG_HEADS)
    hcs = [slice(HG_DK * hd, HG_DK * (hd + 1)) for hd in heads]

    def intra(rb, c):
        rows = rows_of(rb, ROW_BLOCK)
        lv = lev_ref[...]
        hp = _dot(perm_ref[rows, :], hn_ref[...]).astype(BF16)
        qs, ks, fs, vs, gcats = [], [], [], [], []
        for hd in heads:
            lb = lb_ref[:, hcs[hd]]
            q = proj_ref[rows, HG_DK * hd:HG_DK * (hd + 1)]
            gf = proj_ref[rows, 512 + HG_DK * hd:512 + HG_DK * (hd + 1)]
            v = proj_ref[rows, 1024 + HG_DV * hd:1024 + HG_DV * (hd + 1)].astype(BF16)
            f = lb + (1.0 - lb) * _sigmoid(gf)
            g = jnp.log2(f)
            k = 1.0 - f
            if t_valid < C:
                live = tin < t_valid
                f = jnp.where(live, f, 1.0)
                g = jnp.where(live, g, 0.0)
                k = jnp.where(live, k, 0.0)
            g1 = g.astype(BF16)
            r1 = g - g1.astype(F32)
            g2 = r1.astype(BF16)
            g3 = (r1 - g2.astype(F32)).astype(BF16)
            qs.append(q)
            ks.append(k)
            fs.append(f)
            vs.append(v)
            gcats.append(jnp.concatenate([g1, g2, g3], axis=1))
        bparts = [_dot(tri_ref[...], gcats[hd]) for hd in heads]
        uz = [_dot(hp, win_ref[:, W_S5 * sec:W_S5 * (sec + 1)]) for sec in range(2)]
        for sec in range(2):
            uz_ref[rows, W_S5 * sec:W_S5 * (sec + 1)] = uz[sec]
        bs = []
        for hd in heads:
            bp = bparts[hd]
            b = (bp[:, 0:HG_DK] + bp[:, HG_DK:2 * HG_DK]) + bp[:, 2 * HG_DK:3 * HG_DK]
            bsall_ref[hd] = b
            bs.append(b)
        qbs = [q.astype(BF16) for q in qs]
        kbs = [k.astype(BF16) for k in ks]
        atts = [jnp.where(lv == 0, _dot_nt(qbs[hd], kbs[hd]), 0.0) for hd in heads]
        for li, m in enumerate(levels):
            two_m = 2 * m
            p2 = row & (two_m - 1)
            if m > 2:
                sgn = jnp.where(p2 >= m, 1.0, -1.0)
            for hd in heads:
                b = bs[hd]
                f = fs[hd]
                if m == 1:
                    dec = jnp.where(p2 == 1, f, 1.0)
                elif m == 2:
                    dec = jnp.where(p2 == 0, pltpu.roll(f, ROW_BLOCK - 1, axis=0),
                                    jnp.where(p2 == 1, 1.0,
                                              jnp.where(p2 == 2, f, f * pltpu.roll(f, 1, axis=0))))
                else:
                    ref = _row_bcast_blocks(bsall_ref.at[hd], two_m, m - 1)
                    dec = jnp.exp2((b - ref) * sgn)
                decb = dec.astype(BF16)
                atts[hd] = jnp.where(lv == li + 1, _dot_nt(qbs[hd] * decb, kbs[hd] * decb), atts[hd])
            if li == 0:
                for i in range(2):
                    u_i = uz[0][:, 256 * i:256 * (i + 1)].astype(BF16)
                    bu_ref[rows, 2 * S5_HALF * i:2 * S5_HALF * (i + 1)] = _dot(u_i, bcat_ref[i])
        for hd in heads:
            o_ref[rows, hcs[hd]] = _dot(atts[hd].astype(BF16), vs[hd])
        for hd in heads:
            hc = hcs[hd]
            bs_ref = bsall_ref.at[hd]
            b = bs[hd]
            b_last = _row_bcast_blocks(bs_ref, C, C - 1)
            qe_ref[rows, hc] = (qs[hd] * jnp.exp2(b)).astype(qe_ref.dtype)
            ke_ref[rows, hc] = (ks[hd] * jnp.exp2(b_last - b)).astype(ke_ref.dtype)
            v_ref[rows, hc] = vs[hd].astype(v_ref.dtype)
            for n in range(nbat):
                last = bs_ref[n * C + C - 1:n * C + C, :]
                dl_ref[rb * nbat + n, :, hc] = jnp.broadcast_to(jnp.exp2(last), (SUBLANES, HG_DK))
        return c
    lax.fori_loop(0, n_rb, intra, 0)

    width = min(S5_HALF, (SUBLANES * S5_HALF) // NB)
    for i in range(2):
        for off in range(0, S5_HALF, width):
            st = slice(S5_HALF * i + off, S5_HALF * i + off + width)
            re_c = 2 * S5_HALF * i + off
            im_c = re_c + S5_HALF
            ar = are_ref[:, st]
            ai = aim_ref[:, st]

            def step(t, carry, re_c=re_c, im_c=im_c, ar=ar, ai=ai):
                hr, hi = carry
                trows = pl.ds(pl.multiple_of(t * NB, NB), NB)
                nhr = ar * hr - ai * hi + bu_ref[trows, re_c:re_c + width]
                nhi = ar * hi + ai * hr + bu_ref[trows, im_c:im_c + width]
                bu_ref[trows, re_c:re_c + width] = nhr
                bu_ref[trows, im_c:im_c + width] = nhi
                return nhr, nhi
            hr, hi = lax.fori_loop(0, t_valid, step, (hr_ref[:, st], hi_ref[:, st]), unroll=2)
            hr_ref[:, st] = hr
            hi_ref[:, st] = hi

    def inter_out(bi):
        srows = pl.ds(pl.multiple_of(bi * C, C), C)
        for hd in heads:
            hc = hcs[hd]
            if has_state:
                oi = _dot(qe_ref[srows, hc], s0_ref[bi, hd].astype(qe_ref.dtype))
            else:
                oi = _dot_nt(qe_ref[srows, hc], s_ref[bi, hd].astype(qe_ref.dtype))
            o_ref[srows, hc] = o_ref[srows, hc] + oi

    def inter_state(bi):
        srows = pl.ds(pl.multiple_of(bi * C, C), C)
        for hd in heads:
            hc = hcs[hd]
            if has_state:
                upd = _dot_tn(ke_ref[srows, hc], v_ref[srows, hc])
                s_ref[bi, hd] = s0_ref[bi, hd] * dl_ref[bi, :, hc].T[:, 0:1] + upd
            else:
                upd = _dot_tn(v_ref[srows, hc], ke_ref[srows, hc])
                s_ref[bi, hd] = s_ref[bi, hd] * dl_ref[bi, 0:1, hc] + upd

    merge_inter = nbat <= 2

    def readout_inter(rb, c):
        rows = rows_of(rb, ROW_BLOCK)
        ys = []
        for i in range(2):
            hs = bu_ref[rows, 2 * S5_HALF * i:2 * S5_HALF * (i + 1)].astype(BF16)
            ys.append(_dot(hs, ccat_ref[i]))
        if merge_inter:
            for n in range(nbat):
                inter_out(rb * nbat + n)
        u = uz_ref[rows, 0:W_S5]
        y = jnp.concatenate(ys, axis=1) + d_ref[...] * u
        g = _gelu_tanh(y)
        glu = _dot(g.astype(BF16), wglu_ref[...]) + bglu_ref[...]
        if merge_inter:
            for n in range(nbat):
                inter_state(rb * nbat + n)
        zs = uz_ref[rows, W_S5:2 * W_S5]
        s5 = g * _sigmoid(glu) * (zs * _sigmoid(zs))
        s5_ref[rows, :] = s5.astype(BF16)
        return c
    lax.fori_loop(0, n_rb, readout_inter, 0)

    if not merge_inter:
        def inter(bi, c):
            inter_out(bi)
            inter_state(bi)
            return c
        lax.fori_loop(0, NB, inter, 0, unroll=2)

    def phase5(rb, c):
        rows = rows_of(rb)
        cat_ref[rows, 0:W_S5] = _dot(permt_ref[rows, :], s5_ref[...]).astype(BF16)
        for hd in range(HG_HEADS):
            hc = slice(HG_DV * hd, HG_DV * (hd + 1))
            o = o_ref[rows, hc]
            o = o * lax.rsqrt(jnp.mean(o * o, axis=-1, keepdims=True) + EPS)
            zh = proj_ref[rows, 1536 + HG_DV * hd:1536 + HG_DV * (hd + 1)]
            hg = o * og_ref[:, hc] * (zh * _sigmoid(zh))
            cat_ref[rows, W_S5 + HG_DV * hd:W_S5 + HG_DV * (hd + 1)] = hg.astype(BF16)
        out = _dot(cat_ref[rows, :], wout_ref[...])
        xn = x_rows(rb) + out
        yv = xn * lax.rsqrt(jnp.mean(xn * xn, axis=-1, keepdims=True) + EPS) * fg_ref[...]
        y_ref[pl.ds(rb * mbat, mbat)] = yv.reshape(mbat, C, D_MODEL)
        return c
    lax.fori_loop(0, n_mb, phase5, 0)

    if not has_state:
        @pl.when(j == nj - 1)
        def _fin():
            def tr(b, c):
                for hd in range(HG_HEADS):
                    s_ref[b, hd] = s_ref[b, hd].T
                return c
            lax.fori_loop(0, NB, tr, 0)


def _time_major_perm(NB, C):
    R = NB * C
    dst = np.arange(R)
    src = (dst % NB) * C + dst // NB
    p = np.zeros((R, R), np.float32)
    p[dst, src] = 1.0
    return p


def _mixer_call(x, state, weights, *, NB, C, t_valid, name):
    B, L, _ = x.shape
    has_state = state is not None
    assert not has_state or L == C, "a given state is consumed by a single chunk per sequence"
    grid = (B // NB, L // C)
    R = NB * C
    perm = _time_major_perm(NB, C)
    weights = (jnp.asarray(perm, BF16), jnp.asarray(perm.T, BF16)) + tuple(weights)

    def full(a):
        nd = a.ndim
        return pl.BlockSpec(a.shape, lambda i, j, nd=nd: (0,) * nd, pipeline_mode=pl.Buffered(1))

    x_spec = pl.BlockSpec((NB, C, D_MODEL), lambda i, j: (i, j, 0))
    h_spec = pl.BlockSpec((NB, S5_LANES), lambda i, j: (i, 0))
    s_spec = pl.BlockSpec((NB, HG_HEADS, HG_DK, HG_DV), lambda i, j: (i, 0, 0, 0))
    in_specs = [x_spec] + ([h_spec, h_spec, s_spec] if has_state else []) + [full(w) for w in weights]
    args = [x] + (list(state) if has_state else []) + list(weights)
    out_shape = [
        jax.ShapeDtypeStruct((B, L, D_MODEL), x.dtype),
        jax.ShapeDtypeStruct((B, S5_LANES), F32),
        jax.ShapeDtypeStruct((B, S5_LANES), F32),
        jax.ShapeDtypeStruct((B, HG_HEADS, HG_DK, HG_DV), F32),
    ]
    chunk_dt = BF16 if C % BF16_TILE_ROWS == 0 else F32
    scratch = [
        pltpu.VMEM((R, D_MODEL), BF16),
        pltpu.VMEM((R, HG_COLS), F32),
        pltpu.VMEM((R, 2 * W_S5), F32),
        pltpu.VMEM((R, 2 * S5_LANES), F32),
        pltpu.VMEM((R, W_S5), BF16),
        pltpu.VMEM((R, W_HG), F32),
        pltpu.VMEM((R, W_HG), chunk_dt),
        pltpu.VMEM((R, W_HG), chunk_dt),
        pltpu.VMEM((R, W_HG), chunk_dt),
        pltpu.VMEM((HG_HEADS, ROW_BLOCK, HG_DK), F32),
        pltpu.VMEM((ROW_BLOCK, ROW_BLOCK), jnp.int32),
        pltpu.VMEM((ROW_BLOCK, ROW_BLOCK), BF16),
        pltpu.VMEM((NB, SUBLANES, W_HG), F32),
        pltpu.VMEM((R, D_MODEL), BF16),
    ]
    return pl.pallas_call(
        functools.partial(_mixer_kernel, NB=NB, C=C, t_valid=t_valid, has_state=has_state),
        grid=grid,
        in_specs=in_specs,
        out_specs=[x_spec, h_spec, h_spec, s_spec],
        out_shape=out_shape,
        scratch_shapes=scratch,
        compiler_params=pltpu.CompilerParams(
            dimension_semantics=("arbitrary", "arbitrary"),
            vmem_limit_bytes=V7X_VMEM_LIMIT),
        name=name,
    )(*args)


def kernel(x_prompt, x_sample, state_s5_re, state_s5_im, state_hgrn, norm_g, w_in, s5_lambda_re, s5_lambda_im,
           s5_log_step, s5_b_re, s5_b_im, s5_c_re, s5_c_im, s5_d, w_glu, b_glu, hgrn_lb_logits, hgrn_onorm_g,
           w_out, final_norm_g):
    assert norm_g.shape[0] == 1, "single-layer model"
    l = 0
    a_re, a_im, bbre, bbim, lb = _prepare_params(
        s5_lambda_re[l], s5_lambda_im[l], s5_log_step[l], s5_b_re[l], s5_b_im[l], hgrn_lb_logits)
    bcat, ccat = _block_diag_weights(bbre, bbim, s5_c_re[l], s5_c_im[l])
    weights = (
        norm_g[l].astype(F32).reshape(1, D_MODEL),
        w_in[l].astype(BF16),
        a_re, a_im, bcat, ccat,
        s5_d[l].astype(F32).reshape(1, W_S5),
        w_glu[l].astype(BF16),
        b_glu[l].astype(F32).reshape(1, W_S5),
        lb,
        hgrn_onorm_g[l].astype(F32).reshape(1, W_HG),
        w_out[l].astype(BF16),
        final_norm_g.astype(F32).reshape(1, D_MODEL),
    )

    Bp, Lp, _ = x_prompt.shape
    yp, p_re, p_im, p_hg = _mixer_call(x_prompt, None, weights, NB=Bp, C=64, t_valid=64, name="mixer_prompt")

    Bs, Ls, _ = x_sample.shape
    xs = jnp.pad(x_sample, ((0, 0), (0, SAMPLE_PAD_SEQ - Ls), (0, 0)))
    state = (state_s5_re[l].reshape(Bs, S5_LANES).astype(F32),
             state_s5_im[l].reshape(Bs, S5_LANES).astype(F32),
             state_hgrn[l].astype(F32))
    ys, s_re, s_im, s_hg = _mixer_call(xs, state, weights, NB=16, C=SAMPLE_PAD_SEQ, t_valid=Ls, name="mixer_sample")
    ys = ys[:, :Ls]

    sd = state_s5_re.dtype
    shp = (1, Bp, S5_GROUPS, S5_STATE)
    shs = (1, Bs, S5_GROUPS, S5_STATE)
    return (yp, ys,
            p_re.reshape(shp).astype(sd), p_im.reshape(shp).astype(sd), p_hg[None].astype(state_hgrn.dtype),
            s_re.reshape(shs).astype(sd), s_im.reshape(shs).astype(sd), s_hg[None].astype(state_hgrn.dtype))
```

```python
import functools
import math

import jax
import jax.numpy as jnp
import numpy as np
from jax import lax
from jax.experimental import pallas as pl
from jax.experimental.pallas import tpu as pltpu

F32 = jnp.float32
BF16 = jnp.bfloat16

D_MODEL = 1024
W_S5 = 512
W_HG = 512
S5_CH = 16
S5_GROUPS = 32
S5_STATE = 64
S5_LANES = S5_GROUPS * S5_STATE
S5_HALF = S5_LANES // 2
HG_DK = 128
HG_HEADS = 4
HG_DV = 128
EPS = 1e-6
LAMBDA_RE_MAX = -1e-4
PROJ_OUT = 3072
OFF_Q = 1024
HG_COLS = PROJ_OUT - OFF_Q
LOG2E = 1.4426950408889634

LANES = 128
SUBLANES = 8
BF16_TILE_ROWS = 16
ROW_BLOCK = 128
MATMUL_ROWS = 256
V7X_VMEM_LIMIT = 60 * 1024 * 1024
SAMPLE_PAD_SEQ = 8


def _prep_kernel(lre_ref, lim_ref, ls_ref, brt_ref, bit_ref, lg_ref,
                 are_ref, aim_ref, bbre_ref, bbim_ref, lb_ref):
    lr = jnp.minimum(lre_ref[...], LAMBDA_RE_MAX)
    li = lim_ref[...]
    dt = jnp.exp(ls_ref[...])
    er = jnp.exp(lr * dt)
    a_re = er * jnp.cos(li * dt)
    a_im = er * jnp.sin(li * dt)
    den = lr * lr + li * li
    zr = ((a_re - 1.0) * lr + a_im * li) / den
    zi = (a_im * lr - (a_re - 1.0) * li) / den
    are_ref[...] = a_re
    aim_ref[...] = a_im
    for g in range(S5_GROUPS):
        br = brt_ref[g]
        bi = bit_ref[g]
        zr_g = zr[g:g + 1, :]
        zi_g = zi[g:g + 1, :]
        bbre_ref[g] = zr_g * br - zi_g * bi
        bbim_ref[g] = zr_g * bi + zi_g * br
    lg = lg_ref[...]
    e = jnp.exp(lg - jnp.max(lg, axis=0, keepdims=True))
    lb_ref[...] = e[0:1, :] / jnp.sum(e, axis=0, keepdims=True)


def _prepare_params(lam_re, lam_im, log_step, b_re, b_im, lb_logits):
    gps = jax.ShapeDtypeStruct((S5_GROUPS, S5_STATE), F32)
    gcs = jax.ShapeDtypeStruct((S5_GROUPS, S5_CH, S5_STATE), F32)
    a_re, a_im, bbre, bbim, lb = pl.pallas_call(
        _prep_kernel,
        out_shape=[gps, gps, gcs, gcs, jax.ShapeDtypeStruct((1, HG_HEADS * HG_DK), F32)],
        name="s5_hgrn_prep",
    )(lam_re.astype(F32), lam_im.astype(F32), log_step.astype(F32).reshape(S5_GROUPS, 1),
      jnp.transpose(b_re.astype(F32), (0, 2, 1)), jnp.transpose(b_im.astype(F32), (0, 2, 1)),
      lb_logits.astype(F32))
    return a_re.reshape(1, S5_LANES), a_im.reshape(1, S5_LANES), bbre, bbim, lb


def _block_diag_weights(bbre, bbim, c_re, c_im):
    hg = S5_GROUPS // 2
    eye = jnp.eye(hg, dtype=BF16)[None, :, None, :, None]

    def in_half(bbt):
        x = bbt.astype(BF16).reshape(2, hg, S5_CH, 1, S5_STATE) * eye
        return x.reshape(2, hg * S5_CH, hg * S5_STATE)

    def out_half(c):
        ct = jnp.transpose(c.astype(BF16), (0, 2, 1)).reshape(2, hg, S5_STATE, 1, S5_CH)
        return (ct * eye).reshape(2, hg * S5_STATE, hg * S5_CH)

    bcat = jnp.concatenate([in_half(bbre), in_half(bbim)], axis=2)
    ccat = jnp.concatenate([out_half(c_re), out_half(-c_im)], axis=1)
    return bcat, ccat


def _sigmoid(x):
    return 1.0 / (1.0 + jnp.exp2(x * (-LOG2E)))


def _gelu_tanh(x):
    return 0.5 * x * (1.0 + jnp.tanh(math.sqrt(2.0 / math.pi) * (x + 0.044715 * (x * x * x))))


def _dot(a, b):
    return jnp.dot(a, b, preferred_element_type=F32)


def _dot_nt(a, b):
    return lax.dot_general(a, b, (((1,), (1,)), ((), ())), preferred_element_type=F32)


def _dot_tn(a, b):
    return lax.dot_general(a, b, (((0,), (0,)), ((), ())), preferred_element_type=F32)


def _row_bcast_blocks(src_ref, blk, off):
    parts = []
    for n in range(ROW_BLOCK // blk):
        row = src_ref[n * blk + off:n * blk + off + 1, :]
        parts.append(jnp.broadcast_to(row, (blk, HG_DK)))
    return parts[0] if len(parts) == 1 else jnp.concatenate(parts, axis=0)


def _mixer_kernel(*refs, NB, C, t_valid, has_state):
    if has_state:
        x_ref, h0r_ref, h0i_ref, s0_ref = refs[:4]
        refs = refs[4:]
    else:
        x_ref = refs[0]
        refs = refs[1:]
    (perm_ref, permt_ref, ng_ref, win_ref, are_ref, aim_ref, bcat_ref, ccat_ref, d_ref, wglu_ref, bglu_ref,
     lb_ref, og_ref, wout_ref, fg_ref,
     y_ref, hr_ref, hi_ref, s_ref,
     hn_ref, proj_ref, uz_ref, bu_ref, s5_ref, o_ref, qe_ref, ke_ref, v_ref, bsall_ref, lev_ref, tri_ref,
     dl_ref, cat_ref) = refs

    j = pl.program_id(1)
    nj = pl.num_programs(1)
    R = NB * C
    n_rb = R // ROW_BLOCK
    nbat = ROW_BLOCK // C

    @pl.when(j == 0)
    def _init():
        if has_state:
            hr_ref[...] = h0r_ref[...]
            hi_ref[...] = h0i_ref[...]
        else:
            hr_ref[...] = jnp.zeros(hr_ref.shape, F32)
            hi_ref[...] = jnp.zeros(hi_ref.shape, F32)
            s_ref[...] = jnp.zeros(s_ref.shape, F32)

    MM = min(R, MATMUL_ROWS)
    n_mb = R // MM
    mbat = MM // C

    def x_rows(mb):
        return x_ref[pl.ds(mb * mbat, mbat)].reshape(MM, D_MODEL)

    def rows_of(rb, n=MM):
        return pl.ds(pl.multiple_of(rb * n, n), n)

    def phase1a(rb, c):
        rows = rows_of(rb)
        x = x_rows(rb)
        hn = x * lax.rsqrt(jnp.mean(x * x, axis=-1, keepdims=True) + EPS) * ng_ref[...]
        hn = hn.astype(BF16)
        hn_ref[rows, :] = hn
        for sec in range(HG_COLS // 512):
            cs = slice(512 * sec, 512 * (sec + 1))
            proj_ref[rows, cs] = _dot(hn, win_ref[:, OFF_Q + 512 * sec:OFF_Q + 512 * (sec + 1)])
        return c
    lax.fori_loop(0, n_mb, phase1a, 0)

    row = lax.broadcasted_iota(jnp.int32, (ROW_BLOCK, ROW_BLOCK), 0)
    col = lax.broadcasted_iota(jnp.int32, (ROW_BLOCK, ROW_BLOCK), 1)
    tin = row & (C - 1)
    levels = []
    m = 1
    while m < C:
        levels.append(m)
        m *= 2
    lev = jnp.where(row == col, 0, -1)
    for li, m in enumerate(levels):
        same = (row & ~(2 * m - 1)) == (col & ~(2 * m - 1))
        own = same & ((row & m) != 0) & ((col & m) == 0)
        lev = jnp.where(own, li + 1, lev)
    lev_ref[...] = lev
    in_chunk = (row & ~(C - 1)) == (col & ~(C - 1))
    tri_ref[...] = jnp.where(in_chunk & (col <= row), 1.0, 0.0).astype(BF16)

    heads = range(HG_HEADS)
    hcs = [slice(HG_DK * hd, HG_DK * (hd + 1)) for hd in heads]
    scan_width = min(S5_HALF, (SUBLANES * S5_HALF) // NB)
    assert t_valid == C or n_rb == 1
    block_steps = min(ROW_BLOCK // NB, t_valid)

    def intra(rb, c):
        rows = rows_of(rb, ROW_BLOCK)
        lv = lev_ref[...]
        hp = _dot(perm_ref[rows, :], hn_ref[...]).astype(BF16)
        qs, ks, fs, vs, gcats = [], [], [], [], []
        for hd in heads:
            lb = lb_ref[:, hcs[hd]]
            q = proj_ref[rows, HG_DK * hd:HG_DK * (hd + 1)]
            gf = proj_ref[rows, 512 + HG_DK * hd:512 + HG_DK * (hd + 1)]
            v = proj_ref[rows, 1024 + HG_DV * hd:1024 + HG_DV * (hd + 1)].astype(BF16)
            f = lb + (1.0 - lb) * _sigmoid(gf)
            g = jnp.log2(f)
            k = 1.0 - f
            if t_valid < C:
                live = tin < t_valid
                f = jnp.where(live, f, 1.0)
                g = jnp.where(live, g, 0.0)
                k = jnp.where(live, k, 0.0)
            g1 = g.astype(BF16)
            r1 = g - g1.astype(F32)
            g2 = r1.astype(BF16)
            g3 = (r1 - g2.astype(F32)).astype(BF16)
            qs.append(q)
            ks.append(k)
            fs.append(f)
            vs.append(v)
            gcats.append(jnp.concatenate([g1, g2, g3], axis=1))
        bparts = [_dot(tri_ref[...], gcats[hd]) for hd in heads]
        uz = [_dot(hp, win_ref[:, W_S5 * sec:W_S5 * (sec + 1)]) for sec in range(2)]
        for sec in range(2):
            uz_ref[rows, W_S5 * sec:W_S5 * (sec + 1)] = uz[sec]
        bs = []
        for hd in heads:
            bp = bparts[hd]
            b = (bp[:, 0:HG_DK] + bp[:, HG_DK:2 * HG_DK]) + bp[:, 2 * HG_DK:3 * HG_DK]
            bsall_ref[hd] = b
            bs.append(b)
        qbs = [q.astype(BF16) for q in qs]
        kbs = [k.astype(BF16) for k in ks]
        atts = [jnp.where(lv == 0, _dot_nt(qbs[hd], kbs[hd]), 0.0) for hd in heads]
        for li, m in enumerate(levels):
            two_m = 2 * m
            p2 = row & (two_m - 1)
            if m > 2:
                sgn = jnp.where(p2 >= m, 1.0, -1.0)
            for hd in heads:
                b = bs[hd]
                f = fs[hd]
                if m == 1:
                    dec = jnp.where(p2 == 1, f, 1.0)
                elif m == 2:
                    dec = jnp.where(p2 == 0, pltpu.roll(f, ROW_BLOCK - 1, axis=0),
                                    jnp.where(p2 == 1, 1.0,
                                              jnp.where(p2 == 2, f, f * pltpu.roll(f, 1, axis=0))))
                else:
                    ref = _row_bcast_blocks(bsall_ref.at[hd], two_m, m - 1)
                    dec = jnp.exp2((b - ref) * sgn)
                decb = dec.astype(BF16)
                atts[hd] = jnp.where(lv == li + 1, _dot_nt(qbs[hd] * decb, kbs[hd] * decb), atts[hd])
            if li == 0:
                for i in range(2):
                    u_i = uz[0][:, 256 * i:256 * (i + 1)].astype(BF16)
                    bu_ref[rows, 2 * S5_HALF * i:2 * S5_HALF * (i + 1)] = _dot(u_i, bcat_ref[i])
        for hd in heads:
            o_ref[rows, hcs[hd]] = _dot(atts[hd].astype(BF16), vs[hd])
        for hd in heads:
            hc = hcs[hd]
            bs_ref = bsall_ref.at[hd]
            b = bs[hd]
            b_last = _row_bcast_blocks(bs_ref, C, C - 1)
            qe_ref[rows, hc] = (qs[hd] * jnp.exp2(b)).astype(qe_ref.dtype)
            ke_ref[rows, hc] = (ks[hd] * jnp.exp2(b_last - b)).astype(ke_ref.dtype)
            v_ref[rows, hc] = vs[hd].astype(v_ref.dtype)
            for n in range(nbat):
                last = bs_ref[n * C + C - 1:n * C + C, :]
                dl_ref[rb * nbat + n, :, hc] = jnp.broadcast_to(jnp.exp2(last), (SUBLANES, HG_DK))
        for i in range(2):
            for off in range(0, S5_HALF, scan_width):
                st = slice(S5_HALF * i + off, S5_HALF * i + off + scan_width)
                re_c = 2 * S5_HALF * i + off
                im_c = re_c + S5_HALF
                ar = are_ref[:, st]
                ai = aim_ref[:, st]
                hr = hr_ref[:, st]
                hi = hi_ref[:, st]
                for tt in range(block_steps):
                    trows = pl.ds(pl.multiple_of(rb * ROW_BLOCK + tt * NB, NB), NB)
                    hr, hi = (ar * hr - ai * hi + bu_ref[trows, re_c:re_c + scan_width],
                              ar * hi + ai * hr + bu_ref[trows, im_c:im_c + scan_width])
                    bu_ref[trows, re_c:re_c + scan_width] = hr
                    bu_ref[trows, im_c:im_c + scan_width] = hi
                hr_ref[:, st] = hr
                hi_ref[:, st] = hi
        return c
    lax.fori_loop(0, n_rb, intra, 0)

    def inter_out(bi):
        srows = pl.ds(pl.multiple_of(bi * C, C), C)
        for hd in heads:
            hc = hcs[hd]
            if has_state:
                oi = _dot(qe_ref[srows, hc], s0_ref[bi, hd].astype(qe_ref.dtype))
            else:
                oi = _dot_nt(qe_ref[srows, hc], s_ref[bi, hd].astype(qe_ref.dtype))
            o_ref[srows, hc] = o_ref[srows, hc] + oi

    def inter_state(bi):
        srows = pl.ds(pl.multiple_of(bi * C, C), C)
        for hd in heads:
            hc = hcs[hd]
            if has_state:
                upd = _dot_tn(ke_ref[srows, hc], v_ref[srows, hc])
                s_ref[bi, hd] = s0_ref[bi, hd] * dl_ref[bi, :, hc].T[:, 0:1] + upd
            else:
                upd = _dot_tn(v_ref[srows, hc], ke_ref[srows, hc])
                s_ref[bi, hd] = s_ref[bi, hd] * dl_ref[bi, 0:1, hc] + upd

    merge_inter = nbat <= 2

    def readout_inter(rb, c):
        rows = rows_of(rb, ROW_BLOCK)
        ys = []
        for i in range(2):
            hs = bu_ref[rows, 2 * S5_HALF * i:2 * S5_HALF * (i + 1)].astype(BF16)
            ys.append(_dot(hs, ccat_ref[i]))
        if merge_inter:
            for n in range(nbat):
                inter_out(rb * nbat + n)
        u = uz_ref[rows, 0:W_S5]
        y = jnp.concatenate(ys, axis=1) + d_ref[...] * u
        g = _gelu_tanh(y)
        glu = _dot(g.astype(BF16), wglu_ref[...]) + bglu_ref[...]
        if merge_inter:
            for n in range(nbat):
                inter_state(rb * nbat + n)
        zs = uz_ref[rows, W_S5:2 * W_S5]
        s5 = g * _sigmoid(glu) * (zs * _sigmoid(zs))
        s5_ref[rows, :] = s5.astype(BF16)
        return c
    lax.fori_loop(0, n_rb, readout_inter, 0)

    if not merge_inter:
        def inter(bi, c):
            inter_out(bi)
            inter_state(bi)
            return c
        lax.fori_loop(0, NB, inter, 0, unroll=2)

    def phase5(rb, c):
        rows = rows_of(rb)
        cat_ref[rows, 0:W_S5] = _dot(permt_ref[rows, :], s5_ref[...]).astype(BF16)
        for hd in range(HG_HEADS):
            hc = slice(HG_DV * hd, HG_DV * (hd + 1))
            o = o_ref[rows, hc]
            o = o * lax.rsqrt(jnp.mean(o * o, axis=-1, keepdims=True) + EPS)
            zh = proj_ref[rows, 1536 + HG_DV * hd:1536 + HG_DV * (hd + 1)]
            hg = o * og_ref[:, hc] * (zh * _sigmoid(zh))
            cat_ref[rows, W_S5 + HG_DV * hd:W_S5 + HG_DV * (hd + 1)] = hg.astype(BF16)
        out = _dot(cat_ref[rows, :], wout_ref[...])
        xn = x_rows(rb) + out
        yv = xn * lax.rsqrt(jnp.mean(xn * xn, axis=-1, keepdims=True) + EPS) * fg_ref[...]
        y_ref[pl.ds(rb * mbat, mbat)] = yv.reshape(mbat, C, D_MODEL)
        return c
    lax.fori_loop(0, n_mb, phase5, 0)

    if not has_state:
        @pl.when(j == nj - 1)
        def _fin():
            def tr(b, c):
                for hd in range(HG_HEADS):
                    s_ref[b, hd] = s_ref[b, hd].T
                return c
            lax.fori_loop(0, NB, tr, 0)


def _time_major_perm(NB, C):
    R = NB * C
    dst = np.arange(R)
    src = (dst % NB) * C + dst // NB
    p = np.zeros((R, R), np.float32)
    p[dst, src] = 1.0
    return p


def _mixer_call(x, state, weights, *, NB, C, t_valid, name):
    B, L, _ = x.shape
    has_state = state is not None
    assert not has_state or L == C, "a given state is consumed by a single chunk per sequence"
    grid = (B // NB, L // C)
    R = NB * C
    perm = _time_major_perm(NB, C)
    weights = (jnp.asarray(perm, BF16), jnp.asarray(perm.T, BF16)) + tuple(weights)

    def full(a):
        nd = a.ndim
        return pl.BlockSpec(a.shape, lambda i, j, nd=nd: (0,) * nd, pipeline_mode=pl.Buffered(1))

    x_spec = pl.BlockSpec((NB, C, D_MODEL), lambda i, j: (i, j, 0))
    h_spec = pl.BlockSpec((NB, S5_LANES), lambda i, j: (i, 0))
    s_spec = pl.BlockSpec((NB, HG_HEADS, HG_DK, HG_DV), lambda i, j: (i, 0, 0, 0))
    in_specs = [x_spec] + ([h_spec, h_spec, s_spec] if has_state else []) + [full(w) for w in weights]
    args = [x] + (list(state) if has_state else []) + list(weights)
    out_shape = [
        jax.ShapeDtypeStruct((B, L, D_MODEL), x.dtype),
        jax.ShapeDtypeStruct((B, S5_LANES), F32),
        jax.ShapeDtypeStruct((B, S5_LANES), F32),
        jax.ShapeDtypeStruct((B, HG_HEADS, HG_DK, HG_DV), F32),
    ]
    chunk_dt = BF16 if C % BF16_TILE_ROWS == 0 else F32
    scratch = [
        pltpu.VMEM((R, D_MODEL), BF16),
        pltpu.VMEM((R, HG_COLS), F32),
        pltpu.VMEM((R, 2 * W_S5), F32),
        pltpu.VMEM((R, 2 * S5_LANES), F32),
        pltpu.VMEM((R, W_S5), BF16),
        pltpu.VMEM((R, W_HG), F32),
        pltpu.VMEM((R, W_HG), chunk_dt),
        pltpu.VMEM((R, W_HG), chunk_dt),
        pltpu.VMEM((R, W_HG), chunk_dt),
        pltpu.VMEM((HG_HEADS, ROW_BLOCK, HG_DK), F32),
        pltpu.VMEM((ROW_BLOCK, ROW_BLOCK), jnp.int32),
        pltpu.VMEM((ROW_BLOCK, ROW_BLOCK), BF16),
        pltpu.VMEM((NB, SUBLANES, W_HG), F32),
        pltpu.VMEM((R, D_MODEL), BF16),
    ]
    return pl.pallas_call(
        functools.partial(_mixer_kernel, NB=NB, C=C, t_valid=t_valid, has_state=has_state),
        grid=grid,
        in_specs=in_specs,
        out_specs=[x_spec, h_spec, h_spec, s_spec],
        out_shape=out_shape,
        scratch_shapes=scratch,
        compiler_params=pltpu.CompilerParams(
            dimension_semantics=("arbitrary", "arbitrary"),
            vmem_limit_bytes=V7X_VMEM_LIMIT),
        name=name,
    )(*args)


def kernel(x_prompt, x_sample, state_s5_re, state_s5_im, state_hgrn, norm_g, w_in, s5_lambda_re, s5_lambda_im,
           s5_log_step, s5_b_re, s5_b_im, s5_c_re, s5_c_im, s5_d, w_glu, b_glu, hgrn_lb_logits, hgrn_onorm_g,
           w_out, final_norm_g):
    assert norm_g.shape[0] == 1, "single-layer model"
    l = 0
    a_re, a_im, bbre, bbim, lb = _prepare_params(
        s5_lambda_re[l], s5_lambda_im[l], s5_log_step[l], s5_b_re[l], s5_b_im[l], hgrn_lb_logits)
    bcat, ccat = _block_diag_weights(bbre, bbim, s5_c_re[l], s5_c_im[l])
    weights = (
        norm_g[l].astype(F32).reshape(1, D_MODEL),
        w_in[l].astype(BF16),
        a_re, a_im, bcat, ccat,
        s5_d[l].astype(F32).reshape(1, W_S5),
        w_glu[l].astype(BF16),
        b_glu[l].astype(F32).reshape(1, W_S5),
        lb,
        hgrn_onorm_g[l].astype(F32).reshape(1, W_HG),
        w_out[l].astype(BF16),
        final_norm_g.astype(F32).reshape(1, D_MODEL),
    )

    Bp, Lp, _ = x_prompt.shape
    yp, p_re, p_im, p_hg = _mixer_call(x_prompt, None, weights, NB=Bp, C=64, t_valid=64, name="mixer_prompt")

    Bs, Ls, _ = x_sample.shape
    xs = jnp.pad(x_sample, ((0, 0), (0, SAMPLE_PAD_SEQ - Ls), (0, 0)))
    state = (state_s5_re[l].reshape(Bs, S5_LANES).astype(F32),
             state_s5_im[l].reshape(Bs, S5_LANES).astype(F32),
             state_hgrn[l].astype(F32))
    ys, s_re, s_im, s_hg = _mixer_call(xs, state, weights, NB=16, C=SAMPLE_PAD_SEQ, t_valid=Ls, name="mixer_sample")
    ys = ys[:, :Ls]

    sd = state_s5_re.dtype
    shp = (1, Bp, S5_GROUPS, S5_STATE)
    shs = (1, Bs, S5_GROUPS, S5_STATE)
    return (yp, ys,
            p_re.reshape(shp).astype(sd), p_im.reshape(shp).astype(sd), p_hg[None].astype(state_hgrn.dtype),
            s_re.reshape(shs).astype(sd), s_im.reshape(shs).astype(sd), s_hg[None].astype(state_hgrn.dtype))
```

```python
import functools
import math

import jax
import jax.numpy as jnp
import numpy as np
from jax import lax
from jax.experimental import pallas as pl
from jax.experimental.pallas import tpu as pltpu

F32 = jnp.float32
BF16 = jnp.bfloat16

D_MODEL = 1024
W_S5 = 512
W_HG = 512
S5_CH = 16
S5_GROUPS = 32
S5_STATE = 64
S5_LANES = S5_GROUPS * S5_STATE
S5_HALF = S5_LANES // 2
HG_DK = 128
HG_HEADS = 4
HG_DV = 128
EPS = 1e-6
LAMBDA_RE_MAX = -1e-4
PROJ_OUT = 3072
OFF_Q = 1024
HG_COLS = PROJ_OUT - OFF_Q
LOG2E = 1.4426950408889634

LANES = 128
SUBLANES = 8
BF16_TILE_ROWS = 16
ROW_BLOCK = 128
MATMUL_ROWS = 512
V7X_VMEM_LIMIT = 60 * 1024 * 1024


def _prep_kernel(lre_ref, lim_ref, ls_ref, brt_ref, bit_ref, lg_ref,
                 are_ref, aim_ref, bbre_ref, bbim_ref, lb_ref):
    lr = jnp.minimum(lre_ref[...], LAMBDA_RE_MAX)
    li = lim_ref[...]
    dt = jnp.exp(ls_ref[...])
    er = jnp.exp(lr * dt)
    a_re = er * jnp.cos(li * dt)
    a_im = er * jnp.sin(li * dt)
    den = lr * lr + li * li
    zr = ((a_re - 1.0) * lr + a_im * li) / den
    zi = (a_im * lr - (a_re - 1.0) * li) / den
    are_ref[...] = a_re
    aim_ref[...] = a_im
    for g in range(S5_GROUPS):
        br = brt_ref[g]
        bi = bit_ref[g]
        zr_g = zr[g:g + 1, :]
        zi_g = zi[g:g + 1, :]
        bbre_ref[g] = zr_g * br - zi_g * bi
        bbim_ref[g] = zr_g * bi + zi_g * br
    lg = lg_ref[...]
    e = jnp.exp(lg - jnp.max(lg, axis=0, keepdims=True))
    lb_ref[...] = e[0:1, :] / jnp.sum(e, axis=0, keepdims=True)


def _prepare_params(lam_re, lam_im, log_step, b_re, b_im, lb_logits):
    gps = jax.ShapeDtypeStruct((S5_GROUPS, S5_STATE), F32)
    gcs = jax.ShapeDtypeStruct((S5_GROUPS, S5_CH, S5_STATE), F32)
    a_re, a_im, bbre, bbim, lb = pl.pallas_call(
        _prep_kernel,
        out_shape=[gps, gps, gcs, gcs, jax.ShapeDtypeStruct((1, HG_HEADS * HG_DK), F32)],
        name="s5_hgrn_prep",
    )(lam_re.astype(F32), lam_im.astype(F32), log_step.astype(F32).reshape(S5_GROUPS, 1),
      jnp.transpose(b_re.astype(F32), (0, 2, 1)), jnp.transpose(b_im.astype(F32), (0, 2, 1)),
      lb_logits.astype(F32))
    return a_re.reshape(1, S5_LANES), a_im.reshape(1, S5_LANES), bbre, bbim, lb


def _block_diag_weights(bbre, bbim, c_re, c_im):
    hg = S5_GROUPS // 2
    eye = jnp.eye(hg, dtype=BF16)[None, :, None, :, None]

    def in_half(bbt):
        x = bbt.astype(BF16).reshape(2, hg, S5_CH, 1, S5_STATE) * eye
        return x.reshape(2, hg * S5_CH, hg * S5_STATE)

    def out_half(c):
        ct = jnp.transpose(c.astype(BF16), (0, 2, 1)).reshape(2, hg, S5_STATE, 1, S5_CH)
        return (ct * eye).reshape(2, hg * S5_STATE, hg * S5_CH)

    bcat = jnp.concatenate([in_half(bbre), in_half(bbim)], axis=2)
    ccat = jnp.concatenate([out_half(c_re), out_half(-c_im)], axis=1)
    return bcat, ccat


def _sigmoid(x):
    return 1.0 / (1.0 + jnp.exp2(x * (-LOG2E)))


def _gelu_tanh(x):
    return 0.5 * x * (1.0 + jnp.tanh(math.sqrt(2.0 / math.pi) * (x + 0.044715 * (x * x * x))))


def _dot(a, b):
    return jnp.dot(a, b, preferred_element_type=F32)


def _dot_nt(a, b):
    return lax.dot_general(a, b, (((1,), (1,)), ((), ())), preferred_element_type=F32)


def _dot_tn(a, b):
    return lax.dot_general(a, b, (((0,), (0,)), ((), ())), preferred_element_type=F32)


def _row_bcast_blocks(src_ref, blk, off):
    parts = []
    for n in range(ROW_BLOCK // blk):
        row = src_ref[n * blk + off:n * blk + off + 1, :]
        parts.append(jnp.broadcast_to(row, (blk, HG_DK)))
    return parts[0] if len(parts) == 1 else jnp.concatenate(parts, axis=0)


def _mixer_kernel(*refs, NB, C, t_valid, has_state):
    if has_state:
        x_ref, h0r_ref, h0i_ref, s0_ref = refs[:4]
        refs = refs[4:]
    else:
        x_ref = refs[0]
        refs = refs[1:]
    (perm_ref, permt_ref, ng_ref, win_ref, are_ref, aim_ref, bcat_ref, ccat_ref, d_ref, wglu_ref, bglu_ref,
     lb_ref, og_ref, wout_ref, fg_ref,
     y_ref, hr_ref, hi_ref, s_ref,
     hn_ref, proj_ref, uz_ref, bu_ref, s5_ref, o_ref, qe_ref, ke_ref, v_ref, bsall_ref, lev_ref, tri_ref,
     dl_ref, cat_ref) = refs

    j = pl.program_id(1)
    nj = pl.num_programs(1)
    R = NB * C
    n_rb = R // ROW_BLOCK
    nbat = ROW_BLOCK // C

    @pl.when(j == 0)
    def _init():
        if has_state:
            hr_ref[...] = h0r_ref[...]
            hi_ref[...] = h0i_ref[...]
        else:
            hr_ref[...] = jnp.zeros(hr_ref.shape, F32)
            hi_ref[...] = jnp.zeros(hi_ref.shape, F32)
            s_ref[...] = jnp.zeros(s_ref.shape, F32)

    MM = min(R, MATMUL_ROWS)
    n_mb = R // MM
    mbat = MM // C

    def rows_of(rb, n=MM):
        return pl.ds(pl.multiple_of(rb * n, n), n)

    flat_rows = len(x_ref.shape) == 2

    def x_rows(mb):
        if flat_rows:
            return x_ref[rows_of(mb), :]
        return x_ref[pl.ds(mb * mbat, mbat)].reshape(MM, D_MODEL)

    def phase1a(rb, c):
        rows = rows_of(rb)
        x = x_rows(rb)
        hn = x * lax.rsqrt(jnp.mean(x * x, axis=-1, keepdims=True) + EPS) * ng_ref[...]
        hn = hn.astype(BF16)
        hn_ref[rows, :] = hn
        for sec in range(HG_COLS // 512):
            cs = slice(512 * sec, 512 * (sec + 1))
            proj_ref[rows, cs] = _dot(hn, win_ref[:, OFF_Q + 512 * sec:OFF_Q + 512 * (sec + 1)])
        return c
    lax.fori_loop(0, n_mb, phase1a, 0)

    row = lax.broadcasted_iota(jnp.int32, (ROW_BLOCK, ROW_BLOCK), 0)
    col = lax.broadcasted_iota(jnp.int32, (ROW_BLOCK, ROW_BLOCK), 1)
    tin = row & (C - 1)
    levels = []
    m = 1
    while m < C:
        levels.append(m)
        m *= 2
    lev = jnp.where(row == col, 0, -1)
    for li, m in enumerate(levels):
        same = (row & ~(2 * m - 1)) == (col & ~(2 * m - 1))
        own = same & ((row & m) != 0) & ((col & m) == 0)
        lev = jnp.where(own, li + 1, lev)
    lev_ref[...] = lev
    in_chunk = (row & ~(C - 1)) == (col & ~(C - 1))
    tri_ref[...] = jnp.where(in_chunk & (col <= row), 1.0, 0.0).astype(BF16)

    heads = range(HG_HEADS)
    hcs = [slice(HG_DK * hd, HG_DK * (hd + 1)) for hd in heads]
    scan_width = min(S5_HALF, (SUBLANES * S5_HALF) // NB)
    assert t_valid == C or n_rb == 1
    block_steps = min(ROW_BLOCK // NB, t_valid)

    def intra(rb, c):
        rows = rows_of(rb, ROW_BLOCK)
        lv = lev_ref[...]
        hp = _dot(perm_ref[rows, :], hn_ref[...]).astype(BF16)
        qs, ks, fs, vs, gcats = [], [], [], [], []
        for hd in heads:
            lb = lb_ref[:, hcs[hd]]
            q = proj_ref[rows, HG_DK * hd:HG_DK * (hd + 1)]
            gf = proj_ref[rows, 512 + HG_DK * hd:512 + HG_DK * (hd + 1)]
            v = proj_ref[rows, 1024 + HG_DV * hd:1024 + HG_DV * (hd + 1)].astype(BF16)
            f = lb + (1.0 - lb) * _sigmoid(gf)
            g = jnp.log2(f)
            k = 1.0 - f
            if t_valid < C:
                live = tin < t_valid
                f = jnp.where(live, f, 1.0)
                g = jnp.where(live, g, 0.0)
                k = jnp.where(live, k, 0.0)
            g1 = g.astype(BF16)
            r1 = g - g1.astype(F32)
            g2 = r1.astype(BF16)
            g3 = (r1 - g2.astype(F32)).astype(BF16)
            p4 = row & 3
            dec1 = jnp.where((row & 1) == 1, f, 1.0)
            dec2 = jnp.where(p4 == 0, pltpu.roll(f, ROW_BLOCK - 1, axis=0),
                             jnp.where(p4 == 1, 1.0,
                                       jnp.where(p4 == 2, f, f * pltpu.roll(f, 1, axis=0))))
            qs.append(q)
            ks.append(k)
            fs.append((dec1.astype(BF16), dec2.astype(BF16)))
            vs.append(v)
            gcats.append(jnp.concatenate([g1, g2, g3], axis=1))
        bparts = [_dot(tri_ref[...], gcats[hd]) for hd in heads]
        uz = [_dot(hp, win_ref[:, W_S5 * sec:W_S5 * (sec + 1)]) for sec in range(2)]
        for sec in range(2):
            uz_ref[rows, W_S5 * sec:W_S5 * (sec + 1)] = uz[sec]
        bs = []
        for hd in heads:
            bp = bparts[hd]
            b = (bp[:, 0:HG_DK] + bp[:, HG_DK:2 * HG_DK]) + bp[:, 2 * HG_DK:3 * HG_DK]
            bsall_ref[hd] = b
            bs.append(b)
        qbs = [q.astype(BF16) for q in qs]
        kbs = [k.astype(BF16) for k in ks]
        atts =[jnp.where(lv == 0, _dot_nt(qbs[hd], kbs[hd]), 0.0) for hd in heads]
        for li, m in enumerate(levels):
            two_m = 2 * m
            p2 = row & (two_m - 1)
            if m > 2:
                sgn = jnp.where(p2 >= m, 1.0, -1.0)
            for hd in heads:
                if m <= 2:
                    decb = fs[hd][li]
                else:
                    ref = _row_bcast_blocks(bsall_ref.at[hd], two_m, m - 1)
                    decb = jnp.exp2((bs[hd] - ref) * sgn).astype(BF16)
                atts[hd] = jnp.where(lv == li + 1, _dot_nt(qbs[hd] * decb, kbs[hd] * decb), atts[hd])
            if li == 0:
                for i in range(2):
                    u_i = uz[0][:, 256 * i:256 * (i + 1)].astype(BF16)
                    bu_ref[rows, 2 * S5_HALF * i:2 * S5_HALF * (i + 1)] = _dot(u_i, bcat_ref[i])
        for hd in heads:
            o_ref[rows, hcs[hd]] = _dot(atts[hd].astype(BF16), vs[hd])
        for hd in heads:
            hc = hcs[hd]
            bs_ref = bsall_ref.at[hd]
            b = bs[hd]
            if C >= SUBLANES:
                b_last = _row_bcast_blocks(bs_ref, C, C - 1)
            else:
                b_last = b
                for d in range(1, C):
                    b_last = jnp.where(tin == C - 1 - d, pltpu.roll(b, ROW_BLOCK - d, axis=0), b_last)
            qe_ref[rows, hc] = (qs[hd] * jnp.exp2(b)).astype(qe_ref.dtype)
            ke_ref[rows, hc] = (ks[hd] * jnp.exp2(b_last - b)).astype(ke_ref.dtype)
            v_ref[rows, hc] = vs[hd].astype(v_ref.dtype)
            for n in range(nbat):
                last = bs_ref[n * C + C - 1:n * C + C, :]
                dl_ref[rb * nbat + n, :, hc] = jnp.broadcast_to(jnp.exp2(last), (SUBLANES, HG_DK))
        for i in range(2):
            for off in range(0, S5_HALF, scan_width):
                st = slice(S5_HALF * i + off, S5_HALF * i + off + scan_width)
                re_c = 2 * S5_HALF * i + off
                im_c = re_c + S5_HALF
                ar = are_ref[:, st]
                ai = aim_ref[:, st]
                hr = hr_ref[:, st]
                hi = hi_ref[:, st]
                for tt in range(block_steps):
                    trows = pl.ds(pl.multiple_of(rb * ROW_BLOCK + tt * NB, NB), NB)
                    hr, hi = (ar * hr - ai * hi + bu_ref[trows, re_c:re_c + scan_width],
                              ar * hi + ai * hr + bu_ref[trows, im_c:im_c + scan_width])
                    bu_ref[trows, re_c:re_c + scan_width] = hr
                    bu_ref[trows, im_c:im_c + scan_width] = hi
                hr_ref[:, st] = hr
                hi_ref[:, st] = hi
        return c
    lax.fori_loop(0, n_rb, intra, 0)

    def inter_out(bi):
        srows = pl.ds(pl.multiple_of(bi * C, C), C)
        for hd in heads:
            hc = hcs[hd]
            if has_state:
                oi = _dot(qe_ref[srows, hc], s0_ref[bi, hd].astype(qe_ref.dtype))
            else:
                oi = _dot_nt(qe_ref[srows, hc], s_ref[bi, hd].astype(qe_ref.dtype))
            o_ref[srows, hc] = o_ref[srows, hc] + oi

    def inter_state(bi):
        srows = pl.ds(pl.multiple_of(bi * C, C), C)
        for hd in heads:
            hc = hcs[hd]
            if has_state:
                upd = _dot_tn(ke_ref[srows, hc], v_ref[srows, hc])
                s_ref[bi, hd] = s0_ref[bi, hd] * dl_ref[bi, :, hc].T[:, 0:1] + upd
            else:
                upd = _dot_tn(v_ref[srows, hc], ke_ref[srows, hc])
                s_ref[bi, hd] = s_ref[bi, hd] * dl_ref[bi, 0:1, hc] + upd

    merge_inter = nbat <= 2

    def readout_inter(rb, c):
        rows = rows_of(rb, ROW_BLOCK)
        ys = []
        for i in range(2):
            hs = bu_ref[rows, 2 * S5_HALF * i:2 * S5_HALF * (i + 1)].astype(BF16)
            ys.append(_dot(hs, ccat_ref[i]))
        if merge_inter:
            for n in range(nbat):
                inter_out(rb * nbat + n)
        u = uz_ref[rows, 0:W_S5]
        y = jnp.concatenate(ys, axis=1) + d_ref[...] * u
        g = _gelu_tanh(y)
        glu = _dot(g.astype(BF16), wglu_ref[...]) + bglu_ref[...]
        if merge_inter:
            for n in range(nbat):
                inter_state(rb * nbat + n)
        zs = uz_ref[rows, W_S5:2 * W_S5]
        s5 = g * _sigmoid(glu) * (zs * _sigmoid(zs))
        s5_ref[rows, :] = s5.astype(BF16)
        return c
    lax.fori_loop(0, n_rb, readout_inter, 0)

    if not merge_inter and C >= SUBLANES:
        def inter(bi, c):
            inter_out(bi)
            inter_state(bi)
            return c
        lax.fori_loop(0, NB, inter, 0, unroll=2)

    if C < SUBLANES:
        assert has_state and not merge_inter
        per_tile = SUBLANES // C
        rid = lax.broadcasted_iota(jnp.int32, (SUBLANES, HG_DK), 0)

        def inter_tile(ti, c):
            srows = pl.ds(pl.multiple_of(ti * SUBLANES, SUBLANES), SUBLANES)
            for hd in heads:
                hc = hcs[hd]
                qe8 = qe_ref[srows, hc]
                ke8 = ke_ref[srows, hc]
                v8 = v_ref[srows, hc]
                acc = o_ref[srows, hc]
                for s in range(per_tile):
                    bi = ti * per_tile + s
                    mine = (rid >= s * C) & (rid < (s + 1) * C)
                    st = s0_ref[bi, hd]
                    acc = acc + _dot(jnp.where(mine, qe8, 0.0), st)
                    upd = _dot_tn(jnp.where(mine, ke8, 0.0), v8)
                    s_ref[bi, hd] = st * dl_ref[bi, :, hc].T[:, 0:1] + upd
                o_ref[srows, hc] = acc
            return c
        lax.fori_loop(0, NB // per_tile, inter_tile, 0, unroll=2)

    def phase5(rb, c):
        rows = rows_of(rb)
        cat_ref[rows, 0:W_S5] = _dot(permt_ref[rows, :], s5_ref[...]).astype(BF16)
        for hd in range(HG_HEADS):
            hc = slice(HG_DV * hd, HG_DV * (hd + 1))
            o = o_ref[rows, hc]
            o = o * lax.rsqrt(jnp.mean(o * o, axis=-1, keepdims=True) + EPS)
            zh = proj_ref[rows, 1536 + HG_DV * hd:1536 + HG_DV * (hd + 1)]
            hg = o * og_ref[:, hc] * (zh * _sigmoid(zh))
            cat_ref[rows, W_S5 + HG_DV * hd:W_S5 + HG_DV * (hd + 1)] = hg.astype(BF16)
        out = _dot(cat_ref[rows, :], wout_ref[...])
        xn = x_rows(rb) + out
        yv = xn * lax.rsqrt(jnp.mean(xn * xn, axis=-1, keepdims=True) + EPS) * fg_ref[...]
        if flat_rows:
            y_ref[rows, :] = yv
        else:
            y_ref[pl.ds(rb * mbat, mbat)] = yv.reshape(mbat, C, D_MODEL)
        return c
    lax.fori_loop(0, n_mb, phase5, 0)

    if not has_state:
        @pl.when(j == nj - 1)
        def _fin():
            def tr(b, c):
                for hd in range(HG_HEADS):
                    s_ref[b, hd] = s_ref[b, hd].T
                return c
            lax.fori_loop(0, NB, tr, 0)


def _time_major_perm(NB, C):
    R = NB * C
    dst = np.arange(R)
    src = (dst % NB) * C + dst // NB
    p = np.zeros((R, R), np.float32)
    p[dst, src] = 1.0
    return p


def _mixer_call(x, state, weights, *, NB, C, t_valid, name):
    B, L, _ = x.shape
    has_state = state is not None
    assert not has_state or L == C, "a given state is consumed by a single chunk per sequence"
    grid = (B // NB, L // C)
    R = NB * C
    perm = _time_major_perm(NB, C)
    weights = (jnp.asarray(perm, BF16), jnp.asarray(perm.T, BF16)) + tuple(weights)

    def full(a):
        nd = a.ndim
        return pl.BlockSpec(a.shape, lambda i, j, nd=nd: (0,) * nd, pipeline_mode=pl.Buffered(1))

    flat_rows = C % SUBLANES != 0
    if flat_rows:
        assert L == C
        x = x.reshape(B * L, D_MODEL)
        x_spec = pl.BlockSpec((R, D_MODEL), lambda i, j: (i, 0))
    else:
        x_spec = pl.BlockSpec((NB, C, D_MODEL), lambda i, j: (i, j, 0))
    h_spec = pl.BlockSpec((NB, S5_LANES), lambda i, j: (i, 0))
    s_spec = pl.BlockSpec((NB, HG_HEADS, HG_DK, HG_DV), lambda i, j: (i, 0, 0, 0))
    in_specs = [x_spec] + ([h_spec, h_spec, s_spec] if has_state else []) + [full(w) for w in weights]
    args = [x] + (list(state) if has_state else []) + list(weights)
    out_shape = [
        jax.ShapeDtypeStruct(x.shape, x.dtype),
        jax.ShapeDtypeStruct((B, S5_LANES), F32),
        jax.ShapeDtypeStruct((B, S5_LANES), F32),
        jax.ShapeDtypeStruct((B, HG_HEADS, HG_DK, HG_DV), F32),
    ]
    chunk_dt = BF16 if C % BF16_TILE_ROWS == 0 else F32
    scratch = [
        pltpu.VMEM((R, D_MODEL), BF16),
        pltpu.VMEM((R, HG_COLS), F32),
        pltpu.VMEM((R, 2 * W_S5), F32),
        pltpu.VMEM((R, 2 * S5_LANES), F32),
        pltpu.VMEM((R, W_S5), BF16),
        pltpu.VMEM((R, W_HG), F32),
        pltpu.VMEM((R, W_HG), chunk_dt),
        pltpu.VMEM((R, W_HG), chunk_dt),
        pltpu.VMEM((R, W_HG), chunk_dt),
        pltpu.VMEM((HG_HEADS, ROW_BLOCK, HG_DK), F32),
        pltpu.VMEM((ROW_BLOCK, ROW_BLOCK), jnp.int32),
        pltpu.VMEM((ROW_BLOCK, ROW_BLOCK), BF16),
        pltpu.VMEM((NB, SUBLANES, W_HG), F32),
        pltpu.VMEM((R, D_MODEL), BF16),
    ]
    return pl.pallas_call(
        functools.partial(_mixer_kernel, NB=NB, C=C, t_valid=t_valid, has_state=has_state),
        grid=grid,
        in_specs=in_specs,
        out_specs=[x_spec, h_spec, h_spec, s_spec],
        out_shape=out_shape,
        scratch_shapes=scratch,
        compiler_params=pltpu.CompilerParams(
            dimension_semantics=("arbitrary", "arbitrary"),
            vmem_limit_bytes=V7X_VMEM_LIMIT),
        name=name,
    )(*args)


def kernel(x_prompt, x_sample, state_s5_re, state_s5_im, state_hgrn, norm_g, w_in, s5_lambda_re, s5_lambda_im,
           s5_log_step, s5_b_re, s5_b_im, s5_c_re, s5_c_im, s5_d, w_glu, b_glu, hgrn_lb_logits, hgrn_onorm_g,
           w_out, final_norm_g):
    assert norm_g.shape[0] == 1, "single-layer model"
    l = 0
    a_re, a_im, bbre, bbim, lb = _prepare_params(
        s5_lambda_re[l], s5_lambda_im[l], s5_log_step[l], s5_b_re[l], s5_b_im[l], hgrn_lb_logits)
    bcat, ccat = _block_diag_weights(bbre, bbim, s5_c_re[l], s5_c_im[l])
    weights = (
        norm_g[l].astype(F32).reshape(1, D_MODEL),
        w_in[l].astype(BF16),
        a_re, a_im, bcat, ccat,
        s5_d[l].astype(F32).reshape(1, W_S5),
        w_glu[l].astype(BF16),
        b_glu[l].astype(F32).reshape(1, W_S5),
        lb,
        hgrn_onorm_g[l].astype(F32).reshape(1, W_HG),
        w_out[l].astype(BF16),
        final_norm_g.astype(F32).reshape(1, D_MODEL),
    )

    Bp, Lp, _ = x_prompt.shape
    yp, p_re, p_im, p_hg = _mixer_call(x_prompt, None, weights, NB=Bp, C=64, t_valid=64, name="mixer_prompt")

    Bs, Ls, _ = x_sample.shape
    state = (state_s5_re[l].reshape(Bs, S5_LANES).astype(F32),
             state_s5_im[l].reshape(Bs, S5_LANES).astype(F32),
             state_hgrn[l].astype(F32))
    ys, s_re, s_im, s_hg = _mixer_call(x_sample, state, weights, NB=ROW_BLOCK // Ls, C=Ls, t_valid=Ls,
                                       name="mixer_sample")
    ys = ys.reshape(Bs, Ls, D_MODEL)

    sd = state_s5_re.dtype
    shp = (1, Bp, S5_GROUPS, S5_STATE)
    shs = (1, Bs, S5_GROUPS, S5_STATE)
    return (yp, ys,
            p_re.reshape(shp).astype(sd), p_im.reshape(shp).astype(sd), p_hg[None].astype(state_hgrn.dtype),
            s_re.reshape(shs).astype(sd), s_im.reshape(shs).astype(sd), s_hg[None].astype(state_hgrn.dtype))
```

```python
import functools
import math

import jax
import jax.numpy as jnp
import numpy as np
from jax import lax
from jax.experimental import pallas as pl
from jax.experimental.pallas import tpu as pltpu

F32 = jnp.float32
BF16 = jnp.bfloat16

D_MODEL = 1024
W_S5 = 512
W_HG = 512
S5_CH = 16
S5_GROUPS = 32
S5_STATE = 64
S5_LANES = S5_GROUPS * S5_STATE
S5_HALF = S5_LANES // 2
HG_DK = 128
HG_HEADS = 4
HG_DV = 128
EPS = 1e-6
LAMBDA_RE_MAX = -1e-4
PROJ_OUT = 3072
OFF_Q = 1024
HG_COLS = PROJ_OUT - OFF_Q
LOG2E = 1.4426950408889634

LANES = 128
SUBLANES = 8
BF16_TILE_ROWS = 16
ROW_BLOCK = 128
MATMUL_ROWS = 512
READOUT_ROWS = 256
S5_INPUT_ROWS = 256
V7X_VMEM_LIMIT = 60 * 1024 * 1024


def _prep_kernel(lre_ref, lim_ref, ls_ref, brt_ref, bit_ref, lg_ref,
                 are_ref, aim_ref, bbre_ref, bbim_ref, lb_ref):
    lr = jnp.minimum(lre_ref[...], LAMBDA_RE_MAX)
    li = lim_ref[...]
    dt = jnp.exp(ls_ref[...])
    er = jnp.exp(lr * dt)
    a_re = er * jnp.cos(li * dt)
    a_im = er * jnp.sin(li * dt)
    den = lr * lr + li * li
    zr = ((a_re - 1.0) * lr + a_im * li) / den
    zi = (a_im * lr - (a_re - 1.0) * li) / den
    are_ref[...] = a_re
    aim_ref[...] = a_im
    for g in range(S5_GROUPS):
        br = brt_ref[g]
        bi = bit_ref[g]
        zr_g = zr[g:g + 1, :]
        zi_g = zi[g:g + 1, :]
        bbre_ref[g] = zr_g * br - zi_g * bi
        bbim_ref[g] = zr_g * bi + zi_g * br
    lg = lg_ref[...]
    e = jnp.exp(lg - jnp.max(lg, axis=0, keepdims=True))
    lb_ref[...] = e[0:1, :] / jnp.sum(e, axis=0, keepdims=True)


def _prepare_params(lam_re, lam_im, log_step, b_re, b_im, lb_logits):
    gps = jax.ShapeDtypeStruct((S5_GROUPS, S5_STATE), F32)
    gcs = jax.ShapeDtypeStruct((S5_GROUPS, S5_CH, S5_STATE), F32)
    a_re, a_im, bbre, bbim, lb = pl.pallas_call(
        _prep_kernel,
        out_shape=[gps, gps, gcs, gcs, jax.ShapeDtypeStruct((1, HG_HEADS * HG_DK), F32)],
        name="s5_hgrn_prep",
    )(lam_re.astype(F32), lam_im.astype(F32), log_step.astype(F32).reshape(S5_GROUPS, 1),
      jnp.transpose(b_re.astype(F32), (0, 2, 1)), jnp.transpose(b_im.astype(F32), (0, 2, 1)),
      lb_logits.astype(F32))
    return a_re.reshape(1, S5_LANES), a_im.reshape(1, S5_LANES), bbre, bbim, lb


def _block_diag_weights(bbre, bbim, c_re, c_im):
    hg = S5_GROUPS // 2
    eye = jnp.eye(hg, dtype=BF16)[None, :, None, :, None]

    def in_half(bbt):
        x = bbt.astype(BF16).reshape(2, hg, S5_CH, 1, S5_STATE) * eye
        return x.reshape(2, hg * S5_CH, hg * S5_STATE)

    def out_half(c):
        ct = jnp.transpose(c.astype(BF16), (0, 2, 1)).reshape(2, hg, S5_STATE, 1, S5_CH)
        return (ct * eye).reshape(2, hg * S5_STATE, hg * S5_CH)

    bcat = jnp.concatenate([in_half(bbre), in_half(bbim)], axis=2)
    ccat = jnp.concatenate([out_half(c_re), out_half(-c_im)], axis=1)
    return bcat, ccat


def _sigmoid(x):
    return 1.0 / (1.0 + jnp.exp2(x * (-LOG2E)))


def _gelu_tanh(x):
    return 0.5 * x * (1.0 + jnp.tanh(math.sqrt(2.0 / math.pi) * (x + 0.044715 * (x * x * x))))


def _dot(a, b):
    return jnp.dot(a, b, preferred_element_type=F32)


def _dot_nt(a, b):
    return lax.dot_general(a, b, (((1,), (1,)), ((), ())), preferred_element_type=F32)


def _dot_tn(a, b):
    return lax.dot_general(a, b, (((0,), (0,)), ((), ())), preferred_element_type=F32)


def _row_bcast_blocks(src_ref, blk, off):
    parts = []
    for n in range(ROW_BLOCK // blk):
        row = src_ref[n * blk + off:n * blk + off + 1, :]
        parts.append(jnp.broadcast_to(row, (blk, HG_DK)))
    return parts[0] if len(parts) == 1 else jnp.concatenate(parts, axis=0)


def _mixer_kernel(*refs, NB, C, t_valid, has_state):
    if has_state:
        x_ref, h0r_ref, h0i_ref, s0_ref = refs[:4]
        refs = refs[4:]
    else:
        x_ref = refs[0]
        refs = refs[1:]
    (perm_ref, permt_ref, ng_ref, win_ref, are_ref, aim_ref, bcat_ref, ccat_ref, d_ref, wglu_ref, bglu_ref,
     lb_ref, og_ref, wout_ref, fg_ref,
     y_ref, hr_ref, hi_ref, s_ref,
     hn_ref, proj_ref, uz_ref, bu_ref, s5_ref, o_ref, qe_ref, ke_ref, v_ref, bsall_ref, lev_ref, tri_ref,
     dl_ref, cat_ref) = refs

    j = pl.program_id(1)
    nj = pl.num_programs(1)
    R = NB * C
    n_rb = R // ROW_BLOCK
    nbat = ROW_BLOCK // C

    @pl.when(j == 0)
    def _init():
        if has_state:
            hr_ref[...] = h0r_ref[...]
            hi_ref[...] = h0i_ref[...]
        else:
            hr_ref[...] = jnp.zeros(hr_ref.shape, F32)
            hi_ref[...] = jnp.zeros(hi_ref.shape, F32)
            s_ref[...] = jnp.zeros(s_ref.shape, F32)

    MM = min(R, MATMUL_ROWS)
    n_mb = R // MM
    mbat = MM // C

    def rows_of(rb, n=MM):
        return pl.ds(pl.multiple_of(rb * n, n), n)

    flat_rows = len(x_ref.shape) == 2

    def x_rows(mb):
        if flat_rows:
            return x_ref[rows_of(mb), :]
        return x_ref[pl.ds(mb * mbat, mbat)].reshape(MM, D_MODEL)

    def phase1a(rb, c):
        rows = rows_of(rb)
        x = x_rows(rb)
        hn = x * lax.rsqrt(jnp.mean(x * x, axis=-1, keepdims=True) + EPS) * ng_ref[...]
        hn = hn.astype(BF16)
        hn_ref[rows, :] = hn
        for sec in range(HG_COLS // 512):
            cs = slice(512 * sec, 512 * (sec + 1))
            proj_ref[rows, cs] = _dot(hn, win_ref[:, OFF_Q + 512 * sec:OFF_Q + 512 * (sec + 1)])
        return c
    lax.fori_loop(0, n_mb, phase1a, 0)

    row = lax.broadcasted_iota(jnp.int32, (ROW_BLOCK, ROW_BLOCK), 0)
    col = lax.broadcasted_iota(jnp.int32, (ROW_BLOCK, ROW_BLOCK), 1)
    tin = row & (C - 1)
    levels = []
    m = 1
    while m < C:
        levels.append(m)
        m *= 2
    @pl.when(j == 0)
    def _masks():
        lev = jnp.where(row == col, 0, -1)
        for li, m in enumerate(levels):
            same = (row & ~(2 * m - 1)) == (col & ~(2 * m - 1))
            own = same & ((row & m) != 0) & ((col & m) == 0)
            lev = jnp.where(own, li + 1, lev)
        lev_ref[...] = lev
        in_chunk = (row & ~(C - 1)) == (col & ~(C - 1))
        tri_ref[...] = jnp.where(in_chunk & (col <= row), 1.0, 0.0).astype(BF16)

    heads = range(HG_HEADS)
    hcs = [slice(HG_DK * hd, HG_DK * (hd + 1)) for hd in heads]
    scan_width = min(S5_HALF, (SUBLANES * S5_HALF) // NB)
    assert t_valid == C or n_rb == 1
    block_steps = min(ROW_BLOCK // NB, t_valid)

    def hgrn_block(rb, mxu_fill):
        rows = rows_of(rb, ROW_BLOCK)
        lv = lev_ref[...]
        qs, ks, fs, vs, gcats = [], [], [], [], []
        for hd in heads:
            lb = lb_ref[:, hcs[hd]]
            q = proj_ref[rows, HG_DK * hd:HG_DK * (hd + 1)]
            gf = proj_ref[rows, 512 + HG_DK * hd:512 + HG_DK * (hd + 1)]
            v = proj_ref[rows, 1024 + HG_DV * hd:1024 + HG_DV * (hd + 1)].astype(BF16)
            f = lb + (1.0 - lb) * _sigmoid(gf)
            g = jnp.log2(f)
            k = 1.0 - f
            if t_valid < C:
                live = tin < t_valid
                f = jnp.where(live, f, 1.0)
                g = jnp.where(live, g, 0.0)
                k = jnp.where(live, k, 0.0)
            g1 = g.astype(BF16)
            r1 = g - g1.astype(F32)
            g2 = r1.astype(BF16)
            g3 = (r1 - g2.astype(F32)).astype(BF16)
            p4 = row & 3
            dec1 = jnp.where((row & 1) == 1, f, 1.0)
            dec2 = jnp.where(p4 == 0, pltpu.roll(f, ROW_BLOCK - 1, axis=0),
                             jnp.where(p4 == 1, 1.0,
                                       jnp.where(p4 == 2, f, f * pltpu.roll(f, 1, axis=0))))
            qs.append(q)
            ks.append(k)
            fs.append((dec1.astype(BF16), dec2.astype(BF16)))
            vs.append(v)
            gcats.append(jnp.concatenate([g1, g2, g3], axis=1))
        bparts = [_dot(tri_ref[...], gcats[hd]) for hd in heads]
        bs = []
        for hd in heads:
            bp = bparts[hd]
            b = (bp[:, 0:HG_DK] + bp[:, HG_DK:2 * HG_DK]) + bp[:, 2 * HG_DK:3 * HG_DK]
            bsall_ref[hd] = b
            bs.append(b)
        qbs = [q.astype(BF16) for q in qs]
        kbs = [k.astype(BF16) for k in ks]
        atts =[jnp.where(lv == 0, _dot_nt(qbs[hd], kbs[hd]), 0.0) for hd in heads]
        for li, m in enumerate(levels):
            two_m = 2 * m
            p2 = row & (two_m - 1)
            if m > 2:
                sgn = jnp.where(p2 >= m, 1.0, -1.0)
            for hd in heads:
                if m <= 2:
                    decb = fs[hd][li]
                else:
                    ref = _row_bcast_blocks(bsall_ref.at[hd], two_m, m - 1)
                    decb = jnp.exp2((bs[hd] - ref) * sgn).astype(BF16)
                atts[hd] = jnp.where(lv == li + 1, _dot_nt(qbs[hd] * decb, kbs[hd] * decb), atts[hd])
            if li == 0:
                for piece in mxu_fill:
                    piece()
        for hd in heads:
            o_ref[rows, hcs[hd]] = _dot(atts[hd].astype(BF16), vs[hd])
        for hd in heads:
            hc = hcs[hd]
            bs_ref = bsall_ref.at[hd]
            b = bs[hd]
            if C >= SUBLANES:
                b_last = _row_bcast_blocks(bs_ref, C, C - 1)
            else:
                b_last = b
                for d in range(1, C):
                    b_last = jnp.where(tin == C - 1 - d, pltpu.roll(b, ROW_BLOCK - d, axis=0), b_last)
            qe_ref[rows, hc] = (qs[hd] * jnp.exp2(b)).astype(qe_ref.dtype)
            ke_ref[rows, hc] = (ks[hd] * jnp.exp2(b_last - b)).astype(ke_ref.dtype)
            v_ref[rows, hc] = vs[hd].astype(v_ref.dtype)
            for n in range(nbat):
                last = bs_ref[n * C + C - 1:n * C + C, :]
                dl_ref[rb * nbat + n, :, hc] = jnp.broadcast_to(jnp.exp2(last), (SUBLANES, HG_DK))
    SR = min(R, S5_INPUT_ROWS)
    sblocks = SR // ROW_BLOCK

    def merged(it, c):
        rows = rows_of(it, SR)
        hp = _dot(perm_ref[rows, :], hn_ref[...]).astype(BF16)
        uz = {}

        def uz_piece(sec):
            uz[sec] = _dot(hp, win_ref[:, W_S5 * sec:W_S5 * (sec + 1)])
            uz_ref[rows, W_S5 * sec:W_S5 * (sec + 1)] = uz[sec]

        def bu_piece(i):
            u_i = uz[0][:, 256 * i:256 * (i + 1)].astype(BF16)
            bu_ref[rows, 2 * S5_HALF * i:2 * S5_HALF * (i + 1)] = _dot(u_i, bcat_ref[i])

        pieces = [functools.partial(uz_piece, 0), functools.partial(uz_piece, 1),
                  functools.partial(bu_piece, 0), functools.partial(bu_piece, 1)]
        per_block = len(pieces) // sblocks
        for sb in range(sblocks):
            hgrn_block(it * sblocks + sb, pieces[sb * per_block:(sb + 1) * per_block])
        for i in range(2):
            for off in range(0, S5_HALF, scan_width):
                st = slice(S5_HALF * i + off, S5_HALF * i + off + scan_width)
                re_c = 2 * S5_HALF * i + off
                im_c = re_c + S5_HALF
                ar = are_ref[:, st]
                ai = aim_ref[:, st]
                hr = hr_ref[:, st]
                hi = hi_ref[:, st]
                for tt in range(block_steps * sblocks):
                    trows = pl.ds(pl.multiple_of(it * SR + tt * NB, NB), NB)
                    hr, hi = (ar * hr - ai * hi + bu_ref[trows, re_c:re_c + scan_width],
                              ar * hi + ai * hr + bu_ref[trows, im_c:im_c + scan_width])
                    bu_ref[trows, re_c:re_c + scan_width] = hr
                    bu_ref[trows, im_c:im_c + scan_width] = hi
                hr_ref[:, st] = hr
                hi_ref[:, st] = hi
        return c
    lax.fori_loop(0, R // SR, merged, 0)

    def inter_out(bi):
        srows = pl.ds(pl.multiple_of(bi * C, C), C)
        for hd in heads:
            hc = hcs[hd]
            if has_state:
                oi = _dot(qe_ref[srows, hc], s0_ref[bi, hd].astype(qe_ref.dtype))
            else:
                oi = _dot_nt(qe_ref[srows, hc], s_ref[bi, hd].astype(qe_ref.dtype))
            o_ref[srows, hc] = o_ref[srows, hc] + oi

    def inter_state(bi):
        srows = pl.ds(pl.multiple_of(bi * C, C), C)
        for hd in heads:
            hc = hcs[hd]
            if has_state:
                upd = _dot_tn(ke_ref[srows, hc], v_ref[srows, hc])
                s_ref[bi, hd] = s0_ref[bi, hd] * dl_ref[bi, :, hc].T[:, 0:1] + upd
            else:
                upd = _dot_tn(v_ref[srows, hc], ke_ref[srows, hc])
                s_ref[bi, hd] = s_ref[bi, hd] * dl_ref[bi, 0:1, hc] + upd

    RR = min(R, READOUT_ROWS)
    rbat = RR // C
    merge_inter = rbat <= 4

    def readout_inter(rb, c):
        rows = rows_of(rb, RR)
        ys = []
        for i in range(2):
            hs = bu_ref[rows, 2 * S5_HALF * i:2 * S5_HALF * (i + 1)].astype(BF16)
            ys.append(_dot(hs, ccat_ref[i]))
        if merge_inter:
            for n in range(rbat):
                inter_out(rb * rbat + n)
        u = uz_ref[rows, 0:W_S5]
        y = jnp.concatenate(ys, axis=1) + d_ref[...] * u
        g = _gelu_tanh(y)
        glu = _dot(g.astype(BF16), wglu_ref[...]) + bglu_ref[...]
        if merge_inter:
            for n in range(rbat):
                inter_state(rb * rbat + n)
        zs = uz_ref[rows, W_S5:2 * W_S5]
        s5 = g * _sigmoid(glu) * (zs * _sigmoid(zs))
        s5_ref[rows, :] = s5.astype(BF16)
        return c
    lax.fori_loop(0, R // RR, readout_inter, 0)

    if not merge_inter and C >= SUBLANES:
        def inter(bi, c):
            inter_out(bi)
            inter_state(bi)
            return c
        lax.fori_loop(0, NB, inter, 0, unroll=2)

    if C < SUBLANES:
        assert has_state and not merge_inter
        per_tile = SUBLANES // C
        rid = lax.broadcasted_iota(jnp.int32, (SUBLANES, HG_DK), 0)

        def inter_tile(ti, c):
            srows = pl.ds(pl.multiple_of(ti * SUBLANES, SUBLANES), SUBLANES)
            for hd in heads:
                hc = hcs[hd]
                qe8 = qe_ref[srows, hc]
                ke8 = ke_ref[srows, hc]
                v8 = v_ref[srows, hc]
                acc = o_ref[srows, hc]
                for s in range(per_tile):
                    bi = ti * per_tile + s
                    mine = (rid >= s * C) & (rid < (s + 1) * C)
                    st = s0_ref[bi, hd]
                    acc = acc + _dot(jnp.where(mine, qe8, 0.0), st)
                    upd = _dot_tn(jnp.where(mine, ke8, 0.0), v8)
                    s_ref[bi, hd] = st * dl_ref[bi, :, hc].T[:, 0:1] + upd
                o_ref[srows, hc] = acc
            return c
        lax.fori_loop(0, NB // per_tile, inter_tile, 0, unroll=2)

    def phase5(rb, c):
        rows = rows_of(rb)
        cat_ref[rows, 0:W_S5] = _dot(permt_ref[rows, :], s5_ref[...]).astype(BF16)
        for hd in range(HG_HEADS):
            hc = slice(HG_DV * hd, HG_DV * (hd + 1))
            o = o_ref[rows, hc]
            o = o * lax.rsqrt(jnp.mean(o * o, axis=-1, keepdims=True) + EPS)
            zh = proj_ref[rows, 1536 + HG_DV * hd:1536 + HG_DV * (hd + 1)]
            hg = o * og_ref[:, hc] * (zh * _sigmoid(zh))
            cat_ref[rows, W_S5 + HG_DV * hd:W_S5 + HG_DV * (hd + 1)] = hg.astype(BF16)
        out = _dot(cat_ref[rows, :], wout_ref[...])
        xn = x_rows(rb) + out
        yv = xn * lax.rsqrt(jnp.mean(xn * xn, axis=-1, keepdims=True) + EPS) * fg_ref[...]
        if flat_rows:
            y_ref[rows, :] = yv
        else:
            y_ref[pl.ds(rb * mbat, mbat)] = yv.reshape(mbat, C, D_MODEL)
        return c
    lax.fori_loop(0, n_mb, phase5, 0)

    if not has_state:
        @pl.when(j == nj - 1)
        def _fin():
            def tr(b, c):
                for hd in range(HG_HEADS):
                    s_ref[b, hd] = s_ref[b, hd].T
                return c
            lax.fori_loop(0, NB, tr, 0)


def _time_major_perm(NB, C):
    R = NB * C
    dst = np.arange(R)
    src = (dst % NB) * C + dst // NB
    p = np.zeros((R, R), np.float32)
    p[dst, src] = 1.0
    return p


def _mixer_call(x, state, weights, *, NB, C, t_valid, name):
    B, L, _ = x.shape
    has_state = state is not None
    assert not has_state or L == C, "a given state is consumed by a single chunk per sequence"
    grid = (B // NB, L // C)
    R = NB * C
    perm = _time_major_perm(NB, C)
    weights = (jnp.asarray(perm, BF16), jnp.asarray(perm.T, BF16)) + tuple(weights)

    def full(a):
        nd = a.ndim
        return pl.BlockSpec(a.shape, lambda i, j, nd=nd: (0,) * nd, pipeline_mode=pl.Buffered(1))

    flat_rows = C % SUBLANES != 0
    if flat_rows:
        assert L == C
        x = x.reshape(B * L, D_MODEL)
        x_spec = pl.BlockSpec((R, D_MODEL), lambda i, j: (i, 0))
    else:
        x_spec = pl.BlockSpec((NB, C, D_MODEL), lambda i, j: (i, j, 0))
    h_spec = pl.BlockSpec((NB, S5_LANES), lambda i, j: (i, 0))
    s_spec = pl.BlockSpec((NB, HG_HEADS, HG_DK, HG_DV), lambda i, j: (i, 0, 0, 0))
    in_specs = [x_spec] + ([h_spec, h_spec, s_spec] if has_state else []) + [full(w) for w in weights]
    args = [x] + (list(state) if has_state else []) + list(weights)
    out_shape = [
        jax.ShapeDtypeStruct(x.shape, x.dtype),
        jax.ShapeDtypeStruct((B, S5_LANES), F32),
        jax.ShapeDtypeStruct((B, S5_LANES), F32),
        jax.ShapeDtypeStruct((B, HG_HEADS, HG_DK, HG_DV), F32),
    ]
    chunk_dt = BF16 if C % BF16_TILE_ROWS == 0 else F32
    scratch = [
        pltpu.VMEM((R, D_MODEL), BF16),
        pltpu.VMEM((R, HG_COLS), F32),
        pltpu.VMEM((R, 2 * W_S5), F32),
        pltpu.VMEM((R, 2 * S5_LANES), F32),
        pltpu.VMEM((R, W_S5), BF16),
        pltpu.VMEM((R, W_HG), F32),
        pltpu.VMEM((R, W_HG), chunk_dt),
        pltpu.VMEM((R, W_HG), chunk_dt),
        pltpu.VMEM((R, W_HG), chunk_dt),
        pltpu.VMEM((HG_HEADS, ROW_BLOCK, HG_DK), F32),
        pltpu.VMEM((ROW_BLOCK, ROW_BLOCK), jnp.int32),
        pltpu.VMEM((ROW_BLOCK, ROW_BLOCK), BF16),
        pltpu.VMEM((NB, SUBLANES, W_HG), F32),
        pltpu.VMEM((R, D_MODEL), BF16),
    ]
    return pl.pallas_call(
        functools.partial(_mixer_kernel, NB=NB, C=C, t_valid=t_valid, has_state=has_state),
        grid=grid,
        in_specs=in_specs,
        out_specs=[x_spec, h_spec, h_spec, s_spec],
        out_shape=out_shape,
        scratch_shapes=scratch,
        compiler_params=pltpu.CompilerParams(
            dimension_semantics=("arbitrary", "arbitrary"),
            vmem_limit_bytes=V7X_VMEM_LIMIT),
        name=name,
    )(*args)


def kernel(x_prompt, x_sample, state_s5_re, state_s5_im, state_hgrn, norm_g, w_in, s5_lambda_re, s5_lambda_im,
           s5_log_step, s5_b_re, s5_b_im, s5_c_re, s5_c_im, s5_d, w_glu, b_glu, hgrn_lb_logits, hgrn_onorm_g,
           w_out, final_norm_g):
    assert norm_g.shape[0] == 1, "single-layer model"
    l = 0
    a_re, a_im, bbre, bbim, lb = _prepare_params(
        s5_lambda_re[l], s5_lambda_im[l], s5_log_step[l], s5_b_re[l], s5_b_im[l], hgrn_lb_logits)
    bcat, ccat = _block_diag_weights(bbre, bbim, s5_c_re[l], s5_c_im[l])
    weights = (
        norm_g[l].astype(F32).reshape(1, D_MODEL),
        w_in[l].astype(BF16),
        a_re, a_im, bcat, ccat,
        s5_d[l].astype(F32).reshape(1, W_S5),
        w_glu[l].astype(BF16),
        b_glu[l].astype(F32).reshape(1, W_S5),
        lb,
        hgrn_onorm_g[l].astype(F32).reshape(1, W_HG),
        w_out[l].astype(BF16),
        final_norm_g.astype(F32).reshape(1, D_MODEL),
    )

    Bp, Lp, _ = x_prompt.shape
    yp, p_re, p_im, p_hg = _mixer_call(x_prompt, None, weights, NB=Bp, C=64, t_valid=64, name="mixer_prompt")

    Bs, Ls, _ = x_sample.shape
    state = (state_s5_re[l].reshape(Bs, S5_LANES).astype(F32),
             state_s5_im[l].reshape(Bs, S5_LANES).astype(F32),
             state_hgrn[l].astype(F32))
    ys, s_re, s_im, s_hg = _mixer_call(x_sample, state, weights, NB=ROW_BLOCK // Ls, C=Ls, t_valid=Ls,
                                       name="mixer_sample")
    ys = ys.reshape(Bs, Ls, D_MODEL)

    sd = state_s5_re.dtype
    shp = (1, Bp, S5_GROUPS, S5_STATE)
    shs = (1, Bs, S5_GROUPS, S5_STATE)
    return (yp, ys,
            p_re.reshape(shp).astype(sd), p_im.reshape(shp).astype(sd), p_hg[None].astype(state_hgrn.dtype),
            s_re.reshape(shs).astype(sd), s_im.reshape(shs).astype(sd), s_hg[None].astype(state_hgrn.dtype))
```

```python
import functools
import math

import jax
import jax.numpy as jnp
import numpy as np
from jax import lax
from jax.experimental import pallas as pl
from jax.experimental.pallas import tpu as pltpu

F32 = jnp.float32
BF16 = jnp.bfloat16

D_MODEL = 1024
W_S5 = 512
W_HG = 512
S5_CH = 16
S5_GROUPS = 32
S5_STATE = 64
S5_LANES = S5_GROUPS * S5_STATE
S5_HALF = S5_LANES // 2
HG_DK = 128
HG_HEADS = 4
HG_DV = 128
EPS = 1e-6
LAMBDA_RE_MAX = -1e-4
PROJ_OUT = 3072
OFF_Q = 1024
HG_COLS = PROJ_OUT - OFF_Q
LOG2E = 1.4426950408889634

LANES = 128
SUBLANES = 8
BF16_TILE_ROWS = 16
ROW_BLOCK = 128
MATMUL_ROWS = 512
READOUT_ROWS = 512
S5_INPUT_ROWS = 512
V7X_VMEM_LIMIT = 60 * 1024 * 1024


def _prep_kernel(lre_ref, lim_ref, ls_ref, brt_ref, bit_ref, lg_ref,
                 are_ref, aim_ref, bbre_ref, bbim_ref, lb_ref):
    lr = jnp.minimum(lre_ref[...], LAMBDA_RE_MAX)
    li = lim_ref[...]
    dt = jnp.exp(ls_ref[...])
    er = jnp.exp(lr * dt)
    a_re = er * jnp.cos(li * dt)
    a_im = er * jnp.sin(li * dt)
    den = lr * lr + li * li
    zr = ((a_re - 1.0) * lr + a_im * li) / den
    zi = (a_im * lr - (a_re - 1.0) * li) / den
    are_ref[...] = a_re
    aim_ref[...] = a_im
    for g in range(S5_GROUPS):
        br = brt_ref[g]
        bi = bit_ref[g]
        zr_g = zr[g:g + 1, :]
        zi_g = zi[g:g + 1, :]
        bbre_ref[g] = zr_g * br - zi_g * bi
        bbim_ref[g] = zr_g * bi + zi_g * br
    lg = lg_ref[...]
    e = jnp.exp(lg - jnp.max(lg, axis=0, keepdims=True))
    lb_ref[...] = e[0:1, :] / jnp.sum(e, axis=0, keepdims=True)


def _prepare_params(lam_re, lam_im, log_step, b_re, b_im, lb_logits):
    gps = jax.ShapeDtypeStruct((S5_GROUPS, S5_STATE), F32)
    gcs = jax.ShapeDtypeStruct((S5_GROUPS, S5_CH, S5_STATE), F32)
    a_re, a_im, bbre, bbim, lb = pl.pallas_call(
        _prep_kernel,
        out_shape=[gps, gps, gcs, gcs, jax.ShapeDtypeStruct((1, HG_HEADS * HG_DK), F32)],
        name="s5_hgrn_prep",
    )(lam_re.astype(F32), lam_im.astype(F32), log_step.astype(F32).reshape(S5_GROUPS, 1),
      jnp.transpose(b_re.astype(F32), (0, 2, 1)), jnp.transpose(b_im.astype(F32), (0, 2, 1)),
      lb_logits.astype(F32))
    return a_re.reshape(1, S5_LANES), a_im.reshape(1, S5_LANES), bbre, bbim, lb


def _block_diag_weights(bbre, bbim, c_re, c_im):
    hg = S5_GROUPS // 2
    eye = jnp.eye(hg, dtype=BF16)[None, :, None, :, None]

    def in_half(bbt):
        x = bbt.astype(BF16).reshape(2, hg, S5_CH, 1, S5_STATE) * eye
        return x.reshape(2, hg * S5_CH, hg * S5_STATE)

    def out_half(c):
        ct = jnp.transpose(c.astype(BF16), (0, 2, 1)).reshape(2, hg, S5_STATE, 1, S5_CH)
        return (ct * eye).reshape(2, hg * S5_STATE, hg * S5_CH)

    bcat = jnp.concatenate([in_half(bbre), in_half(bbim)], axis=2)
    ccat = jnp.concatenate([out_half(c_re), out_half(-c_im)], axis=1)
    return bcat, ccat


def _sigmoid(x):
    return 1.0 / (1.0 + jnp.exp2(x * (-LOG2E)))


def _gelu_tanh(x):
    return 0.5 * x * (1.0 + jnp.tanh(math.sqrt(2.0 / math.pi) * (x + 0.044715 * (x * x * x))))


def _dot(a, b):
    return jnp.dot(a, b, preferred_element_type=F32)


def _dot_nt(a, b):
    return lax.dot_general(a, b, (((1,), (1,)), ((), ())), preferred_element_type=F32)


def _dot_tn(a, b):
    return lax.dot_general(a, b, (((0,), (0,)), ((), ())), preferred_element_type=F32)


def _row_bcast_blocks(src_ref, blk, off):
    parts = []
    for n in range(ROW_BLOCK // blk):
        row = src_ref[n * blk + off:n * blk + off + 1, :]
        parts.append(jnp.broadcast_to(row, (blk, HG_DK)))
    return parts[0] if len(parts) == 1 else jnp.concatenate(parts, axis=0)


def _mixer_kernel(*refs, NB, C, t_valid, has_state):
    if has_state:
        x_ref, h0r_ref, h0i_ref, s0_ref = refs[:4]
        refs = refs[4:]
    else:
        x_ref = refs[0]
        refs = refs[1:]
    (perm_ref, permt_ref, ng_ref, win_ref, are_ref, aim_ref, bcat_ref, ccat_ref, d_ref, wglu_ref, bglu_ref,
     lb_ref, og_ref, wout_ref, fg_ref,
     y_ref, hr_ref, hi_ref, s_ref,
     hn_ref, proj_ref, uz_ref, bu_ref, s5_ref, o_ref, qe_ref, ke_ref, v_ref, bsall_ref, lev_ref, tri_ref,
     dl_ref, cat_ref) = refs

    j = pl.program_id(1)
    nj = pl.num_programs(1)
    R = NB * C
    n_rb = R // ROW_BLOCK
    nbat = ROW_BLOCK // C

    @pl.when(j == 0)
    def _init():
        if has_state:
            hr_ref[...] = h0r_ref[...]
            hi_ref[...] = h0i_ref[...]
        else:
            hr_ref[...] = jnp.zeros(hr_ref.shape, F32)
            hi_ref[...] = jnp.zeros(hi_ref.shape, F32)
            s_ref[...] = jnp.zeros(s_ref.shape, F32)

    MM = min(R, MATMUL_ROWS)
    n_mb = R // MM
    mbat = MM // C

    def rows_of(rb, n=MM):
        return pl.ds(pl.multiple_of(rb * n, n), n)

    flat_rows = len(x_ref.shape) == 2

    def x_rows(mb):
        if flat_rows:
            return x_ref[rows_of(mb), :]
        return x_ref[pl.ds(mb * mbat, mbat)].reshape(MM, D_MODEL)

    def phase1a(rb, c):
        rows = rows_of(rb)
        x = x_rows(rb)
        hn = x * lax.rsqrt(jnp.mean(x * x, axis=-1, keepdims=True) + EPS) * ng_ref[...]
        hn = hn.astype(BF16)
        hn_ref[rows, :] = hn
        for sec in range(HG_COLS // 512):
            cs = slice(512 * sec, 512 * (sec + 1))
            proj_ref[rows, cs] = _dot(hn, win_ref[:, OFF_Q + 512 * sec:OFF_Q + 512 * (sec + 1)])
        return c
    lax.fori_loop(0, n_mb, phase1a, 0)

    row = lax.broadcasted_iota(jnp.int32, (ROW_BLOCK, ROW_BLOCK), 0)
    col = lax.broadcasted_iota(jnp.int32, (ROW_BLOCK, ROW_BLOCK), 1)
    tin = row & (C - 1)
    levels = []
    m = 1
    while m < C:
        levels.append(m)
        m *= 2
    @pl.when(j == 0)
    def _masks():
        lev = jnp.where(row == col, 0, -1)
        for li, m in enumerate(levels):
            same = (row & ~(2 * m - 1)) == (col & ~(2 * m - 1))
            own = same & ((row & m) != 0) & ((col & m) == 0)
            lev = jnp.where(own, li + 1, lev)
        lev_ref[...] = lev
        in_chunk = (row & ~(C - 1)) == (col & ~(C - 1))
        tri_ref[...] = jnp.where(in_chunk & (col <= row), 1.0, 0.0).astype(BF16)

    heads = range(HG_HEADS)
    hcs = [slice(HG_DK * hd, HG_DK * (hd + 1)) for hd in heads]
    scan_width = min(S5_HALF, (SUBLANES * S5_HALF) // NB)
    assert t_valid == C or n_rb == 1
    block_steps = min(ROW_BLOCK // NB, t_valid)

    def hgrn_block(rb, mxu_fill):
        rows = rows_of(rb, ROW_BLOCK)
        lv = lev_ref[...]
        qs, ks, fs, vs, gcats = [], [], [], [], []
        for hd in heads:
            lb = lb_ref[:, hcs[hd]]
            q = proj_ref[rows, HG_DK * hd:HG_DK * (hd + 1)]
            gf = proj_ref[rows, 512 + HG_DK * hd:512 + HG_DK * (hd + 1)]
            v = proj_ref[rows, 1024 + HG_DV * hd:1024 + HG_DV * (hd + 1)].astype(BF16)
            f = lb + (1.0 - lb) * _sigmoid(gf)
            g = jnp.log2(f)
            k = 1.0 - f
            if t_valid < C:
                live = tin < t_valid
                f = jnp.where(live, f, 1.0)
                g = jnp.where(live, g, 0.0)
                k = jnp.where(live, k, 0.0)
            g1 = g.astype(BF16)
            r1 = g - g1.astype(F32)
            g2 = r1.astype(BF16)
            g3 = (r1 - g2.astype(F32)).astype(BF16)
            p4 = row & 3
            dec1 = jnp.where((row & 1) == 1, f, 1.0)
            dec2 = jnp.where(p4 == 0, pltpu.roll(f, ROW_BLOCK - 1, axis=0),
                             jnp.where(p4 == 1, 1.0,
                                       jnp.where(p4 == 2, f, f * pltpu.roll(f, 1, axis=0))))
            qs.append(q)
            ks.append(k)
            fs.append((dec1.astype(BF16), dec2.astype(BF16)))
            vs.append(v)
            gcats.append(jnp.concatenate([g1, g2, g3], axis=1))
        bparts = [_dot(tri_ref[...], gcats[hd]) for hd in heads]
        bs = []
        for hd in heads:
            bp = bparts[hd]
            b = (bp[:, 0:HG_DK] + bp[:, HG_DK:2 * HG_DK]) + bp[:, 2 * HG_DK:3 * HG_DK]
            bsall_ref[hd] = b
            bs.append(b)
        qbs = [q.astype(BF16) for q in qs]
        kbs = [k.astype(BF16) for k in ks]
        atts =[jnp.where(lv == 0, _dot_nt(qbs[hd], kbs[hd]), 0.0) for hd in heads]
        for li, m in enumerate(levels):
            two_m = 2 * m
            p2 = row & (two_m - 1)
            if m > 2:
                sgn = jnp.where(p2 >= m, 1.0, -1.0)
            for hd in heads:
                if m <= 2:
                    decb = fs[hd][li]
                else:
                    ref = _row_bcast_blocks(bsall_ref.at[hd], two_m, m - 1)
                    decb = jnp.exp2((bs[hd] - ref) * sgn).astype(BF16)
                atts[hd] = jnp.where(lv == li + 1, _dot_nt(qbs[hd] * decb, kbs[hd] * decb), atts[hd])
            if li == 0:
                for piece in mxu_fill:
                    piece()
        for hd in heads:
            o_ref[rows, hcs[hd]] = _dot(atts[hd].astype(BF16), vs[hd])
        for hd in heads:
            hc = hcs[hd]
            bs_ref = bsall_ref.at[hd]
            b = bs[hd]
            if C >= SUBLANES:
                b_last = _row_bcast_blocks(bs_ref, C, C - 1)
            else:
                b_last = b
                for d in range(1, C):
                    b_last = jnp.where(tin == C - 1 - d, pltpu.roll(b, ROW_BLOCK - d, axis=0), b_last)
            qe_ref[rows, hc] = (qs[hd] * jnp.exp2(b)).astype(qe_ref.dtype)
            ke_ref[rows, hc] = (ks[hd] * jnp.exp2(b_last - b)).astype(ke_ref.dtype)
            v_ref[rows, hc] = vs[hd].astype(v_ref.dtype)
            for n in range(nbat):
                last = bs_ref[n * C + C - 1:n * C + C, :]
                dl_ref[rb * nbat + n, :, hc] = jnp.broadcast_to(jnp.exp2(last), (SUBLANES, HG_DK))
    SR = min(R, S5_INPUT_ROWS)
    sblocks = SR // ROW_BLOCK

    def merged(it, c):
        rows = rows_of(it, SR)
        hp = _dot(perm_ref[rows, :], hn_ref[...]).astype(BF16)
        uz = {}

        def uz_piece(sec):
            uz[sec] = _dot(hp, win_ref[:, W_S5 * sec:W_S5 * (sec + 1)])
            uz_ref[rows, W_S5 * sec:W_S5 * (sec + 1)] = uz[sec]

        def bu_piece(i):
            u_i = uz[0][:, 256 * i:256 * (i + 1)].astype(BF16)
            bu_ref[rows, 2 * S5_HALF * i:2 * S5_HALF * (i + 1)] = _dot(u_i, bcat_ref[i])

        pieces = [functools.partial(uz_piece, 0), functools.partial(uz_piece, 1),
                  functools.partial(bu_piece, 0), functools.partial(bu_piece, 1)]
        per_block = len(pieces) // sblocks
        for sb in range(sblocks):
            hgrn_block(it * sblocks + sb, pieces[sb * per_block:(sb + 1) * per_block])
        for i in range(2):
            for off in range(0, S5_HALF, scan_width):
                st = slice(S5_HALF * i + off, S5_HALF * i + off + scan_width)
                re_c = 2 * S5_HALF * i + off
                im_c = re_c + S5_HALF
                ar = are_ref[:, st]
                ai = aim_ref[:, st]
                hr = hr_ref[:, st]
                hi = hi_ref[:, st]
                for tt in range(block_steps * sblocks):
                    trows = pl.ds(pl.multiple_of(it * SR + tt * NB, NB), NB)
                    hr, hi = (ar * hr - ai * hi + bu_ref[trows, re_c:re_c + scan_width],
                              ar * hi + ai * hr + bu_ref[trows, im_c:im_c + scan_width])
                    bu_ref[trows, re_c:re_c + scan_width] = hr
                    bu_ref[trows, im_c:im_c + scan_width] = hi
                hr_ref[:, st] = hr
                hi_ref[:, st] = hi
        return c
    lax.fori_loop(0, R // SR, merged, 0)

    def inter_out(bi):
        srows = pl.ds(pl.multiple_of(bi * C, C), C)
        for hd in heads:
            hc = hcs[hd]
            if has_state:
                oi = _dot(qe_ref[srows, hc], s0_ref[bi, hd].astype(qe_ref.dtype))
            else:
                oi = _dot_nt(qe_ref[srows, hc], s_ref[bi, hd].astype(qe_ref.dtype))
            o_ref[srows, hc] = o_ref[srows, hc] + oi

    def inter_state(bi):
        srows = pl.ds(pl.multiple_of(bi * C, C), C)
        for hd in heads:
            hc = hcs[hd]
            if has_state:
                upd = _dot_tn(ke_ref[srows, hc], v_ref[srows, hc])
                s_ref[bi, hd] = s0_ref[bi, hd] * dl_ref[bi, :, hc].T[:, 0:1] + upd
            else:
                upd = _dot_tn(v_ref[srows, hc], ke_ref[srows, hc])
                s_ref[bi, hd] = s_ref[bi, hd] * dl_ref[bi, 0:1, hc] + upd

    RR = min(R, READOUT_ROWS)
    rbat = RR // C
    merge_inter = rbat <= 8

    def readout_inter(rb, c):
        rows = rows_of(rb, RR)
        ys = []
        for i in range(2):
            hs = bu_ref[rows, 2 * S5_HALF * i:2 * S5_HALF * (i + 1)].astype(BF16)
            ys.append(_dot(hs, ccat_ref[i]))
        if merge_inter:
            for n in range(rbat):
                inter_out(rb * rbat + n)
        u = uz_ref[rows, 0:W_S5]
        y = jnp.concatenate(ys, axis=1) + d_ref[...] * u
        g = _gelu_tanh(y)
        glu = _dot(g.astype(BF16), wglu_ref[...]) + bglu_ref[...]
        if merge_inter:
            for n in range(rbat):
                inter_state(rb * rbat + n)
        zs = uz_ref[rows, W_S5:2 * W_S5]
        s5 = g * _sigmoid(glu) * (zs * _sigmoid(zs))
        s5_ref[rows, :] = s5.astype(BF16)
        return c
    lax.fori_loop(0, R // RR, readout_inter, 0)

    if not merge_inter and C >= SUBLANES:
        def inter(bi, c):
            inter_out(bi)
            inter_state(bi)
            return c
        lax.fori_loop(0, NB, inter, 0, unroll=2)

    if C < SUBLANES:
        assert has_state and not merge_inter
        per_tile = SUBLANES // C
        rid = lax.broadcasted_iota(jnp.int32, (SUBLANES, HG_DK), 0)

        def inter_tile(ti, c):
            srows = pl.ds(pl.multiple_of(ti * SUBLANES, SUBLANES), SUBLANES)
            for hd in heads:
                hc = hcs[hd]
                qe8 = qe_ref[srows, hc]
                ke8 = ke_ref[srows, hc]
                v8 = v_ref[srows, hc]
                acc = o_ref[srows, hc]
                for s in range(per_tile):
                    bi = ti * per_tile + s
                    mine = (rid >= s * C) & (rid < (s + 1) * C)
                    st = s0_ref[bi, hd]
                    acc = acc + _dot(jnp.where(mine, qe8, 0.0), st)
                    upd = _dot_tn(jnp.where(mine, ke8, 0.0), v8)
                    s_ref[bi, hd] = st * dl_ref[bi, :, hc].T[:, 0:1] + upd
                o_ref[srows, hc] = acc
            return c
        lax.fori_loop(0, NB // per_tile, inter_tile, 0, unroll=2)

    def phase5(rb, c):
        rows = rows_of(rb)
        cat_ref[rows, 0:W_S5] = _dot(permt_ref[rows, :], s5_ref[...]).astype(BF16)
        for hd in range(HG_HEADS):
            hc = slice(HG_DV * hd, HG_DV * (hd + 1))
            o = o_ref[rows, hc]
            o = o * lax.rsqrt(jnp.mean(o * o, axis=-1, keepdims=True) + EPS)
            zh = proj_ref[rows, 1536 + HG_DV * hd:1536 + HG_DV * (hd + 1)]
            hg = o * og_ref[:, hc] * (zh * _sigmoid(zh))
            cat_ref[rows, W_S5 + HG_DV * hd:W_S5 + HG_DV * (hd + 1)] = hg.astype(BF16)
        out = _dot(cat_ref[rows, :], wout_ref[...])
        xn = x_rows(rb) + out
        yv = xn * lax.rsqrt(jnp.mean(xn * xn, axis=-1, keepdims=True) + EPS) * fg_ref[...]
        if flat_rows:
            y_ref[rows, :] = yv
        else:
            y_ref[pl.ds(rb * mbat, mbat)] = yv.reshape(mbat, C, D_MODEL)
        return c
    lax.fori_loop(0, n_mb, phase5, 0)

    if not has_state:
        @pl.when(j == nj - 1)
        def _fin():
            def tr(b, c):
                for hd in range(HG_HEADS):
                    s_ref[b, hd] = s_ref[b, hd].T
                return c
            lax.fori_loop(0, NB, tr, 0)


def _time_major_perm(NB, C):
    R = NB * C
    dst = np.arange(R)
    src = (dst % NB) * C + dst // NB
    p = np.zeros((R, R), np.float32)
    p[dst, src] = 1.0
    return p


def _mixer_call(x, state, weights, *, NB, C, t_valid, name):
    B, L, _ = x.shape
    has_state = state is not None
    assert not has_state or L == C, "a given state is consumed by a single chunk per sequence"
    grid = (B // NB, L // C)
    R = NB * C
    perm = _time_major_perm(NB, C)
    weights = (jnp.asarray(perm, BF16), jnp.asarray(perm.T, BF16)) + tuple(weights)

    def full(a):
        nd = a.ndim
        return pl.BlockSpec(a.shape, lambda i, j, nd=nd: (0,) * nd, pipeline_mode=pl.Buffered(1))

    flat_rows = C % SUBLANES != 0
    if flat_rows:
        assert L == C
        x = x.reshape(B * L, D_MODEL)
        x_spec = pl.BlockSpec((R, D_MODEL), lambda i, j: (i, 0))
    else:
        x_spec = pl.BlockSpec((NB, C, D_MODEL), lambda i, j: (i, j, 0))
    h_spec = pl.BlockSpec((NB, S5_LANES), lambda i, j: (i, 0))
    s_spec = pl.BlockSpec((NB, HG_HEADS, HG_DK, HG_DV), lambda i, j: (i, 0, 0, 0))
    in_specs = [x_spec] + ([h_spec, h_spec, s_spec] if has_state else []) + [full(w) for w in weights]
    args = [x] + (list(state) if has_state else []) + list(weights)
    out_shape = [
        jax.ShapeDtypeStruct(x.shape, x.dtype),
        jax.ShapeDtypeStruct((B, S5_LANES), F32),
        jax.ShapeDtypeStruct((B, S5_LANES), F32),
        jax.ShapeDtypeStruct((B, HG_HEADS, HG_DK, HG_DV), F32),
    ]
    chunk_dt = BF16 if C % BF16_TILE_ROWS == 0 else F32
    scratch = [
        pltpu.VMEM((R, D_MODEL), BF16),
        pltpu.VMEM((R, HG_COLS), F32),
        pltpu.VMEM((R, 2 * W_S5), F32),
        pltpu.VMEM((R, 2 * S5_LANES), F32),
        pltpu.VMEM((R, W_S5), BF16),
        pltpu.VMEM((R, W_HG), F32),
        pltpu.VMEM((R, W_HG), chunk_dt),
        pltpu.VMEM((R, W_HG), chunk_dt),
        pltpu.VMEM((R, W_HG), chunk_dt),
        pltpu.VMEM((HG_HEADS, ROW_BLOCK, HG_DK), F32),
        pltpu.VMEM((ROW_BLOCK, ROW_BLOCK), jnp.int32),
        pltpu.VMEM((ROW_BLOCK, ROW_BLOCK), BF16),
        pltpu.VMEM((NB, SUBLANES, W_HG), F32),
        pltpu.VMEM((R, D_MODEL), BF16),
    ]
    return pl.pallas_call(
        functools.partial(_mixer_kernel, NB=NB, C=C, t_valid=t_valid, has_state=has_state),
        grid=grid,
        in_specs=in_specs,
        out_specs=[x_spec, h_spec, h_spec, s_spec],
        out_shape=out_shape,
        scratch_shapes=scratch,
        compiler_params=pltpu.CompilerParams(
            dimension_semantics=("arbitrary", "arbitrary"),
            vmem_limit_bytes=V7X_VMEM_LIMIT),
        name=name,
    )(*args)


def kernel(x_prompt, x_sample, state_s5_re, state_s5_im, state_hgrn, norm_g, w_in, s5_lambda_re, s5_lambda_im,
           s5_log_step, s5_b_re, s5_b_im, s5_c_re, s5_c_im, s5_d, w_glu, b_glu, hgrn_lb_logits, hgrn_onorm_g,
           w_out, final_norm_g):
    assert norm_g.shape[0] == 1, "single-layer model"
    l = 0
    a_re, a_im, bbre, bbim, lb = _prepare_params(
        s5_lambda_re[l], s5_lambda_im[l], s5_log_step[l], s5_b_re[l], s5_b_im[l], hgrn_lb_logits)
    bcat, ccat = _block_diag_weights(bbre, bbim, s5_c_re[l], s5_c_im[l])
    weights = (
        norm_g[l].astype(F32).reshape(1, D_MODEL),
        w_in[l].astype(BF16),
        a_re, a_im, bcat, ccat,
        s5_d[l].astype(F32).reshape(1, W_S5),
        w_glu[l].astype(BF16),
        b_glu[l].astype(F32).reshape(1, W_S5),
        lb,
        hgrn_onorm_g[l].astype(F32).reshape(1, W_HG),
        w_out[l].astype(BF16),
        final_norm_g.astype(F32).reshape(1, D_MODEL),
    )

    Bp, Lp, _ = x_prompt.shape
    yp, p_re, p_im, p_hg = _mixer_call(x_prompt, None, weights, NB=Bp, C=64, t_valid=64, name="mixer_prompt")

    Bs, Ls, _ = x_sample.shape
    state = (state_s5_re[l].reshape(Bs, S5_LANES).astype(F32),
             state_s5_im[l].reshape(Bs, S5_LANES).astype(F32),
             state_hgrn[l].astype(F32))
    ys, s_re, s_im, s_hg = _mixer_call(x_sample, state, weights, NB=ROW_BLOCK // Ls, C=Ls, t_valid=Ls,
                                       name="mixer_sample")
    ys = ys.reshape(Bs, Ls, D_MODEL)

    sd = state_s5_re.dtype
    shp = (1, Bp, S5_GROUPS, S5_STATE)
    shs = (1, Bs, S5_GROUPS, S5_STATE)
    return (yp, ys,
            p_re.reshape(shp).astype(sd), p_im.reshape(shp).astype(sd), p_hg[None].astype(state_hgrn.dtype),
            s_re.reshape(shs).astype(sd), s_im.reshape(shs).astype(sd), s_hg[None].astype(state_hgrn.dtype))
```

```python
import functools
import math

import jax
import jax.numpy as jnp
import numpy as np
from jax import lax
from jax.experimental import pallas as pl
from jax.experimental.pallas import tpu as pltpu

F32 = jnp.float32
BF16 = jnp.bfloat16

D_MODEL = 1024
W_S5 = 512
W_HG = 512
S5_CH = 16
S5_GROUPS = 32
S5_STATE = 64
S5_LANES = S5_GROUPS * S5_STATE
S5_HALF = S5_LANES // 2
HG_DK = 128
HG_HEADS = 4
HG_DV = 128
EPS = 1e-6
LAMBDA_RE_MAX = -1e-4
PROJ_OUT = 3072
OFF_Q = 1024
HG_COLS = PROJ_OUT - OFF_Q
LOG2E = 1.4426950408889634

LANES = 128
SUBLANES = 8
BF16_TILE_ROWS = 16
ROW_BLOCK = 128
MATMUL_ROWS = 512
READOUT_ROWS = 512
S5_INPUT_ROWS = 512
V7X_VMEM_LIMIT = 60 * 1024 * 1024


def _prep_kernel(lre_ref, lim_ref, ls_ref, brt_ref, bit_ref, lg_ref,
                 are_ref, aim_ref, bbre_ref, bbim_ref, lb_ref):
    lr = jnp.minimum(lre_ref[...], LAMBDA_RE_MAX)
    li = lim_ref[...]
    dt = jnp.exp(ls_ref[...])
    er = jnp.exp(lr * dt)
    a_re = er * jnp.cos(li * dt)
    a_im = er * jnp.sin(li * dt)
    den = lr * lr + li * li
    zr = ((a_re - 1.0) * lr + a_im * li) / den
    zi = (a_im * lr - (a_re - 1.0) * li) / den
    are_ref[...] = a_re
    aim_ref[...] = a_im
    for g in range(S5_GROUPS):
        br = brt_ref[g]
        bi = bit_ref[g]
        zr_g = zr[g:g + 1, :]
        zi_g = zi[g:g + 1, :]
        bbre_ref[g] = zr_g * br - zi_g * bi
        bbim_ref[g] = zr_g * bi + zi_g * br
    lg = lg_ref[...]
    e = jnp.exp(lg - jnp.max(lg, axis=0, keepdims=True))
    lb_ref[...] = e[0:1, :] / jnp.sum(e, axis=0, keepdims=True)


def _prepare_params(lam_re, lam_im, log_step, b_re, b_im, lb_logits):
    gps = jax.ShapeDtypeStruct((S5_GROUPS, S5_STATE), F32)
    gcs = jax.ShapeDtypeStruct((S5_GROUPS, S5_CH, S5_STATE), F32)
    a_re, a_im, bbre, bbim, lb = pl.pallas_call(
        _prep_kernel,
        out_shape=[gps, gps, gcs, gcs, jax.ShapeDtypeStruct((1, HG_HEADS * HG_DK), F32)],
        name="s5_hgrn_prep",
    )(lam_re.astype(F32), lam_im.astype(F32), log_step.astype(F32).reshape(S5_GROUPS, 1),
      jnp.transpose(b_re.astype(F32), (0, 2, 1)), jnp.transpose(b_im.astype(F32), (0, 2, 1)),
      lb_logits.astype(F32))
    return a_re.reshape(1, S5_LANES), a_im.reshape(1, S5_LANES), bbre, bbim, lb


def _block_diag_weights(bbre, bbim, c_re, c_im):
    hg = S5_GROUPS // 2
    eye = jnp.eye(hg, dtype=BF16)[None, :, None, :, None]

    def in_half(bbt):
        x = bbt.astype(BF16).reshape(2, hg, S5_CH, 1, S5_STATE) * eye
        return x.reshape(2, hg * S5_CH, hg * S5_STATE)

    def out_half(c):
        ct = jnp.transpose(c.astype(BF16), (0, 2, 1)).reshape(2, hg, S5_STATE, 1, S5_CH)
        return (ct * eye).reshape(2, hg * S5_STATE, hg * S5_CH)

    bcat = jnp.concatenate([in_half(bbre), in_half(bbim)], axis=2)
    ccat = jnp.concatenate([out_half(c_re), out_half(-c_im)], axis=1)
    return bcat, ccat


def _sigmoid(x):
    return 1.0 / (1.0 + jnp.exp2(x * (-LOG2E)))


def _gelu_tanh(x):
    return 0.5 * x * (1.0 + jnp.tanh(math.sqrt(2.0 / math.pi) * (x + 0.044715 * (x * x * x))))


def _dot(a, b):
    return jnp.dot(a, b, preferred_element_type=F32)


def _dot_nt(a, b):
    return lax.dot_general(a, b, (((1,), (1,)), ((), ())), preferred_element_type=F32)


def _dot_tn(a, b):
    return lax.dot_general(a, b, (((0,), (0,)), ((), ())), preferred_element_type=F32)


def _row_bcast_blocks(src_ref, blk, off):
    parts = []
    for n in range(ROW_BLOCK // blk):
        row = src_ref[n * blk + off:n * blk + off + 1, :]
        parts.append(jnp.broadcast_to(row, (blk, HG_DK)))
    return parts[0] if len(parts) == 1 else jnp.concatenate(parts, axis=0)


def _mixer_kernel(*refs, NB, C, t_valid, has_state):
    if has_state:
        x_ref, h0r_ref, h0i_ref, s0_ref = refs[:4]
        refs = refs[4:]
    else:
        x_ref, xn_ref = refs[:2]
        refs = refs[2:]
    (perm_ref, permt_ref, ng_ref, win_ref, are_ref, aim_ref, bcat_ref, ccat_ref, d_ref, wglu_ref, bglu_ref,
     lb_ref, og_ref, wout_ref, fg_ref,
     y_ref, hr_ref, hi_ref, s_ref,
     hna_ref, proja_ref, hnb_ref, projb_ref, uz_ref, bu_ref, s5_ref, o_ref, qe_ref, ke_ref, v_ref, bsall_ref,
     lev_ref, tri_ref, dl_ref, cat_ref) = refs

    j = pl.program_id(1)
    nj = pl.num_programs(1)
    lookahead = not has_state
    R = NB * C
    n_rb = R // ROW_BLOCK
    nbat = ROW_BLOCK // C

    @pl.when(j == 0)
    def _init():
        if has_state:
            hr_ref[...] = h0r_ref[...]
            hi_ref[...] = h0i_ref[...]
        else:
            hr_ref[...] = jnp.zeros(hr_ref.shape, F32)
            hi_ref[...] = jnp.zeros(hi_ref.shape, F32)
            s_ref[...] = jnp.zeros(s_ref.shape, F32)

    MM = min(R, MATMUL_ROWS)
    n_mb = R // MM
    mbat = MM // C

    def rows_of(rb, n=MM):
        return pl.ds(pl.multiple_of(rb * n, n), n)

    flat_rows = len(x_ref.shape) == 2

    def x_rows(mb):
        if flat_rows:
            return x_ref[rows_of(mb), :]
        return x_ref[pl.ds(mb * mbat, mbat)].reshape(MM, D_MODEL)

    def prenorm(x):
        return (x * lax.rsqrt(jnp.mean(x * x, axis=-1, keepdims=True) + EPS) * ng_ref[...]).astype(BF16)

    def hg_proj(hn, sec):
        return _dot(hn, win_ref[:, OFF_Q + 512 * sec:OFF_Q + 512 * (sec + 1)])

    def project_own_chunk(hn_ref, proj_ref):
        def body(rb, c):
            rows = rows_of(rb)
            hn = prenorm(x_rows(rb))
            hn_ref[rows, :] = hn
            for sec in range(HG_COLS // 512):
                proj_ref[rows, 512 * sec:512 * (sec + 1)] = hg_proj(hn, sec)
            return c
        lax.fori_loop(0, n_mb, body, 0)

    row = lax.broadcasted_iota(jnp.int32, (ROW_BLOCK, ROW_BLOCK), 0)
    col = lax.broadcasted_iota(jnp.int32, (ROW_BLOCK, ROW_BLOCK), 1)
    tin = row & (C - 1)
    levels = []
    m = 1
    while m < C:
        levels.append(m)
        m *= 2

    @pl.when(j == 0)
    def _masks():
        lev = jnp.where(row == col, 0, -1)
        for li, m in enumerate(levels):
            same = (row & ~(2 * m - 1)) == (col & ~(2 * m - 1))
            own = same & ((row & m) != 0) & ((col & m) == 0)
            lev = jnp.where(own, li + 1, lev)
        lev_ref[...] = lev
        in_chunk = (row & ~(C - 1)) == (col & ~(C - 1))
        tri_ref[...] = jnp.where(in_chunk & (col <= row), 1.0, 0.0).astype(BF16)

    def step_body(hn_ref, proj_ref, hn_nxt_ref, proj_nxt_ref):
        heads = range(HG_HEADS)
        hcs = [slice(HG_DK * hd, HG_DK * (hd + 1)) for hd in heads]
        scan_width = min(S5_HALF, (SUBLANES * S5_HALF) // NB)
        assert t_valid == C or n_rb == 1
        block_steps = min(ROW_BLOCK // NB, t_valid)

        def hgrn_block(rb, mxu_fill, dense_fill):
            rows = rows_of(rb, ROW_BLOCK)
            lv = lev_ref[...]
            qs, ks, fs, vs, gcats = [], [], [], [], []
            for hd in heads:
                lb = lb_ref[:, hcs[hd]]
                q = proj_ref[rows, HG_DK * hd:HG_DK * (hd + 1)]
                gf = proj_ref[rows, 512 + HG_DK * hd:512 + HG_DK * (hd + 1)]
                v = proj_ref[rows, 1024 + HG_DV * hd:1024 + HG_DV * (hd + 1)].astype(BF16)
                f = lb + (1.0 - lb) * _sigmoid(gf)
                g = jnp.log2(f)
                k = 1.0 - f
                if t_valid < C:
                    live = tin < t_valid
                    f = jnp.where(live, f, 1.0)
                    g = jnp.where(live, g, 0.0)
                    k = jnp.where(live, k, 0.0)
                g1 = g.astype(BF16)
                r1 = g - g1.astype(F32)
                g2 = r1.astype(BF16)
                g3 = (r1 - g2.astype(F32)).astype(BF16)
                p4 = row & 3
                dec1 = jnp.where((row & 1) == 1, f, 1.0)
                dec2 = jnp.where(p4 == 0, pltpu.roll(f, ROW_BLOCK - 1, axis=0),
                                 jnp.where(p4 == 1, 1.0,
                                           jnp.where(p4 == 2, f, f * pltpu.roll(f, 1, axis=0))))
                qs.append(q)
                ks.append(k)
                fs.append((dec1.astype(BF16), dec2.astype(BF16)))
                vs.append(v)
                gcats.append(jnp.concatenate([g1, g2, g3], axis=1))
            bparts = [_dot(tri_ref[...], gcats[hd]) for hd in heads]
            bs = []
            for hd in heads:
                bp = bparts[hd]
                b = (bp[:, 0:HG_DK] + bp[:, HG_DK:2 * HG_DK]) + bp[:, 2 * HG_DK:3 * HG_DK]
                bsall_ref[hd] = b
                bs.append(b)
            qbs = [q.astype(BF16) for q in qs]
            kbs = [k.astype(BF16) for k in ks]
            atts = [jnp.where(lv == 0, _dot_nt(qbs[hd], kbs[hd]), 0.0) for hd in heads]
            for li, m in enumerate(levels):
                two_m = 2 * m
                p2 = row & (two_m - 1)
                if m > 2:
                    sgn = jnp.where(p2 >= m, 1.0, -1.0)
                for hd in heads:
                    if m <= 2:
                        decb = fs[hd][li]
                    else:
                        ref = _row_bcast_blocks(bsall_ref.at[hd], two_m, m - 1)
                        decb = jnp.exp2((bs[hd] - ref) * sgn).astype(BF16)
                    atts[hd] = jnp.where(lv == li + 1, _dot_nt(qbs[hd] * decb, kbs[hd] * decb), atts[hd])
                if li == 0:
                    for piece in mxu_fill:
                        piece()
                if li == min(1, len(levels) - 1):
                    for piece in dense_fill:
                        piece()
            for hd in heads:
                o_ref[rows, hcs[hd]] = _dot(atts[hd].astype(BF16), vs[hd])
            for hd in heads:
                hc = hcs[hd]
                bs_ref = bsall_ref.at[hd]
                b = bs[hd]
                if C >= SUBLANES:
                    b_last = _row_bcast_blocks(bs_ref, C, C - 1)
                else:
                    b_last = b
                    for d in range(1, C):
                        b_last = jnp.where(tin == C - 1 - d, pltpu.roll(b, ROW_BLOCK - d, axis=0), b_last)
                qe_ref[rows, hc] = (qs[hd] * jnp.exp2(b)).astype(qe_ref.dtype)
                ke_ref[rows, hc] = (ks[hd] * jnp.exp2(b_last - b)).astype(ke_ref.dtype)
                v_ref[rows, hc] = vs[hd].astype(v_ref.dtype)
                for n in range(nbat):
                    last = bs_ref[n * C + C - 1:n * C + C, :]
                    dl_ref[rb * nbat + n, :, hc] = jnp.broadcast_to(jnp.exp2(last), (SUBLANES, HG_DK))

        SR = min(R, S5_INPUT_ROWS)
        sblocks = SR // ROW_BLOCK

        def merged(it, c):
            rows = rows_of(it, SR)
            hp = _dot(perm_ref[rows, :], hn_ref[...]).astype(BF16)
            uz = {}

            def uz_piece(sec):
                uz[sec] = _dot(hp, win_ref[:, W_S5 * sec:W_S5 * (sec + 1)])
                uz_ref[rows, W_S5 * sec:W_S5 * (sec + 1)] = uz[sec]

            def bu_piece(i):
                u_i = uz[0][:, 256 * i:256 * (i + 1)].astype(BF16)
                bu_ref[rows, 2 * S5_HALF * i:2 * S5_HALF * (i + 1)] = _dot(u_i, bcat_ref[i])

            pieces = [functools.partial(uz_piece, 0), functools.partial(uz_piece, 1),
                      functools.partial(bu_piece, 0), functools.partial(bu_piece, 1)]
            per_block = len(pieces) // sblocks
            dense = []
            if hn_nxt_ref is not None:
                assert SR == R
                hn_nxt_ref[...] = prenorm(xn_ref[...].reshape(R, D_MODEL))

                def dense_piece(sec):
                    proj_nxt_ref[:, 512 * sec:512 * (sec + 1)] = hg_proj(hn_nxt_ref[...], sec)
                dense = [functools.partial(dense_piece, sec) for sec in range(HG_COLS // 512)]
            dense_per_block = -(-len(dense) // sblocks)
            for sb in range(sblocks):
                hgrn_block(it * sblocks + sb, pieces[sb * per_block:(sb + 1) * per_block],
                           dense[sb * dense_per_block:(sb + 1) * dense_per_block])
            for i in range(2):
                for off in range(0, S5_HALF, scan_width):
                    st = slice(S5_HALF * i + off, S5_HALF * i + off + scan_width)
                    re_c = 2 * S5_HALF * i + off
                    im_c = re_c + S5_HALF
                    ar = are_ref[:, st]
                    ai = aim_ref[:, st]
                    hr = hr_ref[:, st]
                    hi = hi_ref[:, st]
                    for tt in range(block_steps * sblocks):
                        trows = pl.ds(pl.multiple_of(it * SR + tt * NB, NB), NB)
                        hr, hi = (ar * hr - ai * hi + bu_ref[trows, re_c:re_c + scan_width],
                                  ar * hi + ai * hr + bu_ref[trows, im_c:im_c + scan_width])
                        bu_ref[trows, re_c:re_c + scan_width] = hr
                        bu_ref[trows, im_c:im_c + scan_width] = hi
                    hr_ref[:, st] = hr
                    hi_ref[:, st] = hi
            return c
        lax.fori_loop(0, R // SR, merged, 0)

        def inter_out(bi):
            srows = pl.ds(pl.multiple_of(bi * C, C), C)
            for hd in heads:
                hc = hcs[hd]
                if has_state:
                    oi = _dot(qe_ref[srows, hc], s0_ref[bi, hd].astype(qe_ref.dtype))
                else:
                    oi = _dot_nt(qe_ref[srows, hc], s_ref[bi, hd].astype(qe_ref.dtype))
                o_ref[srows, hc] = o_ref[srows, hc] + oi

        def inter_state(bi):
            srows = pl.ds(pl.multiple_of(bi * C, C), C)
            for hd in heads:
                hc = hcs[hd]
                if has_state:
                    upd = _dot_tn(ke_ref[srows, hc], v_ref[srows, hc])
                    s_ref[bi, hd] = s0_ref[bi, hd] * dl_ref[bi, :, hc].T[:, 0:1] + upd
                else:
                    upd = _dot_tn(v_ref[srows, hc], ke_ref[srows, hc])
                    s_ref[bi, hd] = s_ref[bi, hd] * dl_ref[bi, 0:1, hc] + upd

        RR = min(R, READOUT_ROWS)
        rbat = RR // C
        merge_inter = rbat <= 8

        def readout_inter(rb, c):
            rows = rows_of(rb, RR)
            ys = []
            for i in range(2):
                hs = bu_ref[rows, 2 * S5_HALF * i:2 * S5_HALF * (i + 1)].astype(BF16)
                ys.append(_dot(hs, ccat_ref[i]))
            if merge_inter:
                for n in range(rbat):
                    inter_out(rb * rbat + n)
            u = uz_ref[rows, 0:W_S5]
            y = jnp.concatenate(ys, axis=1) + d_ref[...] * u
            g = _gelu_tanh(y)
            glu = _dot(g.astype(BF16), wglu_ref[...]) + bglu_ref[...]
            if merge_inter:
                for n in range(rbat):
                    inter_state(rb * rbat + n)
            zs = uz_ref[rows, W_S5:2 * W_S5]
            s5 = g * _sigmoid(glu) * (zs * _sigmoid(zs))
            s5_ref[rows, :] = s5.astype(BF16)
            return c
        lax.fori_loop(0, R // RR, readout_inter, 0)

        if not merge_inter and C >= SUBLANES:
            def inter(bi, c):
                inter_out(bi)
                inter_state(bi)
                return c
            lax.fori_loop(0, NB, inter, 0, unroll=2)

        if C < SUBLANES:
            assert has_state and not merge_inter
            per_tile = SUBLANES // C
            rid = lax.broadcasted_iota(jnp.int32, (SUBLANES, HG_DK), 0)

            def inter_tile(ti, c):
                srows = pl.ds(pl.multiple_of(ti * SUBLANES, SUBLANES), SUBLANES)
                for hd in heads:
                    hc = hcs[hd]
                    qe8 = qe_ref[srows, hc]
                    ke8 = ke_ref[srows, hc]
                    v8 = v_ref[srows, hc]
                    acc = o_ref[srows, hc]
                    for s in range(per_tile):
                        bi = ti * per_tile + s
                        mine = (rid >= s * C) & (rid < (s + 1) * C)
                        st = s0_ref[bi, hd]
                        acc = acc + _dot(jnp.where(mine, qe8, 0.0), st)
                        upd = _dot_tn(jnp.where(mine, ke8, 0.0), v8)
                        s_ref[bi, hd] = st * dl_ref[bi, :, hc].T[:, 0:1] + upd
                    o_ref[srows, hc] = acc
                return c
            lax.fori_loop(0, NB // per_tile, inter_tile, 0, unroll=2)

        def phase5(rb, c):
            rows = rows_of(rb)
            cat_ref[rows, 0:W_S5] = _dot(permt_ref[rows, :], s5_ref[...]).astype(BF16)
            for hd in range(HG_HEADS):
                hc = slice(HG_DV * hd, HG_DV * (hd + 1))
                o = o_ref[rows, hc]
                o = o * lax.rsqrt(jnp.mean(o * o, axis=-1, keepdims=True) + EPS)
                zh = proj_ref[rows, 1536 + HG_DV * hd:1536 + HG_DV * (hd + 1)]
                hg = o * og_ref[:, hc] * (zh * _sigmoid(zh))
                cat_ref[rows, W_S5 + HG_DV * hd:W_S5 + HG_DV * (hd + 1)] = hg.astype(BF16)
            out = _dot(cat_ref[rows, :], wout_ref[...])
            xn = x_rows(rb) + out
            yv = xn * lax.rsqrt(jnp.mean(xn * xn, axis=-1, keepdims=True) + EPS) * fg_ref[...]
            if flat_rows:
                y_ref[rows, :] = yv
            else:
                y_ref[pl.ds(rb * mbat, mbat)] = yv.reshape(mbat, C, D_MODEL)
            return c
        lax.fori_loop(0, n_mb, phase5, 0)

    if lookahead:
        pl.when(j == 0)(functools.partial(project_own_chunk, hna_ref, proja_ref))
        pl.when(j % 2 == 0)(functools.partial(step_body, hna_ref, proja_ref, hnb_ref, projb_ref))
        pl.when(j % 2 == 1)(functools.partial(step_body, hnb_ref, projb_ref, hna_ref, proja_ref))
    else:
        project_own_chunk(hna_ref, proja_ref)
        step_body(hna_ref, proja_ref, None, None)

    if not has_state:
        @pl.when(j == nj - 1)
        def _fin():
            def tr(b, c):
                for hd in range(HG_HEADS):
                    s_ref[b, hd] = s_ref[b, hd].T
                return c
            lax.fori_loop(0, NB, tr, 0)


def _time_major_perm(NB, C):
    R = NB * C
    dst = np.arange(R)
    src = (dst % NB) * C + dst // NB
    p = np.zeros((R, R), np.float32)
    p[dst, src] = 1.0
    return p


def _mixer_call(x, state, weights, *, NB, C, t_valid, name):
    B, L, _ = x.shape
    has_state = state is not None
    assert not has_state or L == C, "a given state is consumed by a single chunk per sequence"
    grid = (B // NB, L // C)
    R = NB * C
    perm = _time_major_perm(NB, C)
    weights = (jnp.asarray(perm, BF16), jnp.asarray(perm.T, BF16)) + tuple(weights)

    def full(a):
        nd = a.ndim
        return pl.BlockSpec(a.shape, lambda i, j, nd=nd: (0,) * nd, pipeline_mode=pl.Buffered(1))

    flat_rows = C % SUBLANES != 0
    if flat_rows:
        assert L == C
        x = x.reshape(B * L, D_MODEL)
        x_spec = pl.BlockSpec((R, D_MODEL), lambda i, j: (i, 0))
    else:
        x_spec = pl.BlockSpec((NB, C, D_MODEL), lambda i, j: (i, j, 0))
    h_spec = pl.BlockSpec((NB, S5_LANES), lambda i, j: (i, 0))
    s_spec = pl.BlockSpec((NB, HG_HEADS, HG_DK, HG_DV), lambda i, j: (i, 0, 0, 0))
    if has_state:
        in_specs, args = [x_spec, h_spec, h_spec, s_spec], [x] + list(state)
        ahead_rows = SUBLANES
    else:
        n_chunks = L // C
        nxt_spec = pl.BlockSpec((NB, C, D_MODEL), lambda i, j: (i, jnp.minimum(j + 1, n_chunks - 1), 0))
        in_specs, args = [x_spec, nxt_spec], [x, x]
        ahead_rows = R
    in_specs += [full(w) for w in weights]
    args += list(weights)
    out_shape = [
        jax.ShapeDtypeStruct(x.shape, x.dtype),
        jax.ShapeDtypeStruct((B, S5_LANES), F32),
        jax.ShapeDtypeStruct((B, S5_LANES), F32),
        jax.ShapeDtypeStruct((B, HG_HEADS, HG_DK, HG_DV), F32),
    ]
    chunk_dt = BF16 if C % BF16_TILE_ROWS == 0 else F32
    scratch = [
        pltpu.VMEM((R, D_MODEL), BF16),
        pltpu.VMEM((R, HG_COLS), F32),
        pltpu.VMEM((ahead_rows, D_MODEL), BF16),
        pltpu.VMEM((ahead_rows, HG_COLS), F32),
        pltpu.VMEM((R, 2 * W_S5), F32),
        pltpu.VMEM((R, 2 * S5_LANES), F32),
        pltpu.VMEM((R, W_S5), BF16),
        pltpu.VMEM((R, W_HG), F32),
        pltpu.VMEM((R, W_HG), chunk_dt),
        pltpu.VMEM((R, W_HG), chunk_dt),
        pltpu.VMEM((R, W_HG), chunk_dt),
        pltpu.VMEM((HG_HEADS, ROW_BLOCK, HG_DK), F32),
        pltpu.VMEM((ROW_BLOCK, ROW_BLOCK), jnp.int32),
        pltpu.VMEM((ROW_BLOCK, ROW_BLOCK), BF16),
        pltpu.VMEM((NB, SUBLANES, W_HG), F32),
        pltpu.VMEM((R, D_MODEL), BF16),
    ]
    return pl.pallas_call(
        functools.partial(_mixer_kernel, NB=NB, C=C, t_valid=t_valid, has_state=has_state),
        grid=grid,
        in_specs=in_specs,
        out_specs=[x_spec, h_spec, h_spec, s_spec],
        out_shape=out_shape,
        scratch_shapes=scratch,
        compiler_params=pltpu.CompilerParams(
            dimension_semantics=("arbitrary", "arbitrary"),
            vmem_limit_bytes=V7X_VMEM_LIMIT),
        name=name,
    )(*args)


def kernel(x_prompt, x_sample, state_s5_re, state_s5_im, state_hgrn, norm_g, w_in, s5_lambda_re, s5_lambda_im,
           s5_log_step, s5_b_re, s5_b_im, s5_c_re, s5_c_im, s5_d, w_glu, b_glu, hgrn_lb_logits, hgrn_onorm_g,
           w_out, final_norm_g):
    assert norm_g.shape[0] == 1, "single-layer model"
    l = 0
    a_re, a_im, bbre, bbim, lb = _prepare_params(
        s5_lambda_re[l], s5_lambda_im[l], s5_log_step[l], s5_b_re[l], s5_b_im[l], hgrn_lb_logits)
    bcat, ccat = _block_diag_weights(bbre, bbim, s5_c_re[l], s5_c_im[l])
    weights = (
        norm_g[l].astype(F32).reshape(1, D_MODEL),
        w_in[l].astype(BF16),
        a_re, a_im, bcat, ccat,
        s5_d[l].astype(F32).reshape(1, W_S5),
        w_glu[l].astype(BF16),
        b_glu[l].astype(F32).reshape(1, W_S5),
        lb,
        hgrn_onorm_g[l].astype(F32).reshape(1, W_HG),
        w_out[l].astype(BF16),
        final_norm_g.astype(F32).reshape(1, D_MODEL),
    )

    Bp, Lp, _ = x_prompt.shape
    yp, p_re, p_im, p_hg = _mixer_call(x_prompt, None, weights, NB=Bp, C=64, t_valid=64, name="mixer_prompt")

    Bs, Ls, _ = x_sample.shape
    state = (state_s5_re[l].reshape(Bs, S5_LANES).astype(F32),
             state_s5_im[l].reshape(Bs, S5_LANES).astype(F32),
             state_hgrn[l].astype(F32))
    ys, s_re, s_im, s_hg = _mixer_call(x_sample, state, weights, NB=ROW_BLOCK // Ls, C=Ls, t_valid=Ls,
                                       name="mixer_sample")
    ys = ys.reshape(Bs, Ls, D_MODEL)

    sd = state_s5_re.dtype
    shp = (1, Bp, S5_GROUPS, S5_STATE)
    shs = (1, Bs, S5_GROUPS, S5_STATE)
    return (yp, ys,
            p_re.reshape(shp).astype(sd), p_im.reshape(shp).astype(sd), p_hg[None].astype(state_hgrn.dtype),
            s_re.reshape(shs).astype(sd), s_im.reshape(shs).astype(sd), s_hg[None].astype(state_hgrn.dtype))
```

```python
import functools
import math

import jax
import jax.numpy as jnp
import numpy as np
from jax import lax
from jax.experimental import pallas as pl
from jax.experimental.pallas import tpu as pltpu

F32 = jnp.float32
BF16 = jnp.bfloat16

D_MODEL = 1024
W_S5 = 512
W_HG = 512
S5_CH = 16
S5_GROUPS = 32
S5_STATE = 64
S5_LANES = S5_GROUPS * S5_STATE
S5_HALF = S5_LANES // 2
HG_DK = 128
HG_HEADS = 4
HG_DV = 128
EPS = 1e-6
LAMBDA_RE_MAX = -1e-4
PROJ_OUT = 3072
OFF_Q = 1024
HG_COLS = PROJ_OUT - OFF_Q
LOG2E = 1.4426950408889634

LANES = 128
SUBLANES = 8
BF16_TILE_ROWS = 16
ROW_BLOCK = 128
MATMUL_ROWS = 512
READOUT_ROWS = 512
S5_INPUT_ROWS = 512
V7X_VMEM_LIMIT = 60 * 1024 * 1024


def _prep_kernel(lre_ref, lim_ref, ls_ref, brt_ref, bit_ref, lg_ref,
                 are_ref, aim_ref, bbre_ref, bbim_ref, lb_ref):
    lr = jnp.minimum(lre_ref[...], LAMBDA_RE_MAX)
    li = lim_ref[...]
    dt = jnp.exp(ls_ref[...])
    er = jnp.exp(lr * dt)
    a_re = er * jnp.cos(li * dt)
    a_im = er * jnp.sin(li * dt)
    den = lr * lr + li * li
    zr = ((a_re - 1.0) * lr + a_im * li) / den
    zi = (a_im * lr - (a_re - 1.0) * li) / den
    are_ref[...] = a_re
    aim_ref[...] = a_im
    for g in range(S5_GROUPS):
        br = brt_ref[g]
        bi = bit_ref[g]
        zr_g = zr[g:g + 1, :]
        zi_g = zi[g:g + 1, :]
        bbre_ref[g] = zr_g * br - zi_g * bi
        bbim_ref[g] = zr_g * bi + zi_g * br
    lg = lg_ref[...]
    e = jnp.exp(lg - jnp.max(lg, axis=0, keepdims=True))
    lb_ref[...] = e[0:1, :] / jnp.sum(e, axis=0, keepdims=True)


def _prepare_params(lam_re, lam_im, log_step, b_re, b_im, lb_logits):
    gps = jax.ShapeDtypeStruct((S5_GROUPS, S5_STATE), F32)
    gcs = jax.ShapeDtypeStruct((S5_GROUPS, S5_CH, S5_STATE), F32)
    a_re, a_im, bbre, bbim, lb = pl.pallas_call(
        _prep_kernel,
        out_shape=[gps, gps, gcs, gcs, jax.ShapeDtypeStruct((1, HG_HEADS * HG_DK), F32)],
        name="s5_hgrn_prep",
    )(lam_re.astype(F32), lam_im.astype(F32), log_step.astype(F32).reshape(S5_GROUPS, 1),
      jnp.transpose(b_re.astype(F32), (0, 2, 1)), jnp.transpose(b_im.astype(F32), (0, 2, 1)),
      lb_logits.astype(F32))
    return a_re.reshape(1, S5_LANES), a_im.reshape(1, S5_LANES), bbre, bbim, lb


def _block_diag_weights(bbre, bbim, c_re, c_im):
    hg = S5_GROUPS // 2
    eye = jnp.eye(hg, dtype=BF16)[None, :, None, :, None]

    def in_half(bbt):
        x = bbt.astype(BF16).reshape(2, hg, S5_CH, 1, S5_STATE) * eye
        return x.reshape(2, hg * S5_CH, hg * S5_STATE)

    def out_half(c):
        ct = jnp.transpose(c.astype(BF16), (0, 2, 1)).reshape(2, hg, S5_STATE, 1, S5_CH)
        return (ct * eye).reshape(2, hg * S5_STATE, hg * S5_CH)

    bcat = jnp.concatenate([in_half(bbre), in_half(bbim)], axis=2)
    ccat = jnp.concatenate([out_half(c_re), out_half(-c_im)], axis=1)
    return bcat, ccat


def _sigmoid(x):
    return 1.0 / (1.0 + jnp.exp2(x * (-LOG2E)))


def _gelu_tanh(x):
    return 0.5 * x * (1.0 + jnp.tanh(math.sqrt(2.0 / math.pi) * (x + 0.044715 * (x * x * x))))


def _dot(a, b):
    return jnp.dot(a, b, preferred_element_type=F32)


def _dot_nt(a, b):
    return lax.dot_general(a, b, (((1,), (1,)), ((), ())), preferred_element_type=F32)


def _dot_tn(a, b):
    return lax.dot_general(a, b, (((0,), (0,)), ((), ())), preferred_element_type=F32)


def _row_bcast_blocks(src_ref, blk, off):
    parts = []
    for n in range(ROW_BLOCK // blk):
        row = src_ref[n * blk + off:n * blk + off + 1, :]
        parts.append(jnp.broadcast_to(row, (blk, HG_DK)))
    return parts[0] if len(parts) == 1 else jnp.concatenate(parts, axis=0)


def _mixer_kernel(*refs, NB, C, t_valid, has_state):
    if has_state:
        x_ref, h0r_ref, h0i_ref, s0_ref = refs[:4]
        refs = refs[4:]
    else:
        x_ref = refs[0]
        refs = refs[1:]
    (perm_ref, permt_ref, ng_ref, win_ref, are_ref, aim_ref, bcat_ref, ccat_ref, d_ref, wglu_ref, bglu_ref,
     lb_ref, og_ref, wout_ref, fg_ref,
     y_ref, hr_ref, hi_ref, s_ref,
     hn_ref, proj_ref, uz_ref, bu_ref, s5_ref, o_ref, qe_ref, ke_ref, v_ref, bsall_ref, lev_ref, tri_ref,
     dl_ref, cat_ref) = refs

    j = pl.program_id(1)
    nj = pl.num_programs(1)
    R = NB * C
    n_rb = R // ROW_BLOCK
    nbat = ROW_BLOCK // C

    @pl.when(j == 0)
    def _init():
        if has_state:
            hr_ref[...] = h0r_ref[...]
            hi_ref[...] = h0i_ref[...]
        else:
            hr_ref[...] = jnp.zeros(hr_ref.shape, F32)
            hi_ref[...] = jnp.zeros(hi_ref.shape, F32)
            s_ref[...] = jnp.zeros(s_ref.shape, F32)

    MM = min(R, MATMUL_ROWS)
    n_mb = R // MM
    mbat = MM // C

    def rows_of(rb, n=MM):
        return pl.ds(pl.multiple_of(rb * n, n), n)

    flat_rows = len(x_ref.shape) == 2

    def x_rows(mb):
        if flat_rows:
            return x_ref[rows_of(mb), :]
        return x_ref[pl.ds(mb * mbat, mbat)].reshape(MM, D_MODEL)

    def prenorm(x):
        return (x * lax.rsqrt(jnp.mean(x * x, axis=-1, keepdims=True) + EPS) * ng_ref[...]).astype(BF16)

    def hg_proj(hn, sec):
        return _dot(hn, win_ref[:, OFF_Q + 512 * sec:OFF_Q + 512 * (sec + 1)])

    def project_chunk():
        def body(rb, c):
            rows = rows_of(rb)
            hn = prenorm(x_rows(rb))
            hn_ref[rows, :] = hn
            for sec in range(HG_COLS // 512):
                proj_ref[rows, 512 * sec:512 * (sec + 1)] = hg_proj(hn, sec)
            return c
        lax.fori_loop(0, n_mb, body, 0)

    row = lax.broadcasted_iota(jnp.int32, (ROW_BLOCK, ROW_BLOCK), 0)
    col = lax.broadcasted_iota(jnp.int32, (ROW_BLOCK, ROW_BLOCK), 1)
    tin = row & (C - 1)
    levels = []
    m = 1
    while m < C:
        levels.append(m)
        m *= 2

    @pl.when(j == 0)
    def _masks():
        lev = jnp.where(row == col, 0, -1)
        for li, m in enumerate(levels):
            same = (row & ~(2 * m - 1)) == (col & ~(2 * m - 1))
            own = same & ((row & m) != 0) & ((col & m) == 0)
            lev = jnp.where(own, li + 1, lev)
        lev_ref[...] = lev
        in_chunk = (row & ~(C - 1)) == (col & ~(C - 1))
        tri_ref[...] = jnp.where(in_chunk & (col <= row), 1.0, 0.0).astype(BF16)

    def step_body():
        heads = range(HG_HEADS)
        hcs = [slice(HG_DK * hd, HG_DK * (hd + 1)) for hd in heads]
        scan_width = min(S5_HALF, (SUBLANES * S5_HALF) // NB)
        assert t_valid == C or n_rb == 1
        block_steps = min(ROW_BLOCK // NB, t_valid)

        def hgrn_block(rb, mxu_fill):
            rows = rows_of(rb, ROW_BLOCK)
            lv = lev_ref[...]
            qs, ks, fs, vs, gcats = [], [], [], [], []
            for hd in heads:
                lb = lb_ref[:, hcs[hd]]
                q = proj_ref[rows, HG_DK * hd:HG_DK * (hd + 1)]
                gf = proj_ref[rows, 512 + HG_DK * hd:512 + HG_DK * (hd + 1)]
                v = proj_ref[rows, 1024 + HG_DV * hd:1024 + HG_DV * (hd + 1)].astype(BF16)
                f = lb + (1.0 - lb) * _sigmoid(gf)
                g = jnp.log2(f)
                k = 1.0 - f
                if t_valid < C:
                    live = tin < t_valid
                    f = jnp.where(live, f, 1.0)
                    g = jnp.where(live, g, 0.0)
                    k = jnp.where(live, k, 0.0)
                g1 = g.astype(BF16)
                r1 = g - g1.astype(F32)
                g2 = r1.astype(BF16)
                g3 = (r1 - g2.astype(F32)).astype(BF16)
                p4 = row & 3
                dec1 = jnp.where((row & 1) == 1, f, 1.0)
                dec2 = jnp.where(p4 == 0, pltpu.roll(f, ROW_BLOCK - 1, axis=0),
                                 jnp.where(p4 == 1, 1.0,
                                           jnp.where(p4 == 2, f, f * pltpu.roll(f, 1, axis=0))))
                qs.append(q)
                ks.append(k)
                fs.append((dec1.astype(BF16), dec2.astype(BF16)))
                vs.append(v)
                gcats.append(jnp.concatenate([g1, g2, g3], axis=1))
            bparts = [_dot(tri_ref[...], gcats[hd]) for hd in heads]
            bs = []
            for hd in heads:
                bp = bparts[hd]
                b = (bp[:, 0:HG_DK] + bp[:, HG_DK:2 * HG_DK]) + bp[:, 2 * HG_DK:3 * HG_DK]
                bsall_ref[hd] = b
                bs.append(b)
            qbs = [q.astype(BF16) for q in qs]
            kbs = [k.astype(BF16) for k in ks]
            atts = [jnp.where(lv == 0, _dot_nt(qbs[hd], kbs[hd]), 0.0) for hd in heads]
            for li, m in enumerate(levels):
                two_m = 2 * m
                p2 = row & (two_m - 1)
                if m > 2:
                    sgn = jnp.where(p2 >= m, 1.0, -1.0)
                for hd in heads:
                    if m <= 2:
                        decb = fs[hd][li]
                    else:
                        ref = _row_bcast_blocks(bsall_ref.at[hd], two_m, m - 1)
                        decb = jnp.exp2((bs[hd] - ref) * sgn).astype(BF16)
                    atts[hd] = jnp.where(lv == li + 1, _dot_nt(qbs[hd] * decb, kbs[hd] * decb), atts[hd])
                if li == 0:
                    for piece in mxu_fill:
                        piece()
            for hd in heads:
                o_ref[rows, hcs[hd]] = _dot(atts[hd].astype(BF16), vs[hd])
            for hd in heads:
                hc = hcs[hd]
                bs_ref = bsall_ref.at[hd]
                b = bs[hd]
                if C >= SUBLANES:
                    b_last = _row_bcast_blocks(bs_ref, C, C - 1)
                else:
                    b_last = b
                    for d in range(1, C):
                        b_last = jnp.where(tin == C - 1 - d, pltpu.roll(b, ROW_BLOCK - d, axis=0), b_last)
                qe_ref[rows, hc] = (qs[hd] * jnp.exp2(b)).astype(qe_ref.dtype)
                ke_ref[rows, hc] = (ks[hd] * jnp.exp2(b_last - b)).astype(ke_ref.dtype)
                v_ref[rows, hc] = vs[hd].astype(v_ref.dtype)
                for n in range(nbat):
                    last = bs_ref[n * C + C - 1:n * C + C, :]
                    dl_ref[rb * nbat + n, :, hc] = jnp.broadcast_to(jnp.exp2(last), (SUBLANES, HG_DK))

        SR = min(R, S5_INPUT_ROWS)
        sblocks = SR // ROW_BLOCK

        def merged(it, c):
            rows = rows_of(it, SR)
            hp = _dot(perm_ref[rows, :], hn_ref[...]).astype(BF16)
            uz = {}

            def uz_piece(sec):
                uz[sec] = _dot(hp, win_ref[:, W_S5 * sec:W_S5 * (sec + 1)])
                uz_ref[rows, W_S5 * sec:W_S5 * (sec + 1)] = uz[sec]

            def bu_piece(i):
                u_i = uz[0][:, 256 * i:256 * (i + 1)].astype(BF16)
                bu_ref[rows, 2 * S5_HALF * i:2 * S5_HALF * (i + 1)] = _dot(u_i, bcat_ref[i])

            pieces = [functools.partial(uz_piece, 0), functools.partial(uz_piece, 1),
                      functools.partial(bu_piece, 0), functools.partial(bu_piece, 1)]
            per_block = len(pieces) // sblocks
            for sb in range(sblocks):
                hgrn_block(it * sblocks + sb, pieces[sb * per_block:(sb + 1) * per_block])
            for i in range(2):
                for off in range(0, S5_HALF, scan_width):
                    st = slice(S5_HALF * i + off, S5_HALF * i + off + scan_width)
                    re_c = 2 * S5_HALF * i + off
                    im_c = re_c + S5_HALF
                    ar = are_ref[:, st]
                    ai = aim_ref[:, st]
                    hr = hr_ref[:, st]
                    hi = hi_ref[:, st]
                    for tt in range(block_steps * sblocks):
                        trows = pl.ds(pl.multiple_of(it * SR + tt * NB, NB), NB)
                        hr, hi = (ar * hr - ai * hi + bu_ref[trows, re_c:re_c + scan_width],
                                  ar * hi + ai * hr + bu_ref[trows, im_c:im_c + scan_width])
                        bu_ref[trows, re_c:re_c + scan_width] = hr
                        bu_ref[trows, im_c:im_c + scan_width] = hi
                    hr_ref[:, st] = hr
                    hi_ref[:, st] = hi
            return c
        lax.fori_loop(0, R // SR, merged, 0)

        def inter_out(bi):
            srows = pl.ds(pl.multiple_of(bi * C, C), C)
            for hd in heads:
                hc = hcs[hd]
                if has_state:
                    oi = _dot(qe_ref[srows, hc], s0_ref[bi, hd].astype(qe_ref.dtype))
                else:
                    oi = _dot_nt(qe_ref[srows, hc], s_ref[bi, hd].astype(qe_ref.dtype))
                o_ref[srows, hc] = o_ref[srows, hc] + oi

        def inter_state(bi):
            srows = pl.ds(pl.multiple_of(bi * C, C), C)
            for hd in heads:
                hc = hcs[hd]
                if has_state:
                    upd = _dot_tn(ke_ref[srows, hc], v_ref[srows, hc])
                    s_ref[bi, hd] = s0_ref[bi, hd] * dl_ref[bi, :, hc].T[:, 0:1] + upd
                else:
                    upd = _dot_tn(v_ref[srows, hc], ke_ref[srows, hc])
                    s_ref[bi, hd] = s_ref[bi, hd] * dl_ref[bi, 0:1, hc] + upd

        RR = min(R, READOUT_ROWS)
        rbat = RR // C
        merge_inter = rbat <= 8

        def readout_inter(rb, c):
            rows = rows_of(rb, RR)
            ys = []
            for i in range(2):
                hs = bu_ref[rows, 2 * S5_HALF * i:2 * S5_HALF * (i + 1)].astype(BF16)
                ys.append(_dot(hs, ccat_ref[i]))
            if merge_inter:
                for n in range(rbat):
                    inter_out(rb * rbat + n)
            u = uz_ref[rows, 0:W_S5]
            y = jnp.concatenate(ys, axis=1) + d_ref[...] * u
            g = _gelu_tanh(y)
            glu = _dot(g.astype(BF16), wglu_ref[...]) + bglu_ref[...]
            if merge_inter:
                for n in range(rbat):
                    inter_state(rb * rbat + n)
            zs = uz_ref[rows, W_S5:2 * W_S5]
            s5 = g * _sigmoid(glu) * (zs * _sigmoid(zs))
            s5_ref[rows, :] = s5.astype(BF16)
            return c
        lax.fori_loop(0, R // RR, readout_inter, 0)

        if not merge_inter and C >= SUBLANES:
            def inter(bi, c):
                inter_out(bi)
                inter_state(bi)
                return c
            lax.fori_loop(0, NB, inter, 0, unroll=2)

        if C < SUBLANES:
            assert has_state and not merge_inter
            per_tile = SUBLANES // C
            rid = lax.broadcasted_iota(jnp.int32, (SUBLANES, HG_DK), 0)

            def inter_tile(ti, c):
                srows = pl.ds(pl.multiple_of(ti * SUBLANES, SUBLANES), SUBLANES)
                for hd in heads:
                    hc = hcs[hd]
                    qe8 = qe_ref[srows, hc]
                    ke8 = ke_ref[srows, hc]
                    v8 = v_ref[srows, hc]
                    acc = o_ref[srows, hc]
                    for s in range(per_tile):
                        bi = ti * per_tile + s
                        mine = (rid >= s * C) & (rid < (s + 1) * C)
                        st = s0_ref[bi, hd]
                        acc = acc + _dot(jnp.where(mine, qe8, 0.0), st)
                        upd = _dot_tn(jnp.where(mine, ke8, 0.0), v8)
                        s_ref[bi, hd] = st * dl_ref[bi, :, hc].T[:, 0:1] + upd
                    o_ref[srows, hc] = acc
                return c
            lax.fori_loop(0, NB // per_tile, inter_tile, 0, unroll=2)

        def phase5(rb, c):
            rows = rows_of(rb)
            cat_ref[rows, 0:W_S5] = _dot(permt_ref[rows, :], s5_ref[...]).astype(BF16)
            for hd in range(HG_HEADS):
                hc = slice(HG_DV * hd, HG_DV * (hd + 1))
                o = o_ref[rows, hc]
                o = o * lax.rsqrt(jnp.mean(o * o, axis=-1, keepdims=True) + EPS)
                zh = proj_ref[rows, 1536 + HG_DV * hd:1536 + HG_DV * (hd + 1)]
                hg = o * og_ref[:, hc] * (zh * _sigmoid(zh))
                cat_ref[rows, W_S5 + HG_DV * hd:W_S5 + HG_DV * (hd + 1)] = hg.astype(BF16)
            out = _dot(cat_ref[rows, :], wout_ref[...])
            xn = x_rows(rb) + out
            yv = xn * lax.rsqrt(jnp.mean(xn * xn, axis=-1, keepdims=True) + EPS) * fg_ref[...]
            if flat_rows:
                y_ref[rows, :] = yv
            else:
                y_ref[pl.ds(rb * mbat, mbat)] = yv.reshape(mbat, C, D_MODEL)
            return c
        lax.fori_loop(0, n_mb, phase5, 0)

    project_chunk()
    step_body()

    if not has_state:
        @pl.when(j == nj - 1)
        def _fin():
            def tr(b, c):
                for hd in range(HG_HEADS):
                    s_ref[b, hd] = s_ref[b, hd].T
                return c
            lax.fori_loop(0, NB, tr, 0)


def _time_major_perm(NB, C):
    R = NB * C
    dst = np.arange(R)
    src = (dst % NB) * C + dst // NB
    p = np.zeros((R, R), np.float32)
    p[dst, src] = 1.0
    return p


def _mixer_call(x, state, weights, *, NB, C, t_valid, name):
    B, L, _ = x.shape
    has_state = state is not None
    assert not has_state or L == C, "a given state is consumed by a single chunk per sequence"
    grid = (B // NB, L // C)
    R = NB * C
    perm = _time_major_perm(NB, C)
    weights = (jnp.asarray(perm, BF16), jnp.asarray(perm.T, BF16)) + tuple(weights)

    def full(a):
        nd = a.ndim
        return pl.BlockSpec(a.shape, lambda i, j, nd=nd: (0,) * nd, pipeline_mode=pl.Buffered(1))

    flat_rows = C % SUBLANES != 0
    if flat_rows:
        assert L == C
        x = x.reshape(B * L, D_MODEL)
        x_spec = pl.BlockSpec((R, D_MODEL), lambda i, j: (i, 0))
    else:
        x_spec = pl.BlockSpec((NB, C, D_MODEL), lambda i, j: (i, j, 0))
    h_spec = pl.BlockSpec((NB, S5_LANES), lambda i, j: (i, 0))
    s_spec = pl.BlockSpec((NB, HG_HEADS, HG_DK, HG_DV), lambda i, j: (i, 0, 0, 0))
    in_specs = [x_spec] + ([h_spec, h_spec, s_spec] if has_state else []) + [full(w) for w in weights]
    args = [x] + (list(state) if has_state else []) + list(weights)
    out_shape = [
        jax.ShapeDtypeStruct(x.shape, x.dtype),
        jax.ShapeDtypeStruct((B, S5_LANES), F32),
        jax.ShapeDtypeStruct((B, S5_LANES), F32),
        jax.ShapeDtypeStruct((B, HG_HEADS, HG_DK, HG_DV), F32),
    ]
    chunk_dt = BF16 if C % BF16_TILE_ROWS == 0 else F32
    scratch = [
        pltpu.VMEM((R, D_MODEL), BF16),
        pltpu.VMEM((R, HG_COLS), F32),
        pltpu.VMEM((R, 2 * W_S5), F32),
        pltpu.VMEM((R, 2 * S5_LANES), F32),
        pltpu.VMEM((R, W_S5), BF16),
        pltpu.VMEM((R, W_HG), F32),
        pltpu.VMEM((R, W_HG), chunk_dt),
        pltpu.VMEM((R, W_HG), chunk_dt),
        pltpu.VMEM((R, W_HG), chunk_dt),
        pltpu.VMEM((HG_HEADS, ROW_BLOCK, HG_DK), F32),
        pltpu.VMEM((ROW_BLOCK, ROW_BLOCK), jnp.int32),
        pltpu.VMEM((ROW_BLOCK, ROW_BLOCK), BF16),
        pltpu.VMEM((NB, SUBLANES, W_HG), F32),
        pltpu.VMEM((R, D_MODEL), BF16),
    ]
    return pl.pallas_call(
        functools.partial(_mixer_kernel, NB=NB, C=C, t_valid=t_valid, has_state=has_state),
        grid=grid,
        in_specs=in_specs,
        out_specs=[x_spec, h_spec, h_spec, s_spec],
        out_shape=out_shape,
        scratch_shapes=scratch,
        compiler_params=pltpu.CompilerParams(
            dimension_semantics=("arbitrary", "arbitrary"),
            vmem_limit_bytes=V7X_VMEM_LIMIT),
        name=name,
    )(*args)


def kernel(x_prompt, x_sample, state_s5_re, state_s5_im, state_hgrn, norm_g, w_in, s5_lambda_re, s5_lambda_im,
           s5_log_step, s5_b_re, s5_b_im, s5_c_re, s5_c_im, s5_d, w_glu, b_glu, hgrn_lb_logits, hgrn_onorm_g,
           w_out, final_norm_g):
    assert norm_g.shape[0] == 1, "single-layer model"
    l = 0
    a_re, a_im, bbre, bbim, lb = _prepare_params(
        s5_lambda_re[l], s5_lambda_im[l], s5_log_step[l], s5_b_re[l], s5_b_im[l], hgrn_lb_logits)
    bcat, ccat = _block_diag_weights(bbre, bbim, s5_c_re[l], s5_c_im[l])
    weights = (
        norm_g[l].astype(F32).reshape(1, D_MODEL),
        w_in[l].astype(BF16),
        a_re, a_im, bcat, ccat,
        s5_d[l].astype(F32).reshape(1, W_S5),
        w_glu[l].astype(BF16),
        b_glu[l].astype(F32).reshape(1, W_S5),
        lb,
        hgrn_onorm_g[l].astype(F32).reshape(1, W_HG),
        w_out[l].astype(BF16),
        final_norm_g.astype(F32).reshape(1, D_MODEL),
    )

    Bp, Lp, _ = x_prompt.shape
    yp, p_re, p_im, p_hg = _mixer_call(x_prompt, None, weights, NB=Bp, C=64, t_valid=64, name="mixer_prompt")

    Bs, Ls, _ = x_sample.shape
    state = (state_s5_re[l].reshape(Bs, S5_LANES).astype(F32),
             state_s5_im[l].reshape(Bs, S5_LANES).astype(F32),
             state_hgrn[l].astype(F32))
    ys, s_re, s_im, s_hg = _mixer_call(x_sample, state, weights, NB=ROW_BLOCK // Ls, C=Ls, t_valid=Ls,
                                       name="mixer_sample")
    ys = ys.reshape(Bs, Ls, D_MODEL)

    sd = state_s5_re.dtype
    shp = (1, Bp, S5_GROUPS, S5_STATE)
    shs = (1, Bs, S5_GROUPS, S5_STATE)
    return (yp, ys,
            p_re.reshape(shp).astype(sd), p_im.reshape(shp).astype(sd), p_hg[None].astype(state_hgrn.dtype),
            s_re.reshape(shs).astype(sd), s_im.reshape(shs).astype(sd), s_hg[None].astype(state_hgrn.dtype))
```

```python
import functools
import math

import jax
import jax.numpy as jnp
import numpy as np
from jax import lax
from jax.experimental import pallas as pl
from jax.experimental.pallas import tpu as pltpu

F32 = jnp.float32
BF16 = jnp.bfloat16

D_MODEL = 1024
W_S5 = 512
W_HG = 512
S5_CH = 16
S5_GROUPS = 32
S5_STATE = 64
S5_LANES = S5_GROUPS * S5_STATE
S5_HALF = S5_LANES // 2
HG_DK = 128
HG_HEADS = 4
HG_DV = 128
EPS = 1e-6
LAMBDA_RE_MAX = -1e-4
PROJ_OUT = 3072
OFF_Q = 1024
HG_COLS = PROJ_OUT - OFF_Q
LOG2E = 1.4426950408889634

LANES = 128
SUBLANES = 8
BF16_TILE_ROWS = 16
ROW_BLOCK = 128
MATMUL_ROWS = 512
READOUT_ROWS = 512
S5_INPUT_ROWS = 512
V7X_VMEM_LIMIT = 60 * 1024 * 1024


def _prep_kernel(lre_ref, lim_ref, ls_ref, brt_ref, bit_ref, lg_ref,
                 are_ref, aim_ref, bbre_ref, bbim_ref, lb_ref):
    lr = jnp.minimum(lre_ref[...], LAMBDA_RE_MAX)
    li = lim_ref[...]
    dt = jnp.exp(ls_ref[...])
    er = jnp.exp(lr * dt)
    a_re = er * jnp.cos(li * dt)
    a_im = er * jnp.sin(li * dt)
    den = lr * lr + li * li
    zr = ((a_re - 1.0) * lr + a_im * li) / den
    zi = (a_im * lr - (a_re - 1.0) * li) / den
    are_ref[...] = a_re
    aim_ref[...] = a_im
    for g in range(S5_GROUPS):
        br = brt_ref[g]
        bi = bit_ref[g]
        zr_g = zr[g:g + 1, :]
        zi_g = zi[g:g + 1, :]
        bbre_ref[g] = zr_g * br - zi_g * bi
        bbim_ref[g] = zr_g * bi + zi_g * br
    lg = lg_ref[...]
    e = jnp.exp(lg - jnp.max(lg, axis=0, keepdims=True))
    lb_ref[...] = e[0:1, :] / jnp.sum(e, axis=0, keepdims=True)


def _prepare_params(lam_re, lam_im, log_step, b_re, b_im, lb_logits):
    gps = jax.ShapeDtypeStruct((S5_GROUPS, S5_STATE), F32)
    gcs = jax.ShapeDtypeStruct((S5_GROUPS, S5_CH, S5_STATE), F32)
    a_re, a_im, bbre, bbim, lb = pl.pallas_call(
        _prep_kernel,
        out_shape=[gps, gps, gcs, gcs, jax.ShapeDtypeStruct((1, HG_HEADS * HG_DK), F32)],
        name="s5_hgrn_prep",
    )(lam_re.astype(F32), lam_im.astype(F32), log_step.astype(F32).reshape(S5_GROUPS, 1),
      jnp.transpose(b_re.astype(F32), (0, 2, 1)), jnp.transpose(b_im.astype(F32), (0, 2, 1)),
      lb_logits.astype(F32))
    return a_re.reshape(1, S5_LANES), a_im.reshape(1, S5_LANES), bbre, bbim, lb


def _block_diag_weights(bbre, bbim, c_re, c_im):
    hg = S5_GROUPS // 2
    eye = jnp.eye(hg, dtype=BF16)[None, :, None, :, None]

    def in_half(bbt):
        x = bbt.astype(BF16).reshape(2, hg, S5_CH, 1, S5_STATE) * eye
        return x.reshape(2, hg * S5_CH, hg * S5_STATE)

    def out_half(c):
        ct = jnp.transpose(c.astype(BF16), (0, 2, 1)).reshape(2, hg, S5_STATE, 1, S5_CH)
        return (ct * eye).reshape(2, hg * S5_STATE, hg * S5_CH)

    bcat = jnp.concatenate([in_half(bbre), in_half(bbim)], axis=2)
    ccat = jnp.concatenate([out_half(c_re), out_half(-c_im)], axis=1)
    return bcat, ccat


def _sigmoid(x):
    return 1.0 / (1.0 + jnp.exp2(x * (-LOG2E)))


def _gelu_tanh(x):
    return 0.5 * x * (1.0 + jnp.tanh(math.sqrt(2.0 / math.pi) * (x + 0.044715 * (x * x * x))))


def _dot(a, b):
    return jnp.dot(a, b, preferred_element_type=F32)


def _dot_nt(a, b):
    return lax.dot_general(a, b, (((1,), (1,)), ((), ())), preferred_element_type=F32)


def _dot_tn(a, b):
    return lax.dot_general(a, b, (((0,), (0,)), ((), ())), preferred_element_type=F32)


def _row_bcast_blocks(src_ref, blk, off):
    parts = []
    for n in range(ROW_BLOCK // blk):
        row = src_ref[n * blk + off:n * blk + off + 1, :]
        parts.append(jnp.broadcast_to(row, (blk, HG_DK)))
    return parts[0] if len(parts) == 1 else jnp.concatenate(parts, axis=0)


def _mixer_kernel(*refs, NB, C, t_valid, has_state):
    if has_state:
        x_ref, h0r_ref, h0i_ref, s0_ref = refs[:4]
        refs = refs[4:]
    else:
        x_ref = refs[0]
        refs = refs[1:]
    (perm_ref, permt_ref, ng_ref, win_ref, are_ref, aim_ref, bcat_ref, ccat_ref, d_ref, wglu_ref, bglu_ref,
     lb_ref, og_ref, wout_ref, fg_ref,
     y_ref, hr_ref, hi_ref, s_ref,
     hn_ref, proj_ref, uz_ref, bu_ref, s5_ref, o_ref, qe_ref, ke_ref, v_ref, bsall_ref, lev_ref, tri_ref,
     dl_ref, cat_ref) = refs

    j = pl.program_id(1)
    nj = pl.num_programs(1)
    R = NB * C
    n_rb = R // ROW_BLOCK
    nbat = ROW_BLOCK // C

    @pl.when(j == 0)
    def _init():
        if has_state:
            hr_ref[...] = h0r_ref[...]
            hi_ref[...] = h0i_ref[...]
        else:
            hr_ref[...] = jnp.zeros(hr_ref.shape, F32)
            hi_ref[...] = jnp.zeros(hi_ref.shape, F32)
            s_ref[...] = jnp.zeros(s_ref.shape, F32)

    MM = min(R, MATMUL_ROWS)
    n_mb = R // MM
    mbat = MM // C

    def aligned(start, n):
        return start if isinstance(start, int) else pl.multiple_of(start, n)

    def rows_of(rb, n=MM):
        return pl.ds(aligned(rb * n, n), n)

    def loop(trips, body):
        if trips == 1:
            body(0, 0)
        else:
            lax.fori_loop(0, trips, body, 0)

    TB = ROW_BLOCK // NB

    def seq_major_block(src_ref, blk):
        assert isinstance(blk, int)
        if TB == C:
            return src_ref[blk * ROW_BLOCK:(blk + 1) * ROW_BLOCK, :]
        return jnp.concatenate([src_ref[b * C + blk * TB:b * C + (blk + 1) * TB, :] for b in range(NB)], axis=0)

    flat_rows = len(x_ref.shape) == 2

    def x_rows(mb):
        if flat_rows:
            return x_ref[rows_of(mb), :]
        return x_ref[pl.ds(mb * mbat, mbat)].reshape(MM, D_MODEL)

    def prenorm(x):
        return (x * lax.rsqrt(jnp.mean(x * x, axis=-1, keepdims=True) + EPS) * ng_ref[...]).astype(BF16)

    def hg_proj(hn, sec):
        return _dot(hn, win_ref[:, OFF_Q + 512 * sec:OFF_Q + 512 * (sec + 1)])

    def project_chunk():
        def body(rb, c):
            rows = rows_of(rb)
            hn = prenorm(x_rows(rb))
            hn_ref[rows, :] = hn
            for sec in range(HG_COLS // 512):
                proj_ref[rows, 512 * sec:512 * (sec + 1)] = hg_proj(hn, sec)
            return c
        loop(n_mb, body)

    row = lax.broadcasted_iota(jnp.int32, (ROW_BLOCK, ROW_BLOCK), 0)
    col = lax.broadcasted_iota(jnp.int32, (ROW_BLOCK, ROW_BLOCK), 1)
    tin = row & (C - 1)
    levels = []
    m = 1
    while m < C:
        levels.append(m)
        m *= 2

    @pl.when(j == 0)
    def _masks():
        lev = jnp.where(row == col, 0, -1)
        for li, m in enumerate(levels):
            same = (row & ~(2 * m - 1)) == (col & ~(2 * m - 1))
            own = same & ((row & m) != 0) & ((col & m) == 0)
            lev = jnp.where(own, li + 1, lev)
        lev_ref[...] = lev
        in_chunk = (row & ~(C - 1)) == (col & ~(C - 1))
        tri_ref[...] = jnp.where(in_chunk & (col <= row), 1.0, 0.0).astype(BF16)

    def step_body():
        heads = range(HG_HEADS)
        hcs = [slice(HG_DK * hd, HG_DK * (hd + 1)) for hd in heads]
        scan_width = min(S5_HALF, (SUBLANES * S5_HALF) // NB)
        assert t_valid == C or n_rb == 1
        block_steps = min(ROW_BLOCK // NB, t_valid)

        def hgrn_block(rb, mxu_fill):
            rows = rows_of(rb, ROW_BLOCK)
            lv = lev_ref[...]
            qs, ks, fs, vs, gcats = [], [], [], [], []
            for hd in heads:
                lb = lb_ref[:, hcs[hd]]
                q = proj_ref[rows, HG_DK * hd:HG_DK * (hd + 1)]
                gf = proj_ref[rows, 512 + HG_DK * hd:512 + HG_DK * (hd + 1)]
                v = proj_ref[rows, 1024 + HG_DV * hd:1024 + HG_DV * (hd + 1)].astype(BF16)
                f = lb + (1.0 - lb) * _sigmoid(gf)
                g = jnp.log2(f)
                k = 1.0 - f
                if t_valid < C:
                    live = tin < t_valid
                    f = jnp.where(live, f, 1.0)
                    g = jnp.where(live, g, 0.0)
                    k = jnp.where(live, k, 0.0)
                g1 = g.astype(BF16)
                r1 = g - g1.astype(F32)
                g2 = r1.astype(BF16)
                g3 = (r1 - g2.astype(F32)).astype(BF16)
                p4 = row & 3
                dec1 = jnp.where((row & 1) == 1, f, 1.0)
                dec2 = jnp.where(p4 == 0, pltpu.roll(f, ROW_BLOCK - 1, axis=0),
                                 jnp.where(p4 == 1, 1.0,
                                           jnp.where(p4 == 2, f, f * pltpu.roll(f, 1, axis=0))))
                qs.append(q)
                ks.append(k)
                fs.append((dec1.astype(BF16), dec2.astype(BF16)))
                vs.append(v)
                gcats.append(jnp.concatenate([g1, g2, g3], axis=1))
            bparts = [_dot(tri_ref[...], gcats[hd]) for hd in heads]
            bs = []
            for hd in heads:
                bp = bparts[hd]
                b = (bp[:, 0:HG_DK] + bp[:, HG_DK:2 * HG_DK]) + bp[:, 2 * HG_DK:3 * HG_DK]
                bsall_ref[hd] = b
                bs.append(b)
            qbs = [q.astype(BF16) for q in qs]
            kbs = [k.astype(BF16) for k in ks]
            atts = [jnp.where(lv == 0, _dot_nt(qbs[hd], kbs[hd]), 0.0) for hd in heads]
            for li, m in enumerate(levels):
                two_m = 2 * m
                p2 = row & (two_m - 1)
                if m > 2:
                    sgn = jnp.where(p2 >= m, 1.0, -1.0)
                for hd in heads:
                    if m <= 2:
                        decb = fs[hd][li]
                    else:
                        ref = _row_bcast_blocks(bsall_ref.at[hd], two_m, m - 1)
                        decb = jnp.exp2((bs[hd] - ref) * sgn).astype(BF16)
                    atts[hd] = jnp.where(lv == li + 1, _dot_nt(qbs[hd] * decb, kbs[hd] * decb), atts[hd])
                if li == 0:
                    for piece in mxu_fill:
                        piece()
            for hd in heads:
                o_ref[rows, hcs[hd]] = _dot(atts[hd].astype(BF16), vs[hd])
            for hd in heads:
                hc = hcs[hd]
                bs_ref = bsall_ref.at[hd]
                b = bs[hd]
                if C >= SUBLANES:
                    b_last = _row_bcast_blocks(bs_ref, C, C - 1)
                else:
                    b_last = b
                    for d in range(1, C):
                        b_last = jnp.where(tin == C - 1 - d, pltpu.roll(b, ROW_BLOCK - d, axis=0), b_last)
                qe_ref[rows, hc] = (qs[hd] * jnp.exp2(b)).astype(qe_ref.dtype)
                ke_ref[rows, hc] = (ks[hd] * jnp.exp2(b_last - b)).astype(ke_ref.dtype)
                v_ref[rows, hc] = vs[hd].astype(v_ref.dtype)
                for n in range(nbat):
                    last = bs_ref[n * C + C - 1:n * C + C, :]
                    dl_ref[rb * nbat + n, :, hc] = jnp.broadcast_to(jnp.exp2(last), (SUBLANES, HG_DK))

        SR = min(R, S5_INPUT_ROWS)
        sblocks = SR // ROW_BLOCK

        def merged(it, c):
            rows = rows_of(it, SR)
            hp = jnp.concatenate(
                [_dot(perm_ref[...], seq_major_block(hn_ref, it * sblocks + sb)).astype(BF16)
                 for sb in range(sblocks)], axis=0)
            uz = {}

            def uz_piece(sec):
                uz[sec] = _dot(hp, win_ref[:, W_S5 * sec:W_S5 * (sec + 1)])
                uz_ref[rows, W_S5 * sec:W_S5 * (sec + 1)] = uz[sec]

            def bu_piece(i):
                u_i = uz[0][:, 256 * i:256 * (i + 1)].astype(BF16)
                bu_ref[rows, 2 * S5_HALF * i:2 * S5_HALF * (i + 1)] = _dot(u_i, bcat_ref[i])

            pieces = [functools.partial(uz_piece, 0), functools.partial(uz_piece, 1),
                      functools.partial(bu_piece, 0), functools.partial(bu_piece, 1)]
            per_block = len(pieces) // sblocks
            for sb in range(sblocks):
                hgrn_block(it * sblocks + sb, pieces[sb * per_block:(sb + 1) * per_block])
            for i in range(2):
                for off in range(0, S5_HALF, scan_width):
                    st = slice(S5_HALF * i + off, S5_HALF * i + off + scan_width)
                    re_c = 2 * S5_HALF * i + off
                    im_c = re_c + S5_HALF
                    ar = are_ref[:, st]
                    ai = aim_ref[:, st]
                    hr = hr_ref[:, st]
                    hi = hi_ref[:, st]
                    for tt in range(block_steps * sblocks):
                        trows = pl.ds(aligned(it * SR + tt * NB, NB), NB)
                        hr, hi = (ar * hr - ai * hi + bu_ref[trows, re_c:re_c + scan_width],
                                  ar * hi + ai * hr + bu_ref[trows, im_c:im_c + scan_width])
                        bu_ref[trows, re_c:re_c + scan_width] = hr
                        bu_ref[trows, im_c:im_c + scan_width] = hi
                    hr_ref[:, st] = hr
                    hi_ref[:, st] = hi
            return c
        assert R == SR
        loop(R // SR, merged)

        def inter_out(bi):
            srows = pl.ds(aligned(bi * C, C), C)
            for hd in heads:
                hc = hcs[hd]
                if has_state:
                    oi = _dot(qe_ref[srows, hc], s0_ref[bi, hd].astype(qe_ref.dtype))
                else:
                    oi = _dot_nt(qe_ref[srows, hc], s_ref[bi, hd].astype(qe_ref.dtype))
                o_ref[srows, hc] = o_ref[srows, hc] + oi

        def inter_state(bi):
            srows = pl.ds(aligned(bi * C, C), C)
            for hd in heads:
                hc = hcs[hd]
                if has_state:
                    upd = _dot_tn(ke_ref[srows, hc], v_ref[srows, hc])
                    s_ref[bi, hd] = s0_ref[bi, hd] * dl_ref[bi, :, hc].T[:, 0:1] + upd
                else:
                    upd = _dot_tn(v_ref[srows, hc], ke_ref[srows, hc])
                    s_ref[bi, hd] = s_ref[bi, hd] * dl_ref[bi, 0:1, hc] + upd

        RR = min(R, READOUT_ROWS)
        rbat = RR // C
        merge_inter = rbat <= 8

        def readout_inter(rb, c):
            rows = rows_of(rb, RR)
            ys = []
            for i in range(2):
                hs = bu_ref[rows, 2 * S5_HALF * i:2 * S5_HALF * (i + 1)].astype(BF16)
                ys.append(_dot(hs, ccat_ref[i]))
            if merge_inter:
                for n in range(rbat):
                    inter_out(rb * rbat + n)
            u = uz_ref[rows, 0:W_S5]
            y = jnp.concatenate(ys, axis=1) + d_ref[...] * u
            g = _gelu_tanh(y)
            glu = _dot(g.astype(BF16), wglu_ref[...]) + bglu_ref[...]
            if merge_inter:
                for n in range(rbat):
                    inter_state(rb * rbat + n)
            zs = uz_ref[rows, W_S5:2 * W_S5]
            s5 = g * _sigmoid(glu) * (zs * _sigmoid(zs))
            s5_ref[rows, :] = s5.astype(BF16)
            return c
        loop(R // RR, readout_inter)

        if not merge_inter and C >= SUBLANES:
            def inter(bi, c):
                inter_out(bi)
                inter_state(bi)
                return c
            lax.fori_loop(0, NB, inter, 0, unroll=2)

        if C < SUBLANES:
            assert has_state and not merge_inter
            per_tile = SUBLANES // C
            rid = lax.broadcasted_iota(jnp.int32, (SUBLANES, HG_DK), 0)

            def inter_tile(ti, c):
                srows = pl.ds(pl.multiple_of(ti * SUBLANES, SUBLANES), SUBLANES)
                for hd in heads:
                    hc = hcs[hd]
                    qe8 = qe_ref[srows, hc]
                    ke8 = ke_ref[srows, hc]
                    v8 = v_ref[srows, hc]
                    acc = o_ref[srows, hc]
                    for s in range(per_tile):
                        bi = ti * per_tile + s
                        mine = (rid >= s * C) & (rid < (s + 1) * C)
                        st = s0_ref[bi, hd]
                        acc = acc + _dot(jnp.where(mine, qe8, 0.0), st)
                        upd = _dot_tn(jnp.where(mine, ke8, 0.0), v8)
                        s_ref[bi, hd] = st * dl_ref[bi, :, hc].T[:, 0:1] + upd
                    o_ref[srows, hc] = acc
                return c
            lax.fori_loop(0, NB // per_tile, inter_tile, 0, unroll=2)

        for blk in range(n_rb):
            back = _dot(permt_ref[...], s5_ref[blk * ROW_BLOCK:(blk + 1) * ROW_BLOCK, :]).astype(BF16)
            if TB == C:
                cat_ref[blk * ROW_BLOCK:(blk + 1) * ROW_BLOCK, 0:W_S5] = back
            else:
                for b in range(NB):
                    cat_ref[b * C + blk * TB:b * C + (blk + 1) * TB, 0:W_S5] = back[b * TB:(b + 1) * TB, :]

        def phase5(rb, c):
            rows = rows_of(rb)
            for hd in range(HG_HEADS):
                hc = slice(HG_DV * hd, HG_DV * (hd + 1))
                o = o_ref[rows, hc]
                o = o * lax.rsqrt(jnp.mean(o * o, axis=-1, keepdims=True) + EPS)
                zh = proj_ref[rows, 1536 + HG_DV * hd:1536 + HG_DV * (hd + 1)]
                hg = o * og_ref[:, hc] * (zh * _sigmoid(zh))
                cat_ref[rows, W_S5 + HG_DV * hd:W_S5 + HG_DV * (hd + 1)] = hg.astype(BF16)
            out = _dot(cat_ref[rows, :], wout_ref[...])
            xn = x_rows(rb) + out
            yv = xn * lax.rsqrt(jnp.mean(xn * xn, axis=-1, keepdims=True) + EPS) * fg_ref[...]
            if flat_rows:
                y_ref[rows, :] = yv
            else:
                y_ref[pl.ds(rb * mbat, mbat)] = yv.reshape(mbat, C, D_MODEL)
            return c
        loop(n_mb, phase5)

    project_chunk()
    step_body()

    if not has_state:
        @pl.when(j == nj - 1)
        def _fin():
            def tr(b, c):
                for hd in range(HG_HEADS):
                    s_ref[b, hd] = s_ref[b, hd].T
                return c
            lax.fori_loop(0, NB, tr, 0)


def _time_major_perm(NB):
    steps = ROW_BLOCK // NB
    dst = np.arange(ROW_BLOCK)
    src = (dst % NB) * steps + dst // NB
    p = np.zeros((ROW_BLOCK, ROW_BLOCK), np.float32)
    p[dst, src] = 1.0
    return p


def _mixer_call(x, state, weights, *, NB, C, t_valid, name):
    B, L, _ = x.shape
    has_state = state is not None
    assert not has_state or L == C, "a given state is consumed by a single chunk per sequence"
    grid = (B // NB, L // C)
    R = NB * C
    perm = _time_major_perm(NB)
    weights = (jnp.asarray(perm, BF16), jnp.asarray(perm.T, BF16)) + tuple(weights)

    def full(a):
        nd = a.ndim
        return pl.BlockSpec(a.shape, lambda i, j, nd=nd: (0,) * nd, pipeline_mode=pl.Buffered(1))

    flat_rows = C % SUBLANES != 0
    if flat_rows:
        assert L == C
        x = x.reshape(B * L, D_MODEL)
        x_spec = pl.BlockSpec((R, D_MODEL), lambda i, j: (i, 0))
    else:
        x_spec = pl.BlockSpec((NB, C, D_MODEL), lambda i, j: (i, j, 0))
    h_spec = pl.BlockSpec((NB, S5_LANES), lambda i, j: (i, 0))
    s_spec = pl.BlockSpec((NB, HG_HEADS, HG_DK, HG_DV), lambda i, j: (i, 0, 0, 0))
    in_specs = [x_spec] + ([h_spec, h_spec, s_spec] if has_state else []) + [full(w) for w in weights]
    args = [x] + (list(state) if has_state else []) + list(weights)
    out_shape = [
        jax.ShapeDtypeStruct(x.shape, x.dtype),
        jax.ShapeDtypeStruct((B, S5_LANES), F32),
        jax.ShapeDtypeStruct((B, S5_LANES), F32),
        jax.ShapeDtypeStruct((B, HG_HEADS, HG_DK, HG_DV), F32),
    ]
    chunk_dt = BF16 if C % BF16_TILE_ROWS == 0 else F32
    scratch = [
        pltpu.VMEM((R, D_MODEL), BF16),
        pltpu.VMEM((R, HG_COLS), F32),
        pltpu.VMEM((R, 2 * W_S5), F32),
        pltpu.VMEM((R, 2 * S5_LANES), F32),
        pltpu.VMEM((R, W_S5), BF16),
        pltpu.VMEM((R, W_HG), F32),
        pltpu.VMEM((R, W_HG), chunk_dt),
        pltpu.VMEM((R, W_HG), chunk_dt),
        pltpu.VMEM((R, W_HG), chunk_dt),
        pltpu.VMEM((HG_HEADS, ROW_BLOCK, HG_DK), F32),
        pltpu.VMEM((ROW_BLOCK, ROW_BLOCK), jnp.int32),
        pltpu.VMEM((ROW_BLOCK, ROW_BLOCK), BF16),
        pltpu.VMEM((NB, SUBLANES, W_HG), F32),
        pltpu.VMEM((R, D_MODEL), BF16),
    ]
    return pl.pallas_call(
        functools.partial(_mixer_kernel, NB=NB, C=C, t_valid=t_valid, has_state=has_state),
        grid=grid,
        in_specs=in_specs,
        out_specs=[x_spec, h_spec, h_spec, s_spec],
        out_shape=out_shape,
        scratch_shapes=scratch,
        compiler_params=pltpu.CompilerParams(
            dimension_semantics=("arbitrary", "arbitrary"),
            vmem_limit_bytes=V7X_VMEM_LIMIT),
        name=name,
    )(*args)


def kernel(x_prompt, x_sample, state_s5_re, state_s5_im, state_hgrn, norm_g, w_in, s5_lambda_re, s5_lambda_im,
           s5_log_step, s5_b_re, s5_b_im, s5_c_re, s5_c_im, s5_d, w_glu, b_glu, hgrn_lb_logits, hgrn_onorm_g,
           w_out, final_norm_g):
    assert norm_g.shape[0] == 1, "single-layer model"
    l = 0
    a_re, a_im, bbre, bbim, lb = _prepare_params(
        s5_lambda_re[l], s5_lambda_im[l], s5_log_step[l], s5_b_re[l], s5_b_im[l], hgrn_lb_logits)
    bcat, ccat = _block_diag_weights(bbre, bbim, s5_c_re[l], s5_c_im[l])
    weights = (
        norm_g[l].astype(F32).reshape(1, D_MODEL),
        w_in[l].astype(BF16),
        a_re, a_im, bcat, ccat,
        s5_d[l].astype(F32).reshape(1, W_S5),
        w_glu[l].astype(BF16),
        b_glu[l].astype(F32).reshape(1, W_S5),
        lb,
        hgrn_onorm_g[l].astype(F32).reshape(1, W_HG),
        w_out[l].astype(BF16),
        final_norm_g.astype(F32).reshape(1, D_MODEL),
    )

    Bp, Lp, _ = x_prompt.shape
    yp, p_re, p_im, p_hg = _mixer_call(x_prompt, None, weights, NB=Bp, C=64, t_valid=64, name="mixer_prompt")

    Bs, Ls, _ = x_sample.shape
    state = (state_s5_re[l].reshape(Bs, S5_LANES).astype(F32),
             state_s5_im[l].reshape(Bs, S5_LANES).astype(F32),
             state_hgrn[l].astype(F32))
    ys, s_re, s_im, s_hg = _mixer_call(x_sample, state, weights, NB=ROW_BLOCK // Ls, C=Ls, t_valid=Ls,
                                       name="mixer_sample")
    ys = ys.reshape(Bs, Ls, D_MODEL)

    sd = state_s5_re.dtype
    shp = (1, Bp, S5_GROUPS, S5_STATE)
    shs = (1, Bs, S5_GROUPS, S5_STATE)
    return (yp, ys,
            p_re.reshape(shp).astype(sd), p_im.reshape(shp).astype(sd), p_hg[None].astype(state_hgrn.dtype),
            s_re.reshape(shs).astype(sd), s_im.reshape(shs).astype(sd), s_hg[None].astype(state_hgrn.dtype))
```

```python
import functools
import math

import jax
import jax.numpy as jnp
import numpy as np
from jax import lax
from jax.experimental import pallas as pl
from jax.experimental.pallas import tpu as pltpu

F32 = jnp.float32
BF16 = jnp.bfloat16

D_MODEL = 1024
W_S5 = 512
W_HG = 512
S5_CH = 16
S5_GROUPS = 32
S5_STATE = 64
S5_LANES = S5_GROUPS * S5_STATE
S5_HALF = S5_LANES // 2
HG_DK = 128
HG_HEADS = 4
HG_DV = 128
EPS = 1e-6
LAMBDA_RE_MAX = -1e-4
PROJ_OUT = 3072
OFF_Q = 1024
HG_COLS = PROJ_OUT - OFF_Q
LOG2E = 1.4426950408889634

LANES = 128
SUBLANES = 8
BF16_TILE_ROWS = 16
ROW_BLOCK = 128
MATMUL_ROWS = 512
READOUT_ROWS = 512
S5_INPUT_ROWS = 512
V7X_VMEM_LIMIT = 60 * 1024 * 1024


def _prep_kernel(lre_ref, lim_ref, ls_ref, brt_ref, bit_ref, lg_ref,
                 are_ref, aim_ref, bbre_ref, bbim_ref, lb_ref):
    lr = jnp.minimum(lre_ref[...], LAMBDA_RE_MAX)
    li = lim_ref[...]
    dt = jnp.exp(ls_ref[...])
    er = jnp.exp(lr * dt)
    a_re = er * jnp.cos(li * dt)
    a_im = er * jnp.sin(li * dt)
    den = lr * lr + li * li
    zr = ((a_re - 1.0) * lr + a_im * li) / den
    zi = (a_im * lr - (a_re - 1.0) * li) / den
    for g in range(S5_GROUPS):
        cols = slice(S5_STATE * g, S5_STATE * (g + 1))
        are_ref[:, cols] = a_re[g:g + 1, :]
        aim_ref[:, cols] = a_im[g:g + 1, :]
        br = brt_ref[g]
        bi = bit_ref[g]
        zr_g = zr[g:g + 1, :]
        zi_g = zi[g:g + 1, :]
        bbre_ref[g] = zr_g * br - zi_g * bi
        bbim_ref[g] = zr_g * bi + zi_g * br
    lg = lg_ref[...]
    e = jnp.exp(lg - jnp.max(lg, axis=0, keepdims=True))
    lb_ref[...] = e[0:1, :] / jnp.sum(e, axis=0, keepdims=True)


def _prepare_params(lam_re, lam_im, log_step, b_re, b_im, lb_logits):
    gps = jax.ShapeDtypeStruct((1, S5_LANES), F32)
    gcs = jax.ShapeDtypeStruct((S5_GROUPS, S5_CH, S5_STATE), F32)
    return pl.pallas_call(
        _prep_kernel,
        out_shape=[gps, gps, gcs, gcs, jax.ShapeDtypeStruct((1, HG_HEADS * HG_DK), F32)],
        name="s5_hgrn_prep",
    )(lam_re.astype(F32), lam_im.astype(F32), log_step.astype(F32).reshape(S5_GROUPS, 1),
      jnp.transpose(b_re.astype(F32), (0, 2, 1)), jnp.transpose(b_im.astype(F32), (0, 2, 1)),
      lb_logits.astype(F32))


def _block_diag_weights(bbre, bbim, c_re, c_im):
    hg = S5_GROUPS // 2
    eye = jnp.eye(hg, dtype=BF16)[None, :, None, :, None]

    def in_half(bbt):
        x = bbt.astype(BF16).reshape(2, hg, S5_CH, 1, S5_STATE) * eye
        return x.reshape(2, hg * S5_CH, hg * S5_STATE)

    def out_half(c):
        ct = jnp.transpose(c.astype(BF16), (0, 2, 1)).reshape(2, hg, S5_STATE, 1, S5_CH)
        return (ct * eye).reshape(2, hg * S5_STATE, hg * S5_CH)

    bcat = jnp.concatenate([in_half(bbre), in_half(bbim)], axis=2)
    ccat = jnp.concatenate([out_half(c_re), out_half(-c_im)], axis=1)
    return bcat, ccat


def _sigmoid(x):
    return 1.0 / (1.0 + jnp.exp2(x * (-LOG2E)))


def _gelu_tanh(x):
    return 0.5 * x * (1.0 + jnp.tanh(math.sqrt(2.0 / math.pi) * (x + 0.044715 * (x * x * x))))


def _dot(a, b):
    return jnp.dot(a, b, preferred_element_type=F32)


def _dot_nt(a, b):
    return lax.dot_general(a, b, (((1,), (1,)), ((), ())), preferred_element_type=F32)


def _dot_tn(a, b):
    return lax.dot_general(a, b, (((0,), (0,)), ((), ())), preferred_element_type=F32)


def _row_bcast_blocks(src_ref, blk, off):
    parts = []
    for n in range(ROW_BLOCK // blk):
        row = src_ref[n * blk + off:n * blk + off + 1, :]
        parts.append(jnp.broadcast_to(row, (blk, HG_DK)))
    return parts[0] if len(parts) == 1 else jnp.concatenate(parts, axis=0)


def _mixer_kernel(*refs, NB, C, t_valid, has_state):
    if has_state:
        x_ref, h0r_ref, h0i_ref, s0_ref = refs[:4]
        refs = refs[4:]
    else:
        x_ref = refs[0]
        refs = refs[1:]
    (perm_ref, permt_ref, vec_ref, win_ref, are_ref, aim_ref, bcat_ref, ccat_ref, wglu_ref, lb_ref, wout_ref,
     y_ref, hr_ref, hi_ref, s_ref,
     hn_ref, proj_ref, uz_ref, bu_ref, s5_ref, o_ref, qe_ref, ke_ref, v_ref, bsall_ref, lev_ref, tri_ref,
     dl_ref, cat_ref) = refs

    ng_ref, fg_ref, d_ref, bglu_ref, og_ref = (vec_ref.at[:, a:b] for a, b in _VEC_SLICES)

    j = pl.program_id(1)
    nj = pl.num_programs(1)
    R = NB * C
    n_rb = R // ROW_BLOCK
    nbat = ROW_BLOCK // C

    @pl.when(j == 0)
    def _init():
        if has_state:
            hr_ref[...] = h0r_ref[...]
            hi_ref[...] = h0i_ref[...]
        else:
            hr_ref[...] = jnp.zeros(hr_ref.shape, F32)
            hi_ref[...] = jnp.zeros(hi_ref.shape, F32)
            s_ref[...] = jnp.zeros(s_ref.shape, F32)

    MM = min(R, MATMUL_ROWS)
    n_mb = R // MM
    mbat = MM // C

    def aligned(start, n):
        return start if isinstance(start, int) else pl.multiple_of(start, n)

    def rows_of(rb, n=MM):
        return pl.ds(aligned(rb * n, n), n)

    def loop(trips, body):
        if trips == 1:
            body(0, 0)
        else:
            lax.fori_loop(0, trips, body, 0)

    TB = ROW_BLOCK // NB

    def seq_major_block(src_ref, blk):
        assert isinstance(blk, int)
        if TB == C:
            return src_ref[blk * ROW_BLOCK:(blk + 1) * ROW_BLOCK, :]
        return jnp.concatenate([src_ref[b * C + blk * TB:b * C + (blk + 1) * TB, :] for b in range(NB)], axis=0)

    flat_rows = len(x_ref.shape) == 2

    def x_rows(mb):
        if flat_rows:
            return x_ref[rows_of(mb), :]
        return x_ref[pl.ds(mb * mbat, mbat)].reshape(MM, D_MODEL)

    def prenorm(x):
        return (x * lax.rsqrt(jnp.mean(x * x, axis=-1, keepdims=True) + EPS) * ng_ref[...]).astype(BF16)

    def hg_proj(hn, sec):
        return _dot(hn, win_ref[:, OFF_Q + 512 * sec:OFF_Q + 512 * (sec + 1)])

    def project_chunk():
        def body(rb, c):
            rows = rows_of(rb)
            hn = prenorm(x_rows(rb))
            hn_ref[rows, :] = hn
            for sec in range(HG_COLS // 512):
                proj_ref[rows, 512 * sec:512 * (sec + 1)] = hg_proj(hn, sec)
            return c
        loop(n_mb, body)

    row = lax.broadcasted_iota(jnp.int32, (ROW_BLOCK, ROW_BLOCK), 0)
    col = lax.broadcasted_iota(jnp.int32, (ROW_BLOCK, ROW_BLOCK), 1)
    tin = row & (C - 1)
    levels = []
    m = 1
    while m < C:
        levels.append(m)
        m *= 2

    @pl.when(j == 0)
    def _masks():
        lev = jnp.where(row == col, 0, -1)
        for li, m in enumerate(levels):
            same = (row & ~(2 * m - 1)) == (col & ~(2 * m - 1))
            own = same & ((row & m) != 0) & ((col & m) == 0)
            lev = jnp.where(own, li + 1, lev)
        lev_ref[...] = lev
        in_chunk = (row & ~(C - 1)) == (col & ~(C - 1))
        tri_ref[...] = jnp.where(in_chunk & (col <= row), 1.0, 0.0).astype(BF16)

    def step_body():
        heads = range(HG_HEADS)
        hcs = [slice(HG_DK * hd, HG_DK * (hd + 1)) for hd in heads]
        scan_width = min(S5_HALF, (SUBLANES * S5_HALF) // NB)
        assert t_valid == C or n_rb == 1
        block_steps = min(ROW_BLOCK // NB, t_valid)

        def hgrn_block(rb, mxu_fill):
            rows = rows_of(rb, ROW_BLOCK)
            lv = lev_ref[...]
            qs, ks, fs, vs, gcats = [], [], [], [], []
            for hd in heads:
                lb = lb_ref[:, hcs[hd]]
                q = proj_ref[rows, HG_DK * hd:HG_DK * (hd + 1)]
                gf = proj_ref[rows, 512 + HG_DK * hd:512 + HG_DK * (hd + 1)]
                v = proj_ref[rows, 1024 + HG_DV * hd:1024 + HG_DV * (hd + 1)].astype(BF16)
                f = lb + (1.0 - lb) * _sigmoid(gf)
                g = jnp.log2(f)
                k = 1.0 - f
                if t_valid < C:
                    live = tin < t_valid
                    f = jnp.where(live, f, 1.0)
                    g = jnp.where(live, g, 0.0)
                    k = jnp.where(live, k, 0.0)
                g1 = g.astype(BF16)
                r1 = g - g1.astype(F32)
                g2 = r1.astype(BF16)
                g3 = (r1 - g2.astype(F32)).astype(BF16)
                p4 = row & 3
                dec1 = jnp.where((row & 1) == 1, f, 1.0)
                dec2 = jnp.where(p4 == 0, pltpu.roll(f, ROW_BLOCK - 1, axis=0),
                                 jnp.where(p4 == 1, 1.0,
                                           jnp.where(p4 == 2, f, f * pltpu.roll(f, 1, axis=0))))
                qs.append(q)
                ks.append(k)
                fs.append((dec1.astype(BF16), dec2.astype(BF16)))
                vs.append(v)
                gcats.append(jnp.concatenate([g1, g2, g3], axis=1))
            bparts = [_dot(tri_ref[...], gcats[hd]) for hd in heads]
            bs = []
            for hd in heads:
                bp = bparts[hd]
                b = (bp[:, 0:HG_DK] + bp[:, HG_DK:2 * HG_DK]) + bp[:, 2 * HG_DK:3 * HG_DK]
                bsall_ref[hd] = b
                bs.append(b)
            qbs = [q.astype(BF16) for q in qs]
            kbs = [k.astype(BF16) for k in ks]
            atts = [jnp.where(lv == 0, _dot_nt(qbs[hd], kbs[hd]), 0.0) for hd in heads]
            for li, m in enumerate(levels):
                two_m = 2 * m
                p2 = row & (two_m - 1)
                if m > 2:
                    sgn = jnp.where(p2 >= m, 1.0, -1.0)
                for hd in heads:
                    if m <= 2:
                        decb = fs[hd][li]
                    else:
                        ref = _row_bcast_blocks(bsall_ref.at[hd], two_m, m - 1)
                        decb = jnp.exp2((bs[hd] - ref) * sgn).astype(BF16)
                    atts[hd] = jnp.where(lv == li + 1, _dot_nt(qbs[hd] * decb, kbs[hd] * decb), atts[hd])
                if li == 0:
                    for piece in mxu_fill:
                        piece()
            for hd in heads:
                o_ref[rows, hcs[hd]] = _dot(atts[hd].astype(BF16), vs[hd])
            for hd in heads:
                hc = hcs[hd]
                bs_ref = bsall_ref.at[hd]
                b = bs[hd]
                if C >= SUBLANES:
                    b_last = _row_bcast_blocks(bs_ref, C, C - 1)
                else:
                    b_last = b
                    for d in range(1, C):
                        b_last = jnp.where(tin == C - 1 - d, pltpu.roll(b, ROW_BLOCK - d, axis=0), b_last)
                qe_ref[rows, hc] = (qs[hd] * jnp.exp2(b)).astype(qe_ref.dtype)
                ke_ref[rows, hc] = (ks[hd] * jnp.exp2(b_last - b)).astype(ke_ref.dtype)
                v_ref[rows, hc] = vs[hd].astype(v_ref.dtype)
                for n in range(nbat):
                    last = bs_ref[n * C + C - 1:n * C + C, :]
                    dl_ref[rb * nbat + n, :, hc] = jnp.broadcast_to(jnp.exp2(last), (SUBLANES, HG_DK))

        SR = min(R, S5_INPUT_ROWS)
        sblocks = SR // ROW_BLOCK

        def merged(it, c):
            rows = rows_of(it, SR)
            hp = jnp.concatenate(
                [_dot(perm_ref[...], seq_major_block(hn_ref, it * sblocks + sb)).astype(BF16)
                 for sb in range(sblocks)], axis=0)
            uz = {}

            def uz_piece(sec):
                uz[sec] = _dot(hp, win_ref[:, W_S5 * sec:W_S5 * (sec + 1)])
                uz_ref[rows, W_S5 * sec:W_S5 * (sec + 1)] = uz[sec]

            def bu_piece(i):
                u_i = uz[0][:, 256 * i:256 * (i + 1)].astype(BF16)
                bu_ref[rows, 2 * S5_HALF * i:2 * S5_HALF * (i + 1)] = _dot(u_i, bcat_ref[i])

            pieces = [functools.partial(uz_piece, 0), functools.partial(uz_piece, 1),
                      functools.partial(bu_piece, 0), functools.partial(bu_piece, 1)]
            per_block = len(pieces) // sblocks
            for sb in range(sblocks):
                hgrn_block(it * sblocks + sb, pieces[sb * per_block:(sb + 1) * per_block])
            for i in range(2):
                for off in range(0, S5_HALF, scan_width):
                    st = slice(S5_HALF * i + off, S5_HALF * i + off + scan_width)
                    re_c = 2 * S5_HALF * i + off
                    im_c = re_c + S5_HALF
                    ar = are_ref[:, st]
                    ai = aim_ref[:, st]
                    hr = hr_ref[:, st]
                    hi = hi_ref[:, st]
                    for tt in range(block_steps * sblocks):
                        trows = pl.ds(aligned(it * SR + tt * NB, NB), NB)
                        hr, hi = (ar * hr - ai * hi + bu_ref[trows, re_c:re_c + scan_width],
                                  ar * hi + ai * hr + bu_ref[trows, im_c:im_c + scan_width])
                        bu_ref[trows, re_c:re_c + scan_width] = hr
                        bu_ref[trows, im_c:im_c + scan_width] = hi
                    hr_ref[:, st] = hr
                    hi_ref[:, st] = hi
            return c
        assert R == SR
        loop(R // SR, merged)

        def inter_out(bi):
            srows = pl.ds(aligned(bi * C, C), C)
            for hd in heads:
                hc = hcs[hd]
                if has_state:
                    oi = _dot(qe_ref[srows, hc], s0_ref[bi, hd].astype(qe_ref.dtype))
                else:
                    oi = _dot_nt(qe_ref[srows, hc], s_ref[bi, hd].astype(qe_ref.dtype))
                o_ref[srows, hc] = o_ref[srows, hc] + oi

        def inter_state(bi):
            srows = pl.ds(aligned(bi * C, C), C)
            for hd in heads:
                hc = hcs[hd]
                if has_state:
                    upd = _dot_tn(ke_ref[srows, hc], v_ref[srows, hc])
                    s_ref[bi, hd] = s0_ref[bi, hd] * dl_ref[bi, :, hc].T[:, 0:1] + upd
                else:
                    upd = _dot_tn(v_ref[srows, hc], ke_ref[srows, hc])
                    s_ref[bi, hd] = s_ref[bi, hd] * dl_ref[bi, 0:1, hc] + upd

        RR = min(R, READOUT_ROWS)
        rbat = RR // C
        merge_inter = rbat <= 8

        def readout_inter(rb, c):
            rows = rows_of(rb, RR)
            ys = []
            for i in range(2):
                hs = bu_ref[rows, 2 * S5_HALF * i:2 * S5_HALF * (i + 1)].astype(BF16)
                ys.append(_dot(hs, ccat_ref[i]))
            if merge_inter:
                for n in range(rbat):
                    inter_out(rb * rbat + n)
            u = uz_ref[rows, 0:W_S5]
            y = jnp.concatenate(ys, axis=1) + d_ref[...] * u
            g = _gelu_tanh(y)
            glu = _dot(g.astype(BF16), wglu_ref[...]) + bglu_ref[...]
            if merge_inter:
                for n in range(rbat):
                    inter_state(rb * rbat + n)
            zs = uz_ref[rows, W_S5:2 * W_S5]
            s5 = g * _sigmoid(glu) * (zs * _sigmoid(zs))
            s5_ref[rows, :] = s5.astype(BF16)
            return c
        loop(R // RR, readout_inter)

        if not merge_inter and C >= SUBLANES:
            def inter(bi, c):
                inter_out(bi)
                inter_state(bi)
                return c
            lax.fori_loop(0, NB, inter, 0, unroll=2)

        if C < SUBLANES:
            assert has_state and not merge_inter
            per_tile = SUBLANES // C
            rid = lax.broadcasted_iota(jnp.int32, (SUBLANES, HG_DK), 0)

            def inter_tile(ti, c):
                srows = pl.ds(pl.multiple_of(ti * SUBLANES, SUBLANES), SUBLANES)
                for hd in heads:
                    hc = hcs[hd]
                    qe8 = qe_ref[srows, hc]
                    ke8 = ke_ref[srows, hc]
                    v8 = v_ref[srows, hc]
                    acc = o_ref[srows, hc]
                    for s in range(per_tile):
                        bi = ti * per_tile + s
                        mine = (rid >= s * C) & (rid < (s + 1) * C)
                        st = s0_ref[bi, hd]
                        acc = acc + _dot(jnp.where(mine, qe8, 0.0), st)
                        upd = _dot_tn(jnp.where(mine, ke8, 0.0), v8)
                        s_ref[bi, hd] = st * dl_ref[bi, :, hc].T[:, 0:1] + upd
                    o_ref[srows, hc] = acc
                return c
            lax.fori_loop(0, NB // per_tile, inter_tile, 0, unroll=4)

        for blk in range(n_rb):
            back = _dot(permt_ref[...], s5_ref[blk * ROW_BLOCK:(blk + 1) * ROW_BLOCK, :]).astype(BF16)
            if TB == C:
                cat_ref[blk * ROW_BLOCK:(blk + 1) * ROW_BLOCK, 0:W_S5] = back
            else:
                for b in range(NB):
                    cat_ref[b * C + blk * TB:b * C + (blk + 1) * TB, 0:W_S5] = back[b * TB:(b + 1) * TB, :]

        def phase5(rb, c):
            rows = rows_of(rb)
            for hd in range(HG_HEADS):
                hc = slice(HG_DV * hd, HG_DV * (hd + 1))
                o = o_ref[rows, hc]
                o = o * lax.rsqrt(jnp.mean(o * o, axis=-1, keepdims=True) + EPS)
                zh = proj_ref[rows, 1536 + HG_DV * hd:1536 + HG_DV * (hd + 1)]
                hg = o * og_ref[:, hc] * (zh * _sigmoid(zh))
                cat_ref[rows, W_S5 + HG_DV * hd:W_S5 + HG_DV * (hd + 1)] = hg.astype(BF16)
            out = _dot(cat_ref[rows, :], wout_ref[...])
            xn = x_rows(rb) + out
            yv = xn * lax.rsqrt(jnp.mean(xn * xn, axis=-1, keepdims=True) + EPS) * fg_ref[...]
            if flat_rows:
                y_ref[rows, :] = yv
            else:
                y_ref[pl.ds(rb * mbat, mbat)] = yv.reshape(mbat, C, D_MODEL)
            return c
        loop(n_mb, phase5)

    project_chunk()
    step_body()

    if not has_state:
        @pl.when(j == nj - 1)
        def _fin():
            def tr(b, c):
                for hd in range(HG_HEADS):
                    s_ref[b, hd] = s_ref[b, hd].T
                return c
            lax.fori_loop(0, NB, tr, 0)


_VEC_WIDTHS = (D_MODEL, D_MODEL, W_S5, W_S5, W_HG)
_VEC_SLICES = tuple((sum(_VEC_WIDTHS[:n]), sum(_VEC_WIDTHS[:n + 1])) for n in range(len(_VEC_WIDTHS)))


def _pack_vectors(norm_g, final_norm_g, d, b_glu, onorm_g):
    parts = [norm_g, final_norm_g, d.reshape(-1), b_glu, onorm_g]
    assert tuple(p.shape[0] for p in parts) == _VEC_WIDTHS
    return jnp.concatenate([p.astype(F32) for p in parts]).reshape(1, sum(_VEC_WIDTHS))


def _time_major_perm(NB):
    steps = ROW_BLOCK // NB
    dst = np.arange(ROW_BLOCK)
    src = (dst % NB) * steps + dst // NB
    p = np.zeros((ROW_BLOCK, ROW_BLOCK), np.float32)
    p[dst, src] = 1.0
    return p


def _mixer_call(x, state, weights, *, NB, C, t_valid, name):
    B, L, _ = x.shape
    has_state = state is not None
    assert not has_state or L == C, "a given state is consumed by a single chunk per sequence"
    grid = (B // NB, L // C)
    R = NB * C
    perm = _time_major_perm(NB)
    weights = (jnp.asarray(perm, BF16), jnp.asarray(perm.T, BF16)) + tuple(weights)

    def full(a):
        nd = a.ndim
        return pl.BlockSpec(a.shape, lambda i, j, nd=nd: (0,) * nd, pipeline_mode=pl.Buffered(1))

    flat_rows = C % SUBLANES != 0
    if flat_rows:
        assert L == C
        x = x.reshape(B * L, D_MODEL)
        x_spec = pl.BlockSpec((R, D_MODEL), lambda i, j: (i, 0))
    else:
        x_spec = pl.BlockSpec((NB, C, D_MODEL), lambda i, j: (i, j, 0))
    h_spec = pl.BlockSpec((NB, S5_LANES), lambda i, j: (i, 0))
    s_spec = pl.BlockSpec((NB, HG_HEADS, HG_DK, HG_DV), lambda i, j: (i, 0, 0, 0))
    in_specs = [x_spec] + ([h_spec, h_spec, s_spec] if has_state else []) + [full(w) for w in weights]
    args = [x] + (list(state) if has_state else []) + list(weights)
    out_shape = [
        jax.ShapeDtypeStruct(x.shape, x.dtype),
        jax.ShapeDtypeStruct((B, S5_LANES), F32),
        jax.ShapeDtypeStruct((B, S5_LANES), F32),
        jax.ShapeDtypeStruct((B, HG_HEADS, HG_DK, HG_DV), F32),
    ]
    chunk_dt = BF16 if C % BF16_TILE_ROWS == 0 else F32
    scratch = [
        pltpu.VMEM((R, D_MODEL), BF16),
        pltpu.VMEM((R, HG_COLS), F32),
        pltpu.VMEM((R, 2 * W_S5), F32),
        pltpu.VMEM((R, 2 * S5_LANES), F32),
        pltpu.VMEM((R, W_S5), BF16),
        pltpu.VMEM((R, W_HG), F32),
        pltpu.VMEM((R, W_HG), chunk_dt),
        pltpu.VMEM((R, W_HG), chunk_dt),
        pltpu.VMEM((R, W_HG), chunk_dt),
        pltpu.VMEM((HG_HEADS, ROW_BLOCK, HG_DK), F32),
        pltpu.VMEM((ROW_BLOCK, ROW_BLOCK), jnp.int32),
        pltpu.VMEM((ROW_BLOCK, ROW_BLOCK), BF16),
        pltpu.VMEM((NB, SUBLANES, W_HG), F32),
        pltpu.VMEM((R, D_MODEL), BF16),
    ]
    return pl.pallas_call(
        functools.partial(_mixer_kernel, NB=NB, C=C, t_valid=t_valid, has_state=has_state),
        grid=grid,
        in_specs=in_specs,
        out_specs=[x_spec, h_spec, h_spec, s_spec],
        out_shape=out_shape,
        scratch_shapes=scratch,
        compiler_params=pltpu.CompilerParams(
            dimension_semantics=("arbitrary", "arbitrary"),
            vmem_limit_bytes=V7X_VMEM_LIMIT),
        name=name,
    )(*args)


def kernel(x_prompt, x_sample, state_s5_re, state_s5_im, state_hgrn, norm_g, w_in, s5_lambda_re, s5_lambda_im,
           s5_log_step, s5_b_re, s5_b_im, s5_c_re, s5_c_im, s5_d, w_glu, b_glu, hgrn_lb_logits, hgrn_onorm_g,
           w_out, final_norm_g):
    assert norm_g.shape[0] == 1, "single-layer model"
    l = 0
    a_re, a_im, bbre, bbim, lb = _prepare_params(
        s5_lambda_re[l], s5_lambda_im[l], s5_log_step[l], s5_b_re[l], s5_b_im[l], hgrn_lb_logits)
    bcat, ccat = _block_diag_weights(bbre, bbim, s5_c_re[l], s5_c_im[l])
    weights = (
        _pack_vectors(norm_g[l], final_norm_g, s5_d[l], b_glu[l], hgrn_onorm_g[l]),
        w_in[l].astype(BF16),
        a_re, a_im, bcat, ccat,
        w_glu[l].astype(BF16),
        lb,
        w_out[l].astype(BF16),
    )

    Bp, Lp, _ = x_prompt.shape
    yp, p_re, p_im, p_hg = _mixer_call(x_prompt, None, weights, NB=Bp, C=64, t_valid=64, name="mixer_prompt")

    Bs, Ls, _ = x_sample.shape
    state = (state_s5_re[l].reshape(Bs, S5_LANES).astype(F32),
             state_s5_im[l].reshape(Bs, S5_LANES).astype(F32),
             state_hgrn[l].astype(F32))
    ys, s_re, s_im, s_hg = _mixer_call(x_sample, state, weights, NB=ROW_BLOCK // Ls, C=Ls, t_valid=Ls,
                                       name="mixer_sample")
    ys = ys.reshape(Bs, Ls, D_MODEL)

    sd = state_s5_re.dtype
    shp = (1, Bp, S5_GROUPS, S5_STATE)
    shs = (1, Bs, S5_GROUPS, S5_STATE)
    return (yp, ys,
            p_re.reshape(shp).astype(sd), p_im.reshape(shp).astype(sd), p_hg[None].astype(state_hgrn.dtype),
            s_re.reshape(shs).astype(sd), s_im.reshape(shs).astype(sd), s_hg[None].astype(state_hgrn.dtype))
```

```python
import functools
import math

import jax
import jax.numpy as jnp
import numpy as np
from jax import lax
from jax.experimental import pallas as pl
from jax.experimental.pallas import tpu as pltpu

F32 = jnp.float32
BF16 = jnp.bfloat16

D_MODEL = 1024
W_S5 = 512
W_HG = 512
S5_CH = 16
S5_GROUPS = 32
S5_STATE = 64
S5_LANES = S5_GROUPS * S5_STATE
S5_HALF = S5_LANES // 2
HG_DK = 128
HG_HEADS = 4
HG_DV = 128
EPS = 1e-6
LAMBDA_RE_MAX = -1e-4
PROJ_OUT = 3072
OFF_Q = 1024
HG_COLS = PROJ_OUT - OFF_Q
SECTION = 512
COL_Q, COL_F, COL_I, COL_ZH = (SECTION * n for n in range(4))
PROMPT_CHUNK = 64
LOG2E = 1.4426950408889634

LANES = 128
SUBLANES = 8
BF16_TILE_ROWS = 16
ROW_BLOCK = 128
MATMUL_ROWS = 512
READOUT_ROWS = 512
S5_INPUT_ROWS = 512
V7X_VMEM_LIMIT = 60 * 1024 * 1024


def _prep_kernel(lre_ref, lim_ref, ls_ref, brt_ref, bit_ref, lg_ref,
                 are_ref, aim_ref, bbre_ref, bbim_ref, lb_ref):
    lr = jnp.minimum(lre_ref[...], LAMBDA_RE_MAX)
    li = lim_ref[...]
    dt = jnp.exp(ls_ref[...])
    er = jnp.exp(lr * dt)
    a_re = er * jnp.cos(li * dt)
    a_im = er * jnp.sin(li * dt)
    den = lr * lr + li * li
    zr = ((a_re - 1.0) * lr + a_im * li) / den
    zi = (a_im * lr - (a_re - 1.0) * li) / den
    for g in range(S5_GROUPS):
        cols = slice(S5_STATE * g, S5_STATE * (g + 1))
        are_ref[:, cols] = a_re[g:g + 1, :]
        aim_ref[:, cols] = a_im[g:g + 1, :]
        br = brt_ref[g]
        bi = bit_ref[g]
        zr_g = zr[g:g + 1, :]
        zi_g = zi[g:g + 1, :]
        bbre_ref[g] = zr_g * br - zi_g * bi
        bbim_ref[g] = zr_g * bi + zi_g * br
    lg = lg_ref[...]
    e = jnp.exp(lg - jnp.max(lg, axis=0, keepdims=True))
    lb_ref[...] = e[0:1, :] / jnp.sum(e, axis=0, keepdims=True)


def _prepare_params(lam_re, lam_im, log_step, b_re, b_im, lb_logits):
    gps = jax.ShapeDtypeStruct((1, S5_LANES), F32)
    gcs = jax.ShapeDtypeStruct((S5_GROUPS, S5_CH, S5_STATE), F32)
    return pl.pallas_call(
        _prep_kernel,
        out_shape=[gps, gps, gcs, gcs, jax.ShapeDtypeStruct((1, HG_HEADS * HG_DK), F32)],
        name="s5_hgrn_prep",
    )(lam_re.astype(F32), lam_im.astype(F32), log_step.astype(F32).reshape(S5_GROUPS, 1),
      jnp.transpose(b_re.astype(F32), (0, 2, 1)), jnp.transpose(b_im.astype(F32), (0, 2, 1)),
      lb_logits.astype(F32))


def _block_diag_weights(bbre, bbim, c_re, c_im):
    hg = S5_GROUPS // 2
    eye = jnp.eye(hg, dtype=BF16)[None, :, None, :, None]

    def in_half(bbt):
        x = bbt.astype(BF16).reshape(2, hg, S5_CH, 1, S5_STATE) * eye
        return x.reshape(2, hg * S5_CH, hg * S5_STATE)

    def out_half(c):
        ct = jnp.transpose(c.astype(BF16), (0, 2, 1)).reshape(2, hg, S5_STATE, 1, S5_CH)
        return (ct * eye).reshape(2, hg * S5_STATE, hg * S5_CH)

    bcat = jnp.concatenate([in_half(bbre), in_half(bbim)], axis=2)
    ccat = jnp.concatenate([out_half(c_re), out_half(-c_im)], axis=1)
    return bcat, ccat


def _sigmoid(x):
    return 1.0 / (1.0 + jnp.exp2(x * (-LOG2E)))


def _gelu_tanh(x):
    return 0.5 * x * (1.0 + jnp.tanh(math.sqrt(2.0 / math.pi) * (x + 0.044715 * (x * x * x))))


def _dot(a, b):
    return jnp.dot(a, b, preferred_element_type=F32)


def _dot_nt(a, b):
    return lax.dot_general(a, b, (((1,), (1,)), ((), ())), preferred_element_type=F32)


def _dot_tn(a, b):
    return lax.dot_general(a, b, (((0,), (0,)), ((), ())), preferred_element_type=F32)


def _row_bcast_blocks(src_ref, blk, off):
    parts = []
    for n in range(ROW_BLOCK // blk):
        row = src_ref[n * blk + off:n * blk + off + 1, :]
        parts.append(jnp.broadcast_to(row, (blk, HG_DK)))
    return parts[0] if len(parts) == 1 else jnp.concatenate(parts, axis=0)


def _mixer_kernel(*refs, NB, C, has_state):
    if has_state:
        x_ref, h0r_ref, h0i_ref, s0_ref = refs[:4]
        refs = refs[4:]
    else:
        x_ref = refs[0]
        refs = refs[1:]
    (perm_ref, permt_ref, vec_ref, win_ref, are_ref, aim_ref, bcat_ref, ccat_ref, wglu_ref, lb_ref, wout_ref,
     y_ref, hr_ref, hi_ref, s_ref,
     hn_ref, proj_ref, uz_ref, bu_ref, s5_ref, o_ref, qe_ref, ke_ref, v_ref, bsall_ref, lev_ref, tri_ref,
     dl_ref, cat_ref) = refs

    ng_ref, fg_ref, d_ref, bglu_ref, og_ref = (vec_ref.at[:, a:b] for a, b in _VEC_SLICES)

    j = pl.program_id(1)
    nj = pl.num_programs(1)
    R = NB * C
    n_rb = R // ROW_BLOCK
    nbat = ROW_BLOCK // C

    @pl.when(j == 0)
    def _init():
        if has_state:
            hr_ref[...] = h0r_ref[...]
            hi_ref[...] = h0i_ref[...]
        else:
            hr_ref[...] = jnp.zeros(hr_ref.shape, F32)
            hi_ref[...] = jnp.zeros(hi_ref.shape, F32)
            s_ref[...] = jnp.zeros(s_ref.shape, F32)

    MM = min(R, MATMUL_ROWS)
    n_mb = R // MM
    mbat = MM // C

    def aligned(start, n):
        return start if isinstance(start, int) else pl.multiple_of(start, n)

    def rows_of(rb, n=MM):
        return pl.ds(aligned(rb * n, n), n)

    def loop(trips, body):
        if trips == 1:
            body(0, 0)
        else:
            lax.fori_loop(0, trips, body, 0)

    TB = ROW_BLOCK // NB

    def seq_major_block(src_ref, blk):
        assert isinstance(blk, int)
        if TB == C:
            return src_ref[blk * ROW_BLOCK:(blk + 1) * ROW_BLOCK, :]
        return jnp.concatenate([src_ref[b * C + blk * TB:b * C + (blk + 1) * TB, :] for b in range(NB)], axis=0)

    flat_rows = len(x_ref.shape) == 2

    def x_rows(mb):
        if flat_rows:
            return x_ref[rows_of(mb), :]
        return x_ref[pl.ds(mb * mbat, mbat)].reshape(MM, D_MODEL)

    def prenorm(x):
        return (x * lax.rsqrt(jnp.mean(x * x, axis=-1, keepdims=True) + EPS) * ng_ref[...]).astype(BF16)

    def hg_proj(hn, sec):
        return _dot(hn, win_ref[:, OFF_Q + SECTION * sec:OFF_Q + SECTION * (sec + 1)])

    def project_chunk():
        def body(rb, c):
            rows = rows_of(rb)
            hn = prenorm(x_rows(rb))
            hn_ref[rows, :] = hn
            for sec in range(HG_COLS // SECTION):
                proj_ref[rows, SECTION * sec:SECTION * (sec + 1)] = hg_proj(hn, sec)
            return c
        loop(n_mb, body)

    row = lax.broadcasted_iota(jnp.int32, (ROW_BLOCK, ROW_BLOCK), 0)
    col = lax.broadcasted_iota(jnp.int32, (ROW_BLOCK, ROW_BLOCK), 1)
    tin = row & (C - 1)
    levels = []
    m = 1
    while m < C:
        levels.append(m)
        m *= 2

    @pl.when(j == 0)
    def _masks():
        lev = jnp.where(row == col, 0, -1)
        for li, m in enumerate(levels):
            same = (row & ~(2 * m - 1)) == (col & ~(2 * m - 1))
            own = same & ((row & m) != 0) & ((col & m) == 0)
            lev = jnp.where(own, li + 1, lev)
        lev_ref[...] = lev
        in_chunk = (row & ~(C - 1)) == (col & ~(C - 1))
        tri_ref[...] = jnp.where(in_chunk & (col <= row), 1.0, 0.0).astype(BF16)

    def step_body():
        heads = range(HG_HEADS)
        hcs = [slice(HG_DK * hd, HG_DK * (hd + 1)) for hd in heads]
        scan_width = min(S5_HALF, (SUBLANES * S5_HALF) // NB)

        def hgrn_block(rb, mxu_fill):
            rows = rows_of(rb, ROW_BLOCK)
            lv = lev_ref[...]
            qs, ks, fs, vs, gcats = [], [], [], [], []
            for hd in heads:
                lb = lb_ref[:, hcs[hd]]
                q = proj_ref[rows, COL_Q + HG_DK * hd:COL_Q + HG_DK * (hd + 1)]
                gf = proj_ref[rows, COL_F + HG_DK * hd:COL_F + HG_DK * (hd + 1)]
                v = proj_ref[rows, COL_I + HG_DV * hd:COL_I + HG_DV * (hd + 1)].astype(BF16)
                f = lb + (1.0 - lb) * _sigmoid(gf)
                g = jnp.log2(f)
                k = 1.0 - f
                g1 = g.astype(BF16)
                r1 = g - g1.astype(F32)
                g2 = r1.astype(BF16)
                g3 = (r1 - g2.astype(F32)).astype(BF16)
                p4 = row & 3
                dec1 = jnp.where((row & 1) == 1, f, 1.0)
                dec2 = jnp.where(p4 == 0, pltpu.roll(f, ROW_BLOCK - 1, axis=0),
                                 jnp.where(p4 == 1, 1.0,
                                           jnp.where(p4 == 2, f, f * pltpu.roll(f, 1, axis=0))))
                qs.append(q)
                ks.append(k)
                fs.append((dec1.astype(BF16), dec2.astype(BF16)))
                vs.append(v)
                gcats.append(jnp.concatenate([g1, g2, g3], axis=1))
            bparts = [_dot(tri_ref[...], gcats[hd]) for hd in heads]
            bs = []
            for hd in heads:
                bp = bparts[hd]
                b = (bp[:, 0:HG_DK] + bp[:, HG_DK:2 * HG_DK]) + bp[:, 2 * HG_DK:3 * HG_DK]
                bsall_ref[hd] = b
                bs.append(b)
            qbs = [q.astype(BF16) for q in qs]
            kbs = [k.astype(BF16) for k in ks]
            atts = [jnp.where(lv == 0, _dot_nt(qbs[hd], kbs[hd]), 0.0) for hd in heads]
            for li, m in enumerate(levels):
                two_m = 2 * m
                p2 = row & (two_m - 1)
                if m > 2:
                    sgn = jnp.where(p2 >= m, 1.0, -1.0)
                for hd in heads:
                    if m <= 2:
                        decb = fs[hd][li]
                    else:
                        ref = _row_bcast_blocks(bsall_ref.at[hd], two_m, m - 1)
                        decb = jnp.exp2((bs[hd] - ref) * sgn).astype(BF16)
                    atts[hd] = jnp.where(lv == li + 1, _dot_nt(qbs[hd] * decb, kbs[hd] * decb), atts[hd])
                if li == 0:
                    for piece in mxu_fill:
                        piece()
            for hd in heads:
                o_ref[rows, hcs[hd]] = _dot(atts[hd].astype(BF16), vs[hd])
            for hd in heads:
                hc = hcs[hd]
                bs_ref = bsall_ref.at[hd]
                b = bs[hd]
                if C >= SUBLANES:
                    b_last = _row_bcast_blocks(bs_ref, C, C - 1)
                else:
                    b_last = b
                    for d in range(1, C):
                        b_last = jnp.where(tin == C - 1 - d, pltpu.roll(b, ROW_BLOCK - d, axis=0), b_last)
                qe_ref[rows, hc] = (qs[hd] * jnp.exp2(b)).astype(qe_ref.dtype)
                ke_ref[rows, hc] = (ks[hd] * jnp.exp2(b_last - b)).astype(ke_ref.dtype)
                v_ref[rows, hc] = vs[hd].astype(v_ref.dtype)
                for n in range(nbat):
                    last = bs_ref[n * C + C - 1:n * C + C, :]
                    dl_ref[rb * nbat + n, :, hc] = jnp.broadcast_to(jnp.exp2(last), (SUBLANES, HG_DK))

        SR = min(R, S5_INPUT_ROWS)
        sblocks = SR // ROW_BLOCK

        def merged(it, c):
            rows = rows_of(it, SR)
            hp = jnp.concatenate(
                [_dot(perm_ref[...], seq_major_block(hn_ref, it * sblocks + sb)).astype(BF16)
                 for sb in range(sblocks)], axis=0)
            uz = {}

            def uz_piece(sec):
                uz[sec] = _dot(hp, win_ref[:, W_S5 * sec:W_S5 * (sec + 1)])
                uz_ref[rows, W_S5 * sec:W_S5 * (sec + 1)] = uz[sec]

            def bu_piece(i):
                u_i = uz[0][:, W_S5 // 2 * i:W_S5 // 2 * (i + 1)].astype(BF16)
                bu_ref[rows, 2 * S5_HALF * i:2 * S5_HALF * (i + 1)] = _dot(u_i, bcat_ref[i])

            pieces = [functools.partial(uz_piece, 0), functools.partial(uz_piece, 1),
                      functools.partial(bu_piece, 0), functools.partial(bu_piece, 1)]
            per_block = len(pieces) // sblocks
            for sb in range(sblocks):
                hgrn_block(it * sblocks + sb, pieces[sb * per_block:(sb + 1) * per_block])
            for i in range(2):
                for off in range(0, S5_HALF, scan_width):
                    st = slice(S5_HALF * i + off, S5_HALF * i + off + scan_width)
                    re_c = 2 * S5_HALF * i + off
                    im_c = re_c + S5_HALF
                    ar = are_ref[:, st]
                    ai = aim_ref[:, st]
                    hr = hr_ref[:, st]
                    hi = hi_ref[:, st]
                    for tt in range(TB * sblocks):
                        trows = pl.ds(aligned(it * SR + tt * NB, NB), NB)
                        hr, hi = (ar * hr - ai * hi + bu_ref[trows, re_c:re_c + scan_width],
                                  ar * hi + ai * hr + bu_ref[trows, im_c:im_c + scan_width])
                        bu_ref[trows, re_c:re_c + scan_width] = hr
                        bu_ref[trows, im_c:im_c + scan_width] = hi
                    hr_ref[:, st] = hr
                    hi_ref[:, st] = hi
            return c
        assert R == SR
        loop(R // SR, merged)

        def inter_out(bi):
            srows = pl.ds(aligned(bi * C, C), C)
            for hd in heads:
                hc = hcs[hd]
                if has_state:
                    oi = _dot(qe_ref[srows, hc], s0_ref[bi, hd].astype(qe_ref.dtype))
                else:
                    oi = _dot_nt(qe_ref[srows, hc], s_ref[bi, hd].astype(qe_ref.dtype))
                o_ref[srows, hc] = o_ref[srows, hc] + oi

        def inter_state(bi):
            srows = pl.ds(aligned(bi * C, C), C)
            for hd in heads:
                hc = hcs[hd]
                if has_state:
                    upd = _dot_tn(ke_ref[srows, hc], v_ref[srows, hc])
                    s_ref[bi, hd] = s0_ref[bi, hd] * dl_ref[bi, :, hc].T[:, 0:1] + upd
                else:
                    upd = _dot_tn(v_ref[srows, hc], ke_ref[srows, hc])
                    s_ref[bi, hd] = s_ref[bi, hd] * dl_ref[bi, 0:1, hc] + upd

        RR = min(R, READOUT_ROWS)
        rbat = RR // C
        merge_inter = rbat <= 8

        def readout_inter(rb, c):
            rows = rows_of(rb, RR)
            ys = []
            for i in range(2):
                hs = bu_ref[rows, 2 * S5_HALF * i:2 * S5_HALF * (i + 1)].astype(BF16)
                ys.append(_dot(hs, ccat_ref[i]))
            if merge_inter:
                for n in range(rbat):
                    inter_out(rb * rbat + n)
            u = uz_ref[rows, 0:W_S5]
            y = jnp.concatenate(ys, axis=1) + d_ref[...] * u
            g = _gelu_tanh(y)
            glu = _dot(g.astype(BF16), wglu_ref[...]) + bglu_ref[...]
            if merge_inter:
                for n in range(rbat):
                    inter_state(rb * rbat + n)
            zs = uz_ref[rows, W_S5:2 * W_S5]
            s5 = g * _sigmoid(glu) * (zs * _sigmoid(zs))
            s5_ref[rows, :] = s5.astype(BF16)
            return c
        loop(R // RR, readout_inter)

        if not merge_inter and C >= SUBLANES:
            def inter(bi, c):
                inter_out(bi)
                inter_state(bi)
                return c
            lax.fori_loop(0, NB, inter, 0, unroll=2)

        if C < SUBLANES:
            assert has_state and not merge_inter
            per_tile = SUBLANES // C
            rid = lax.broadcasted_iota(jnp.int32, (SUBLANES, HG_DK), 0)
            cid = lax.broadcasted_iota(jnp.int32, (HG_DK, SUBLANES), 1)

            def inter_tile(ti, c):
                srows = pl.ds(pl.multiple_of(ti * SUBLANES, SUBLANES), SUBLANES)
                for hd in heads:
                    hc = hcs[hd]
                    qe8 = qe_ref[srows, hc]
                    ke_t = ke_ref[srows, hc].T
                    v8 = v_ref[srows, hc]
                    acc = o_ref[srows, hc]
                    dec = dl_ref[ti * per_tile + per_tile - 1, :, hc]
                    for s in range(per_tile - 1):
                        dec = jnp.where(rid == s, dl_ref[ti * per_tile + s, :, hc], dec)
                    dec_t = dec.T
                    for s in range(per_tile):
                        bi = ti * per_tile + s
                        st = s0_ref[bi, hd]
                        rows_s = (rid >= s * C) & (rid < (s + 1) * C)
                        cols_s = (cid >= s * C) & (cid < (s + 1) * C)
                        acc = acc + _dot(jnp.where(rows_s, qe8, 0.0), st)
                        upd = _dot(jnp.where(cols_s, ke_t, 0.0), v8)
                        s_ref[bi, hd] = st * dec_t[:, s:s + 1] + upd
                    o_ref[srows, hc] = acc
                return c
            lax.fori_loop(0, NB // per_tile, inter_tile, 0, unroll=4)

        for blk in range(n_rb):
            back = _dot(permt_ref[...], s5_ref[blk * ROW_BLOCK:(blk + 1) * ROW_BLOCK, :]).astype(BF16)
            if TB == C:
                cat_ref[blk * ROW_BLOCK:(blk + 1) * ROW_BLOCK, 0:W_S5] = back
            else:
                for b in range(NB):
                    cat_ref[b * C + blk * TB:b * C + (blk + 1) * TB, 0:W_S5] = back[b * TB:(b + 1) * TB, :]

        def phase5(rb, c):
            rows = rows_of(rb)
            for hd in range(HG_HEADS):
                hc = slice(HG_DV * hd, HG_DV * (hd + 1))
                o = o_ref[rows, hc]
                o = o * lax.rsqrt(jnp.mean(o * o, axis=-1, keepdims=True) + EPS)
                zh = proj_ref[rows, COL_ZH + HG_DV * hd:COL_ZH + HG_DV * (hd + 1)]
                hg = o * og_ref[:, hc] * (zh * _sigmoid(zh))
                cat_ref[rows, W_S5 + HG_DV * hd:W_S5 + HG_DV * (hd + 1)] = hg.astype(BF16)
            out = _dot(cat_ref[rows, :], wout_ref[...])
            xn = x_rows(rb) + out
            yv = xn * lax.rsqrt(jnp.mean(xn * xn, axis=-1, keepdims=True) + EPS) * fg_ref[...]
            if flat_rows:
                y_ref[rows, :] = yv
            else:
                y_ref[pl.ds(rb * mbat, mbat)] = yv.reshape(mbat, C, D_MODEL)
            return c
        loop(n_mb, phase5)

    project_chunk()
    step_body()

    if not has_state:
        @pl.when(j == nj - 1)
        def _fin():
            def tr(b, c):
                for hd in range(HG_HEADS):
                    s_ref[b, hd] = s_ref[b, hd].T
                return c
            lax.fori_loop(0, NB, tr, 0)


_VEC_WIDTHS = (D_MODEL, D_MODEL, W_S5, W_S5, W_HG)
_VEC_SLICES = tuple((sum(_VEC_WIDTHS[:n]), sum(_VEC_WIDTHS[:n + 1])) for n in range(len(_VEC_WIDTHS)))


def _pack_vectors(norm_g, final_norm_g, d, b_glu, onorm_g):
    parts = [norm_g, final_norm_g, d.reshape(-1), b_glu, onorm_g]
    assert tuple(p.shape[0] for p in parts) == _VEC_WIDTHS
    return jnp.concatenate([p.astype(F32) for p in parts]).reshape(1, sum(_VEC_WIDTHS))


def _time_major_perm(NB):
    steps = ROW_BLOCK // NB
    dst = np.arange(ROW_BLOCK)
    src = (dst % NB) * steps + dst // NB
    p = np.zeros((ROW_BLOCK, ROW_BLOCK), np.float32)
    p[dst, src] = 1.0
    return p


def _mixer_call(x, state, weights, *, NB, C, name):
    B, L, _ = x.shape
    has_state = state is not None
    assert not has_state or L == C, "a given state is consumed by a single chunk per sequence"
    grid = (B // NB, L // C)
    R = NB * C
    perm = _time_major_perm(NB)
    weights = (jnp.asarray(perm, BF16), jnp.asarray(perm.T, BF16)) + tuple(weights)

    def full(a):
        nd = a.ndim
        return pl.BlockSpec(a.shape, lambda i, j, nd=nd: (0,) * nd, pipeline_mode=pl.Buffered(1))

    flat_rows = C % SUBLANES != 0
    if flat_rows:
        assert L == C
        x = x.reshape(B * L, D_MODEL)
        x_spec = pl.BlockSpec((R, D_MODEL), lambda i, j: (i, 0))
    else:
        x_spec = pl.BlockSpec((NB, C, D_MODEL), lambda i, j: (i, j, 0))
    h_spec = pl.BlockSpec((NB, S5_LANES), lambda i, j: (i, 0))
    s_spec = pl.BlockSpec((NB, HG_HEADS, HG_DK, HG_DV), lambda i, j: (i, 0, 0, 0))
    in_specs = [x_spec] + ([h_spec, h_spec, s_spec] if has_state else []) + [full(w) for w in weights]
    args = [x] + (list(state) if has_state else []) + list(weights)
    out_shape = [
        jax.ShapeDtypeStruct(x.shape, x.dtype),
        jax.ShapeDtypeStruct((B, S5_LANES), F32),
        jax.ShapeDtypeStruct((B, S5_LANES), F32),
        jax.ShapeDtypeStruct((B, HG_HEADS, HG_DK, HG_DV), F32),
    ]
    chunk_dt = BF16 if C % BF16_TILE_ROWS == 0 else F32
    scratch = [
        pltpu.VMEM((R, D_MODEL), BF16),
        pltpu.VMEM((R, HG_COLS), F32),
        pltpu.VMEM((R, 2 * W_S5), F32),
        pltpu.VMEM((R, 2 * S5_LANES), F32),
        pltpu.VMEM((R, W_S5), BF16),
        pltpu.VMEM((R, W_HG), F32),
        pltpu.VMEM((R, W_HG), chunk_dt),
        pltpu.VMEM((R, W_HG), chunk_dt),
        pltpu.VMEM((R, W_HG), chunk_dt),
        pltpu.VMEM((HG_HEADS, ROW_BLOCK, HG_DK), F32),
        pltpu.VMEM((ROW_BLOCK, ROW_BLOCK), jnp.int32),
        pltpu.VMEM((ROW_BLOCK, ROW_BLOCK), BF16),
        pltpu.VMEM((NB, SUBLANES, W_HG), F32),
        pltpu.VMEM((R, D_MODEL), BF16),
    ]
    return pl.pallas_call(
        functools.partial(_mixer_kernel, NB=NB, C=C, has_state=has_state),
        grid=grid,
        in_specs=in_specs,
        out_specs=[x_spec, h_spec, h_spec, s_spec],
        out_shape=out_shape,
        scratch_shapes=scratch,
        compiler_params=pltpu.CompilerParams(
            dimension_semantics=("arbitrary", "arbitrary"),
            vmem_limit_bytes=V7X_VMEM_LIMIT),
        name=name,
    )(*args)


def kernel(x_prompt, x_sample, state_s5_re, state_s5_im, state_hgrn, norm_g, w_in, s5_lambda_re, s5_lambda_im,
           s5_log_step, s5_b_re, s5_b_im, s5_c_re, s5_c_im, s5_d, w_glu, b_glu, hgrn_lb_logits, hgrn_onorm_g,
           w_out, final_norm_g):
    assert norm_g.shape[0] == 1, "single-layer model"
    l = 0
    a_re, a_im, bbre, bbim, lb = _prepare_params(
        s5_lambda_re[l], s5_lambda_im[l], s5_log_step[l], s5_b_re[l], s5_b_im[l], hgrn_lb_logits)
    bcat, ccat = _block_diag_weights(bbre, bbim, s5_c_re[l], s5_c_im[l])
    weights = (
        _pack_vectors(norm_g[l], final_norm_g, s5_d[l], b_glu[l], hgrn_onorm_g[l]),
        w_in[l].astype(BF16),
        a_re, a_im, bcat, ccat,
        w_glu[l].astype(BF16),
        lb,
        w_out[l].astype(BF16),
    )

    Bp, Lp, _ = x_prompt.shape
    yp, p_re, p_im, p_hg = _mixer_call(x_prompt, None, weights, NB=Bp, C=PROMPT_CHUNK, name="mixer_prompt")

    Bs, Ls, _ = x_sample.shape
    state = (state_s5_re[l].reshape(Bs, S5_LANES).astype(F32),
             state_s5_im[l].reshape(Bs, S5_LANES).astype(F32),
             state_hgrn[l].astype(F32))
    ys, s_re, s_im, s_hg = _mixer_call(x_sample, state, weights, NB=ROW_BLOCK // Ls, C=Ls, name="mixer_sample")
    ys = ys.reshape(Bs, Ls, D_MODEL)

    sd = state_s5_re.dtype
    shp = (1, Bp, S5_GROUPS, S5_STATE)
    shs = (1, Bs, S5_GROUPS, S5_STATE)
    return (yp, ys,
            p_re.reshape(shp).astype(sd), p_im.reshape(shp).astype(sd), p_hg[None].astype(state_hgrn.dtype),
            s_re.reshape(shs).astype(sd), s_im.reshape(shs).astype(sd), s_hg[None].astype(state_hgrn.dtype))
```

```python
import functools
import math

import jax
import jax.numpy as jnp
from jax import lax
from jax.experimental import pallas as pl
from jax.experimental.pallas import tpu as pltpu

F32 = jnp.float32
BF16 = jnp.bfloat16

D_MODEL = 1024
W_S5 = 512
W_HG = 512
S5_CH = 16
S5_GROUPS = 32
S5_STATE = 64
S5_LANES = S5_GROUPS * S5_STATE
S5_HALF = S5_LANES // 2
HG_DK = 128
HG_HEADS = 4
HG_DV = 128
EPS = 1e-6
LAMBDA_RE_MAX = -1e-4
PROJ_OUT = 3072
OFF_Q = 1024
HG_COLS = PROJ_OUT - OFF_Q
SECTION = 512
COL_Q, COL_F, COL_I, COL_ZH = (SECTION * n for n in range(4))
PROMPT_CHUNK = 64
LOG2E = 1.4426950408889634

LANES = 128
SUBLANES = 8
BF16_TILE_ROWS = 16
ROW_BLOCK = 128
MATMUL_ROWS = 512
READOUT_ROWS = 512
S5_INPUT_ROWS = 512
V7X_VMEM_LIMIT = 60 * 1024 * 1024


def _prep_kernel(lre_ref, lim_ref, ls_ref, brt_ref, bit_ref, lg_ref,
                 are_ref, aim_ref, bbre_ref, bbim_ref, lb_ref):
    lr = jnp.minimum(lre_ref[...], LAMBDA_RE_MAX)
    li = lim_ref[...]
    dt = jnp.exp(ls_ref[...])
    er = jnp.exp(lr * dt)
    a_re = er * jnp.cos(li * dt)
    a_im = er * jnp.sin(li * dt)
    den = lr * lr + li * li
    zr = ((a_re - 1.0) * lr + a_im * li) / den
    zi = (a_im * lr - (a_re - 1.0) * li) / den
    for g in range(S5_GROUPS):
        cols = slice(S5_STATE * g, S5_STATE * (g + 1))
        are_ref[:, cols] = a_re[g:g + 1, :]
        aim_ref[:, cols] = a_im[g:g + 1, :]
        br = brt_ref[g]
        bi = bit_ref[g]
        zr_g = zr[g:g + 1, :]
        zi_g = zi[g:g + 1, :]
        bbre_ref[g] = zr_g * br - zi_g * bi
        bbim_ref[g] = zr_g * bi + zi_g * br
    lg = lg_ref[...]
    e = jnp.exp(lg - jnp.max(lg, axis=0, keepdims=True))
    lb_ref[...] = e[0:1, :] / jnp.sum(e, axis=0, keepdims=True)


def _prepare_params(lam_re, lam_im, log_step, b_re, b_im, lb_logits):
    gps = jax.ShapeDtypeStruct((1, S5_LANES), F32)
    gcs = jax.ShapeDtypeStruct((S5_GROUPS, S5_CH, S5_STATE), F32)
    return pl.pallas_call(
        _prep_kernel,
        out_shape=[gps, gps, gcs, gcs, jax.ShapeDtypeStruct((1, HG_HEADS * HG_DK), F32)],
        name="s5_hgrn_prep",
    )(lam_re.astype(F32), lam_im.astype(F32), log_step.astype(F32).reshape(S5_GROUPS, 1),
      jnp.transpose(b_re.astype(F32), (0, 2, 1)), jnp.transpose(b_im.astype(F32), (0, 2, 1)),
      lb_logits.astype(F32))


def _block_diag_weights(bbre, bbim, c_re, c_im):
    hg = S5_GROUPS // 2
    eye = jnp.eye(hg, dtype=BF16)[None, :, None, :, None]

    def in_half(bbt):
        x = bbt.astype(BF16).reshape(2, hg, S5_CH, 1, S5_STATE) * eye
        return x.reshape(2, hg * S5_CH, hg * S5_STATE)

    def out_half(c):
        ct = jnp.transpose(c.astype(BF16), (0, 2, 1)).reshape(2, hg, S5_STATE, 1, S5_CH)
        return (ct * eye).reshape(2, hg * S5_STATE, hg * S5_CH)

    bcat = jnp.concatenate([in_half(bbre), in_half(bbim)], axis=2)
    ccat = jnp.concatenate([out_half(c_re), out_half(-c_im)], axis=1)
    return bcat, ccat


def _sigmoid(x):
    return 1.0 / (1.0 + jnp.exp2(x * (-LOG2E)))


def _gelu_tanh(x):
    return 0.5 * x * (1.0 + jnp.tanh(math.sqrt(2.0 / math.pi) * (x + 0.044715 * (x * x * x))))


def _dot(a, b):
    return jnp.dot(a, b, preferred_element_type=F32)


def _dot_nt(a, b):
    return lax.dot_general(a, b, (((1,), (1,)), ((), ())), preferred_element_type=F32)


def _dot_tn(a, b):
    return lax.dot_general(a, b, (((0,), (0,)), ((), ())), preferred_element_type=F32)


def _row_bcast_blocks(src_ref, blk, off):
    parts = []
    for n in range(ROW_BLOCK // blk):
        row = src_ref[n * blk + off:n * blk + off + 1, :]
        parts.append(jnp.broadcast_to(row, (blk, HG_DK)))
    return parts[0] if len(parts) == 1 else jnp.concatenate(parts, axis=0)


def _mixer_kernel(*refs, NB, C, has_state):
    if has_state:
        x_ref, h0r_ref, h0i_ref, s0_ref = refs[:4]
        refs = refs[4:]
    else:
        x_ref = refs[0]
        refs = refs[1:]
    (vec_ref, win_ref, are_ref, aim_ref, bcat_ref, ccat_ref, wglu_ref, lb_ref, wout_ref,
     y_ref, hr_ref, hi_ref, s_ref, perm_ref, permt_ref,
     hn_ref, proj_ref, uz_ref, bu_ref, s5_ref, o_ref, qe_ref, ke_ref, v_ref, bsall_ref, lev_ref, tri_ref,
     dl_ref, cat_ref) = refs

    ng_ref, fg_ref, d_ref, bglu_ref, og_ref = (vec_ref.at[:, a:b] for a, b in _VEC_SLICES)

    j = pl.program_id(1)
    nj = pl.num_programs(1)
    R = NB * C
    n_rb = R // ROW_BLOCK
    nbat = ROW_BLOCK // C

    @pl.when(j == 0)
    def _init():
        if has_state:
            hr_ref[...] = h0r_ref[...]
            hi_ref[...] = h0i_ref[...]
        else:
            hr_ref[...] = jnp.zeros(hr_ref.shape, F32)
            hi_ref[...] = jnp.zeros(hi_ref.shape, F32)
            s_ref[...] = jnp.zeros(s_ref.shape, F32)

    MM = min(R, MATMUL_ROWS)
    n_mb = R // MM
    mbat = MM // C

    def aligned(start, n):
        return start if isinstance(start, int) else pl.multiple_of(start, n)

    def rows_of(rb, n=MM):
        return pl.ds(aligned(rb * n, n), n)

    def loop(trips, body):
        if trips == 1:
            body(0, 0)
        else:
            lax.fori_loop(0, trips, body, 0)

    TB = ROW_BLOCK // NB

    def seq_major_block(src_ref, blk):
        assert isinstance(blk, int)
        if TB == C:
            return src_ref[blk * ROW_BLOCK:(blk + 1) * ROW_BLOCK, :]
        return jnp.concatenate([src_ref[b * C + blk * TB:b * C + (blk + 1) * TB, :] for b in range(NB)], axis=0)

    flat_rows = len(x_ref.shape) == 2

    def x_rows(mb):
        if flat_rows:
            return x_ref[rows_of(mb), :]
        return x_ref[pl.ds(mb * mbat, mbat)].reshape(MM, D_MODEL)

    def prenorm(x):
        return (x * lax.rsqrt(jnp.mean(x * x, axis=-1, keepdims=True) + EPS) * ng_ref[...]).astype(BF16)

    def hg_proj(hn, sec):
        return _dot(hn, win_ref[:, OFF_Q + SECTION * sec:OFF_Q + SECTION * (sec + 1)])

    def project_chunk():
        def body(rb, c):
            rows = rows_of(rb)
            hn = prenorm(x_rows(rb))
            hn_ref[rows, :] = hn
            for sec in range(HG_COLS // SECTION):
                proj_ref[rows, SECTION * sec:SECTION * (sec + 1)] = hg_proj(hn, sec)
            return c
        loop(n_mb, body)

    row = lax.broadcasted_iota(jnp.int32, (ROW_BLOCK, ROW_BLOCK), 0)
    col = lax.broadcasted_iota(jnp.int32, (ROW_BLOCK, ROW_BLOCK), 1)
    tin = row & (C - 1)
    levels = []
    m = 1
    while m < C:
        levels.append(m)
        m *= 2

    @pl.when(j == 0)
    def _masks():
        lev = jnp.where(row == col, 0, -1)
        for li, m in enumerate(levels):
            same = (row & ~(2 * m - 1)) == (col & ~(2 * m - 1))
            own = same & ((row & m) != 0) & ((col & m) == 0)
            lev = jnp.where(own, li + 1, lev)
        lev_ref[...] = lev
        in_chunk = (row & ~(C - 1)) == (col & ~(C - 1))
        tri_ref[...] = jnp.where(in_chunk & (col <= row), 1.0, 0.0).astype(BF16)
        assert NB & (NB - 1) == 0
        seq_step = lambda r: (r & (NB - 1)) * TB + (r >> (NB.bit_length() - 1))
        perm_ref[...] = jnp.where(col == seq_step(row), 1.0, 0.0).astype(BF16)
        permt_ref[...] = jnp.where(row == seq_step(col), 1.0, 0.0).astype(BF16)

    def step_body():
        heads = range(HG_HEADS)
        hcs = [slice(HG_DK * hd, HG_DK * (hd + 1)) for hd in heads]
        scan_width = min(S5_HALF, (SUBLANES * S5_HALF) // NB)

        def hgrn_block(rb, mxu_fill):
            rows = rows_of(rb, ROW_BLOCK)
            lv = lev_ref[...]
            qs, ks, fs, vs, gcats = [], [], [], [], []
            for hd in heads:
                lb = lb_ref[:, hcs[hd]]
                q = proj_ref[rows, COL_Q + HG_DK * hd:COL_Q + HG_DK * (hd + 1)]
                gf = proj_ref[rows, COL_F + HG_DK * hd:COL_F + HG_DK * (hd + 1)]
                v = proj_ref[rows, COL_I + HG_DV * hd:COL_I + HG_DV * (hd + 1)].astype(BF16)
                f = lb + (1.0 - lb) * _sigmoid(gf)
                g = jnp.log2(f)
                k = 1.0 - f
                g1 = g.astype(BF16)
                r1 = g - g1.astype(F32)
                g2 = r1.astype(BF16)
                g3 = (r1 - g2.astype(F32)).astype(BF16)
                p4 = row & 3
                dec1 = jnp.where((row & 1) == 1, f, 1.0)
                dec2 = jnp.where(p4 == 0, pltpu.roll(f, ROW_BLOCK - 1, axis=0),
                                 jnp.where(p4 == 1, 1.0,
                                           jnp.where(p4 == 2, f, f * pltpu.roll(f, 1, axis=0))))
                qs.append(q)
                ks.append(k)
                fs.append((dec1.astype(BF16), dec2.astype(BF16)))
                vs.append(v)
                gcats.append(jnp.concatenate([g1, g2, g3], axis=1))
            bparts = [_dot(tri_ref[...], gcats[hd]) for hd in heads]
            bs = []
            for hd in heads:
                bp = bparts[hd]
                b = (bp[:, 0:HG_DK] + bp[:, HG_DK:2 * HG_DK]) + bp[:, 2 * HG_DK:3 * HG_DK]
                bsall_ref[hd] = b
                bs.append(b)
            qbs = [q.astype(BF16) for q in qs]
            kbs = [k.astype(BF16) for k in ks]
            atts = [jnp.where(lv == 0, _dot_nt(qbs[hd], kbs[hd]), 0.0) for hd in heads]
            for li, m in enumerate(levels):
                two_m = 2 * m
                p2 = row & (two_m - 1)
                if m > 2:
                    sgn = jnp.where(p2 >= m, 1.0, -1.0)
                for hd in heads:
                    if m <= 2:
                        decb = fs[hd][li]
                    else:
                        ref = _row_bcast_blocks(bsall_ref.at[hd], two_m, m - 1)
                        decb = jnp.exp2((bs[hd] - ref) * sgn).astype(BF16)
                    atts[hd] = jnp.where(lv == li + 1, _dot_nt(qbs[hd] * decb, kbs[hd] * decb), atts[hd])
                if li == 0:
                    for piece in mxu_fill:
                        piece()
            for hd in heads:
                o_ref[rows, hcs[hd]] = _dot(atts[hd].astype(BF16), vs[hd])
            for hd in heads:
                hc = hcs[hd]
                bs_ref = bsall_ref.at[hd]
                b = bs[hd]
                if C >= SUBLANES:
                    b_last = _row_bcast_blocks(bs_ref, C, C - 1)
                else:
                    b_last = b
                    for d in range(1, C):
                        b_last = jnp.where(tin == C - 1 - d, pltpu.roll(b, ROW_BLOCK - d, axis=0), b_last)
                qe_ref[rows, hc] = (qs[hd] * jnp.exp2(b)).astype(qe_ref.dtype)
                ke_ref[rows, hc] = (ks[hd] * jnp.exp2(b_last - b)).astype(ke_ref.dtype)
                v_ref[rows, hc] = vs[hd].astype(v_ref.dtype)
                for n in range(nbat):
                    last = bs_ref[n * C + C - 1:n * C + C, :]
                    dl_ref[rb * nbat + n, :, hc] = jnp.broadcast_to(jnp.exp2(last), (SUBLANES, HG_DK))

        SR = min(R, S5_INPUT_ROWS)
        sblocks = SR // ROW_BLOCK

        def merged(it, c):
            rows = rows_of(it, SR)
            hp = jnp.concatenate(
                [_dot(perm_ref[...], seq_major_block(hn_ref, it * sblocks + sb)).astype(BF16)
                 for sb in range(sblocks)], axis=0)
            uz = {}

            def uz_piece(sec):
                uz[sec] = _dot(hp, win_ref[:, W_S5 * sec:W_S5 * (sec + 1)])
                uz_ref[rows, W_S5 * sec:W_S5 * (sec + 1)] = uz[sec]

            def bu_piece(i):
                u_i = uz[0][:, W_S5 // 2 * i:W_S5 // 2 * (i + 1)].astype(BF16)
                bu_ref[rows, 2 * S5_HALF * i:2 * S5_HALF * (i + 1)] = _dot(u_i, bcat_ref[i])

            pieces = [functools.partial(uz_piece, 0), functools.partial(uz_piece, 1),
                      functools.partial(bu_piece, 0), functools.partial(bu_piece, 1)]
            per_block = len(pieces) // sblocks
            for sb in range(sblocks):
                hgrn_block(it * sblocks + sb, pieces[sb * per_block:(sb + 1) * per_block])
            for i in range(2):
                for off in range(0, S5_HALF, scan_width):
                    st = slice(S5_HALF * i + off, S5_HALF * i + off + scan_width)
                    re_c = 2 * S5_HALF * i + off
                    im_c = re_c + S5_HALF
                    ar = are_ref[:, st]
                    ai = aim_ref[:, st]
                    hr = hr_ref[:, st]
                    hi = hi_ref[:, st]
                    for tt in range(TB * sblocks):
                        trows = pl.ds(aligned(it * SR + tt * NB, NB), NB)
                        hr, hi = (ar * hr - ai * hi + bu_ref[trows, re_c:re_c + scan_width],
                                  ar * hi + ai * hr + bu_ref[trows, im_c:im_c + scan_width])
                        bu_ref[trows, re_c:re_c + scan_width] = hr
                        bu_ref[trows, im_c:im_c + scan_width] = hi
                    hr_ref[:, st] = hr
                    hi_ref[:, st] = hi
            return c
        assert R == SR
        loop(R // SR, merged)

        def inter_out(bi):
            srows = pl.ds(aligned(bi * C, C), C)
            for hd in heads:
                hc = hcs[hd]
                if has_state:
                    oi = _dot(qe_ref[srows, hc], s0_ref[bi, hd].astype(qe_ref.dtype))
                else:
                    oi = _dot_nt(qe_ref[srows, hc], s_ref[bi, hd].astype(qe_ref.dtype))
                o_ref[srows, hc] = o_ref[srows, hc] + oi

        def inter_state(bi):
            srows = pl.ds(aligned(bi * C, C), C)
            for hd in heads:
                hc = hcs[hd]
                if has_state:
                    upd = _dot_tn(ke_ref[srows, hc], v_ref[srows, hc])
                    s_ref[bi, hd] = s0_ref[bi, hd] * dl_ref[bi, :, hc].T[:, 0:1] + upd
                else:
                    upd = _dot_tn(v_ref[srows, hc], ke_ref[srows, hc])
                    s_ref[bi, hd] = s_ref[bi, hd] * dl_ref[bi, 0:1, hc] + upd

        RR = min(R, READOUT_ROWS)
        rbat = RR // C
        merge_inter = rbat <= 8

        def readout_inter(rb, c):
            rows = rows_of(rb, RR)
            ys = []
            for i in range(2):
                hs = bu_ref[rows, 2 * S5_HALF * i:2 * S5_HALF * (i + 1)].astype(BF16)
                ys.append(_dot(hs, ccat_ref[i]))
            if merge_inter:
                for n in range(rbat):
                    inter_out(rb * rbat + n)
            u = uz_ref[rows, 0:W_S5]
            y = jnp.concatenate(ys, axis=1) + d_ref[...] * u
            g = _gelu_tanh(y)
            glu = _dot(g.astype(BF16), wglu_ref[...]) + bglu_ref[...]
            if merge_inter:
                for n in range(rbat):
                    inter_state(rb * rbat + n)
            zs = uz_ref[rows, W_S5:2 * W_S5]
            s5 = g * _sigmoid(glu) * (zs * _sigmoid(zs))
            s5_ref[rows, :] = s5.astype(BF16)
            return c
        loop(R // RR, readout_inter)

        if not merge_inter and C >= SUBLANES:
            def inter(bi, c):
                inter_out(bi)
                inter_state(bi)
                return c
            lax.fori_loop(0, NB, inter, 0, unroll=2)

        if C < SUBLANES:
            assert has_state and not merge_inter
            per_tile = SUBLANES // C
            rid = lax.broadcasted_iota(jnp.int32, (SUBLANES, HG_DK), 0)
            cid = lax.broadcasted_iota(jnp.int32, (HG_DK, SUBLANES), 1)

            def inter_tile(ti, c):
                srows = pl.ds(pl.multiple_of(ti * SUBLANES, SUBLANES), SUBLANES)
                for hd in heads:
                    hc = hcs[hd]
                    qe8 = qe_ref[srows, hc]
                    ke_t = ke_ref[srows, hc].T
                    v8 = v_ref[srows, hc]
                    acc = o_ref[srows, hc]
                    dec = dl_ref[ti * per_tile + per_tile - 1, :, hc]
                    for s in range(per_tile - 1):
                        dec = jnp.where(rid == s, dl_ref[ti * per_tile + s, :, hc], dec)
                    dec_t = dec.T
                    for s in range(per_tile):
                        bi = ti * per_tile + s
                        st = s0_ref[bi, hd]
                        rows_s = (rid >= s * C) & (rid < (s + 1) * C)
                        cols_s = (cid >= s * C) & (cid < (s + 1) * C)
                        acc = acc + _dot(jnp.where(rows_s, qe8, 0.0), st)
                        upd = _dot(jnp.where(cols_s, ke_t, 0.0), v8)
                        s_ref[bi, hd] = st * dec_t[:, s:s + 1] + upd
                    o_ref[srows, hc] = acc
                return c
            lax.fori_loop(0, NB // per_tile, inter_tile, 0, unroll=4)

        for blk in range(n_rb):
            back = _dot(permt_ref[...], s5_ref[blk * ROW_BLOCK:(blk + 1) * ROW_BLOCK, :]).astype(BF16)
            if TB == C:
                cat_ref[blk * ROW_BLOCK:(blk + 1) * ROW_BLOCK, 0:W_S5] = back
            else:
                for b in range(NB):
                    cat_ref[b * C + blk * TB:b * C + (blk + 1) * TB, 0:W_S5] = back[b * TB:(b + 1) * TB, :]

        def phase5(rb, c):
            rows = rows_of(rb)
            for hd in range(HG_HEADS):
                hc = slice(HG_DV * hd, HG_DV * (hd + 1))
                o = o_ref[rows, hc]
                o = o * lax.rsqrt(jnp.mean(o * o, axis=-1, keepdims=True) + EPS)
                zh = proj_ref[rows, COL_ZH + HG_DV * hd:COL_ZH + HG_DV * (hd + 1)]
                hg = o * og_ref[:, hc] * (zh * _sigmoid(zh))
                cat_ref[rows, W_S5 + HG_DV * hd:W_S5 + HG_DV * (hd + 1)] = hg.astype(BF16)
            out = _dot(cat_ref[rows, :], wout_ref[...])
            xn = x_rows(rb) + out
            yv = xn * lax.rsqrt(jnp.mean(xn * xn, axis=-1, keepdims=True) + EPS) * fg_ref[...]
            if flat_rows:
                y_ref[rows, :] = yv
            else:
                y_ref[pl.ds(rb * mbat, mbat)] = yv.reshape(mbat, C, D_MODEL)
            return c
        loop(n_mb, phase5)

    project_chunk()
    step_body()

    if not has_state:
        @pl.when(j == nj - 1)
        def _fin():
            def tr(b, c):
                for hd in range(HG_HEADS):
                    s_ref[b, hd] = s_ref[b, hd].T
                return c
            lax.fori_loop(0, NB, tr, 0)


_VEC_WIDTHS = (D_MODEL, D_MODEL, W_S5, W_S5, W_HG)
_VEC_SLICES = tuple((sum(_VEC_WIDTHS[:n]), sum(_VEC_WIDTHS[:n + 1])) for n in range(len(_VEC_WIDTHS)))


def _pack_vectors(norm_g, final_norm_g, d, b_glu, onorm_g):
    parts = [norm_g, final_norm_g, d.reshape(-1), b_glu, onorm_g]
    assert tuple(p.shape[0] for p in parts) == _VEC_WIDTHS
    return jnp.concatenate([p.astype(F32) for p in parts]).reshape(1, sum(_VEC_WIDTHS))


def _mixer_call(x, state, weights, *, NB, C, name):
    B, L, _ = x.shape
    has_state = state is not None
    assert not has_state or L == C, "a given state is consumed by a single chunk per sequence"
    grid = (B // NB, L // C)
    R = NB * C

    def full(a):
        nd = a.ndim
        return pl.BlockSpec(a.shape, lambda i, j, nd=nd: (0,) * nd, pipeline_mode=pl.Buffered(1))

    flat_rows = C % SUBLANES != 0
    if flat_rows:
        assert L == C
        x = x.reshape(B * L, D_MODEL)
        x_spec = pl.BlockSpec((R, D_MODEL), lambda i, j: (i, 0))
    else:
        x_spec = pl.BlockSpec((NB, C, D_MODEL), lambda i, j: (i, j, 0))
    h_spec = pl.BlockSpec((NB, S5_LANES), lambda i, j: (i, 0))
    s_spec = pl.BlockSpec((NB, HG_HEADS, HG_DK, HG_DV), lambda i, j: (i, 0, 0, 0))
    in_specs = [x_spec] + ([h_spec, h_spec, s_spec] if has_state else []) + [full(w) for w in weights]
    args = [x] + (list(state) if has_state else []) + list(weights)
    out_shape = [
        jax.ShapeDtypeStruct(x.shape, x.dtype),
        jax.ShapeDtypeStruct((B, S5_LANES), F32),
        jax.ShapeDtypeStruct((B, S5_LANES), F32),
        jax.ShapeDtypeStruct((B, HG_HEADS, HG_DK, HG_DV), F32),
    ]
    chunk_dt = BF16 if C % BF16_TILE_ROWS == 0 else F32
    scratch = [
        pltpu.VMEM((ROW_BLOCK, ROW_BLOCK), BF16),
        pltpu.VMEM((ROW_BLOCK, ROW_BLOCK), BF16),
        pltpu.VMEM((R, D_MODEL), BF16),
        pltpu.VMEM((R, HG_COLS), F32),
        pltpu.VMEM((R, 2 * W_S5), F32),
        pltpu.VMEM((R, 2 * S5_LANES), F32),
        pltpu.VMEM((R, W_S5), BF16),
        pltpu.VMEM((R, W_HG), F32),
        pltpu.VMEM((R, W_HG), chunk_dt),
        pltpu.VMEM((R, W_HG), chunk_dt),
        pltpu.VMEM((R, W_HG), chunk_dt),
        pltpu.VMEM((HG_HEADS, ROW_BLOCK, HG_DK), F32),
        pltpu.VMEM((ROW_BLOCK, ROW_BLOCK), jnp.int32),
        pltpu.VMEM((ROW_BLOCK, ROW_BLOCK), BF16),
        pltpu.VMEM((NB, SUBLANES, W_HG), F32),
        pltpu.VMEM((R, D_MODEL), BF16),
    ]
    return pl.pallas_call(
        functools.partial(_mixer_kernel, NB=NB, C=C, has_state=has_state),
        grid=grid,
        in_specs=in_specs,
        out_specs=[x_spec, h_spec, h_spec, s_spec],
        out_shape=out_shape,
        scratch_shapes=scratch,
        compiler_params=pltpu.CompilerParams(
            dimension_semantics=("arbitrary", "arbitrary"),
            vmem_limit_bytes=V7X_VMEM_LIMIT),
        name=name,
    )(*args)


def kernel(x_prompt, x_sample, state_s5_re, state_s5_im, state_hgrn, norm_g, w_in, s5_lambda_re, s5_lambda_im,
           s5_log_step, s5_b_re, s5_b_im, s5_c_re, s5_c_im, s5_d, w_glu, b_glu, hgrn_lb_logits, hgrn_onorm_g,
           w_out, final_norm_g):
    assert norm_g.shape[0] == 1, "single-layer model"
    l = 0
    a_re, a_im, bbre, bbim, lb = _prepare_params(
        s5_lambda_re[l], s5_lambda_im[l], s5_log_step[l], s5_b_re[l], s5_b_im[l], hgrn_lb_logits)
    bcat, ccat = _block_diag_weights(bbre, bbim, s5_c_re[l], s5_c_im[l])
    weights = (
        _pack_vectors(norm_g[l], final_norm_g, s5_d[l], b_glu[l], hgrn_onorm_g[l]),
        w_in[l].astype(BF16),
        a_re, a_im, bcat, ccat,
        w_glu[l].astype(BF16),
        lb,
        w_out[l].astype(BF16),
    )

    Bp, Lp, _ = x_prompt.shape
    yp, p_re, p_im, p_hg = _mixer_call(x_prompt, None, weights, NB=Bp, C=PROMPT_CHUNK, name="mixer_prompt")

    Bs, Ls, _ = x_sample.shape
    state = (state_s5_re[l].reshape(Bs, S5_LANES).astype(F32),
             state_s5_im[l].reshape(Bs, S5_LANES).astype(F32),
             state_hgrn[l].astype(F32))
    ys, s_re, s_im, s_hg = _mixer_call(x_sample, state, weights, NB=ROW_BLOCK // Ls, C=Ls, name="mixer_sample")
    ys = ys.reshape(Bs, Ls, D_MODEL)

    sd = state_s5_re.dtype
    shp = (1, Bp, S5_GROUPS, S5_STATE)
    shs = (1, Bs, S5_GROUPS, S5_STATE)
    return (yp, ys,
            p_re.reshape(shp).astype(sd), p_im.reshape(shp).astype(sd), p_hg[None].astype(state_hgrn.dtype),
            s_re.reshape(shs).astype(sd), s_im.reshape(shs).astype(sd), s_hg[None].astype(state_hgrn.dtype))
```

```python
import functools
import math

import jax
import jax.numpy as jnp
import numpy as np
from jax import lax
from jax.experimental import pallas as pl
from jax.experimental.pallas import tpu as pltpu

F32 = jnp.float32
BF16 = jnp.bfloat16

D_MODEL = 1024
W_S5 = 512
W_HG = 512
S5_CH = 16
S5_GROUPS = 32
S5_STATE = 64
S5_LANES = S5_GROUPS * S5_STATE
S5_HALF = S5_LANES // 2
HG_DK = 128
HG_HEADS = 4
HG_DV = 128
EPS = 1e-6
LAMBDA_RE_MAX = -1e-4
PROJ_OUT = 3072
OFF_Q = 1024
HG_COLS = PROJ_OUT - OFF_Q
SECTION = 512
COL_Q, COL_F, COL_I, COL_ZH = (SECTION * n for n in range(4))
PROMPT_CHUNK = 64
LOG2E = 1.4426950408889634

LANES = 128
SUBLANES = 8
BF16_TILE_ROWS = 16
ROW_BLOCK = 128
MATMUL_ROWS = 256
READOUT_ROWS = 512
S5_INPUT_ROWS = 512
V7X_VMEM_LIMIT = 60 * 1024 * 1024


def _prep_kernel(lre_ref, lim_ref, ls_ref, brt_ref, bit_ref, lg_ref,
                 are_ref, aim_ref, bbre_ref, bbim_ref, lb_ref):
    lr = jnp.minimum(lre_ref[...], LAMBDA_RE_MAX)
    li = lim_ref[...]
    dt = jnp.exp(ls_ref[...])
    er = jnp.exp(lr * dt)
    a_re = er * jnp.cos(li * dt)
    a_im = er * jnp.sin(li * dt)
    den = lr * lr + li * li
    zr = ((a_re - 1.0) * lr + a_im * li) / den
    zi = (a_im * lr - (a_re - 1.0) * li) / den
    for g in range(S5_GROUPS):
        cols = slice(S5_STATE * g, S5_STATE * (g + 1))
        are_ref[:, cols] = a_re[g:g + 1, :]
        aim_ref[:, cols] = a_im[g:g + 1, :]
        br = brt_ref[g]
        bi = bit_ref[g]
        zr_g = zr[g:g + 1, :]
        zi_g = zi[g:g + 1, :]
        bbre_ref[g] = zr_g * br - zi_g * bi
        bbim_ref[g] = zr_g * bi + zi_g * br
    lg = lg_ref[...]
    e = jnp.exp(lg - jnp.max(lg, axis=0, keepdims=True))
    lb_ref[...] = e[0:1, :] / jnp.sum(e, axis=0, keepdims=True)


def _prepare_params(lam_re, lam_im, log_step, b_re, b_im, lb_logits):
    gps = jax.ShapeDtypeStruct((1, S5_LANES), F32)
    gcs = jax.ShapeDtypeStruct((S5_GROUPS, S5_CH, S5_STATE), F32)
    return pl.pallas_call(
        _prep_kernel,
        out_shape=[gps, gps, gcs, gcs, jax.ShapeDtypeStruct((1, HG_HEADS * HG_DK), F32)],
        name="s5_hgrn_prep",
    )(lam_re.astype(F32), lam_im.astype(F32), log_step.astype(F32).reshape(S5_GROUPS, 1),
      jnp.transpose(b_re.astype(F32), (0, 2, 1)), jnp.transpose(b_im.astype(F32), (0, 2, 1)),
      lb_logits.astype(F32))


def _block_diag_weights(bbre, bbim, c_re, c_im):
    hg = S5_GROUPS // 2
    eye = jnp.eye(hg, dtype=BF16)[None, :, None, :, None]

    def in_half(bbt):
        x = bbt.astype(BF16).reshape(2, hg, S5_CH, 1, S5_STATE) * eye
        return x.reshape(2, hg * S5_CH, hg * S5_STATE)

    def out_half(c):
        ct = jnp.transpose(c.astype(BF16), (0, 2, 1)).reshape(2, hg, S5_STATE, 1, S5_CH)
        return (ct * eye).reshape(2, hg * S5_STATE, hg * S5_CH)

    bcat = jnp.concatenate([in_half(bbre), in_half(bbim)], axis=2)
    ccat = jnp.concatenate([out_half(c_re), out_half(-c_im)], axis=1)
    return bcat, ccat


def _sigmoid(x):
    return 1.0 / (1.0 + jnp.exp2(x * (-LOG2E)))


def _gelu_tanh(x):
    return 0.5 * x * (1.0 + jnp.tanh(math.sqrt(2.0 / math.pi) * (x + 0.044715 * (x * x * x))))


def _dot(a, b):
    return jnp.dot(a, b, preferred_element_type=F32)


def _dot_nt(a, b):
    return lax.dot_general(a, b, (((1,), (1,)), ((), ())), preferred_element_type=F32)


def _dot_tn(a, b):
    return lax.dot_general(a, b, (((0,), (0,)), ((), ())), preferred_element_type=F32)


def _row_bcast_blocks(src_ref, blk, off):
    parts = []
    for n in range(ROW_BLOCK // blk):
        row = src_ref[n * blk + off:n * blk + off + 1, :]
        parts.append(jnp.broadcast_to(row, (blk, HG_DK)))
    return parts[0] if len(parts) == 1 else jnp.concatenate(parts, axis=0)


def _mixer_kernel(*refs, NB, C, has_state):
    if has_state:
        x_ref, h0r_ref, h0i_ref, s0_ref = refs[:4]
        refs = refs[4:]
    else:
        x_ref = refs[0]
        refs = refs[1:]
    (perm_ref, permt_ref, vec_ref, win_ref, are_ref, aim_ref, bcat_ref, ccat_ref, wglu_ref, lb_ref, wout_ref,
     y_ref, hr_ref, hi_ref, s_ref,
     hn_ref, proj_ref, uz_ref, bu_ref, s5_ref, o_ref, qe_ref, ke_ref, v_ref, bsall_ref, lev_ref, tri_ref,
     dl_ref, cat_ref) = refs

    ng_ref, fg_ref, d_ref, bglu_ref, og_ref = (vec_ref.at[:, a:b] for a, b in _VEC_SLICES)

    j = pl.program_id(1)
    nj = pl.num_programs(1)
    R = NB * C
    n_rb = R // ROW_BLOCK
    nbat = ROW_BLOCK // C

    @pl.when(j == 0)
    def _init():
        if has_state:
            hr_ref[...] = h0r_ref[...]
            hi_ref[...] = h0i_ref[...]
        else:
            hr_ref[...] = jnp.zeros(hr_ref.shape, F32)
            hi_ref[...] = jnp.zeros(hi_ref.shape, F32)
            s_ref[...] = jnp.zeros(s_ref.shape, F32)

    MM = min(R, MATMUL_ROWS)
    n_mb = R // MM
    mbat = MM // C

    def aligned(start, n):
        return start if isinstance(start, int) else pl.multiple_of(start, n)

    def rows_of(rb, n=MM):
        return pl.ds(aligned(rb * n, n), n)

    def loop(trips, body):
        if trips == 1:
            body(0, 0)
        else:
            lax.fori_loop(0, trips, body, 0)

    TB = ROW_BLOCK // NB

    def seq_major_block(src_ref, blk):
        assert isinstance(blk, int)
        if TB == C:
            return src_ref[blk * ROW_BLOCK:(blk + 1) * ROW_BLOCK, :]
        return jnp.concatenate([src_ref[b * C + blk * TB:b * C + (blk + 1) * TB, :] for b in range(NB)], axis=0)

    flat_rows = len(x_ref.shape) == 2

    def x_rows(mb):
        if flat_rows:
            return x_ref[rows_of(mb), :]
        return x_ref[pl.ds(mb * mbat, mbat)].reshape(MM, D_MODEL)

    def prenorm(x):
        return (x * lax.rsqrt(jnp.mean(x * x, axis=-1, keepdims=True) + EPS) * ng_ref[...]).astype(BF16)

    def hg_proj(hn, sec):
        return _dot(hn, win_ref[:, OFF_Q + SECTION * sec:OFF_Q + SECTION * (sec + 1)])

    def project_chunk():
        def body(rb, c):
            rows = rows_of(rb)
            hn = prenorm(x_rows(rb))
            hn_ref[rows, :] = hn
            for sec in range(HG_COLS // SECTION):
                proj_ref[rows, SECTION * sec:SECTION * (sec + 1)] = hg_proj(hn, sec)
            return c
        loop(n_mb, body)

    row = lax.broadcasted_iota(jnp.int32, (ROW_BLOCK, ROW_BLOCK), 0)
    col = lax.broadcasted_iota(jnp.int32, (ROW_BLOCK, ROW_BLOCK), 1)
    tin = row & (C - 1)
    levels = []
    m = 1
    while m < C:
        levels.append(m)
        m *= 2

    @pl.when(j == 0)
    def _masks():
        lev = jnp.where(row == col, 0, -1)
        for li, m in enumerate(levels):
            same = (row & ~(2 * m - 1)) == (col & ~(2 * m - 1))
            own = same & ((row & m) != 0) & ((col & m) == 0)
            lev = jnp.where(own, li + 1, lev)
        lev_ref[...] = lev
        in_chunk = (row & ~(C - 1)) == (col & ~(C - 1))
        tri_ref[...] = jnp.where(in_chunk & (col <= row), 1.0, 0.0).astype(BF16)

    def step_body():
        heads = range(HG_HEADS)
        hcs = [slice(HG_DK * hd, HG_DK * (hd + 1)) for hd in heads]
        scan_width = min(S5_HALF, (SUBLANES * S5_HALF) // NB)

        def hgrn_block(rb, mxu_fill):
            rows = rows_of(rb, ROW_BLOCK)
            lv = lev_ref[...]
            qs, ks, fs, vs, gcats = [], [], [], [], []
            for hd in heads:
                lb = lb_ref[:, hcs[hd]]
                q = proj_ref[rows, COL_Q + HG_DK * hd:COL_Q + HG_DK * (hd + 1)]
                gf = proj_ref[rows, COL_F + HG_DK * hd:COL_F + HG_DK * (hd + 1)]
                v = proj_ref[rows, COL_I + HG_DV * hd:COL_I + HG_DV * (hd + 1)].astype(BF16)
                f = lb + (1.0 - lb) * _sigmoid(gf)
                g = jnp.log2(f)
                k = 1.0 - f
                g1 = g.astype(BF16)
                r1 = g - g1.astype(F32)
                g2 = r1.astype(BF16)
                g3 = (r1 - g2.astype(F32)).astype(BF16)
                p4 = row & 3
                dec1 = jnp.where((row & 1) == 1, f, 1.0)
                dec2 = jnp.where(p4 == 0, pltpu.roll(f, ROW_BLOCK - 1, axis=0),
                                 jnp.where(p4 == 1, 1.0,
                                           jnp.where(p4 == 2, f, f * pltpu.roll(f, 1, axis=0))))
                qs.append(q)
                ks.append(k)
                fs.append((dec1.astype(BF16), dec2.astype(BF16)))
                vs.append(v)
                gcats.append(jnp.concatenate([g1, g2, g3], axis=1))
            bparts = [_dot(tri_ref[...], gcats[hd]) for hd in heads]
            bs = []
            for hd in heads:
                bp = bparts[hd]
                b = (bp[:, 0:HG_DK] + bp[:, HG_DK:2 * HG_DK]) + bp[:, 2 * HG_DK:3 * HG_DK]
                bsall_ref[hd] = b
                bs.append(b)
            qbs = [q.astype(BF16) for q in qs]
            kbs = [k.astype(BF16) for k in ks]
            atts = [jnp.where(lv == 0, _dot_nt(qbs[hd], kbs[hd]), 0.0) for hd in heads]
            for li, m in enumerate(levels):
                two_m = 2 * m
                p2 = row & (two_m - 1)
                if m > 2:
                    sgn = jnp.where(p2 >= m, 1.0, -1.0)
                for hd in heads:
                    if m <= 2:
                        decb = fs[hd][li]
                    else:
                        ref = _row_bcast_blocks(bsall_ref.at[hd], two_m, m - 1)
                        decb = jnp.exp2((bs[hd] - ref) * sgn).astype(BF16)
                    atts[hd] = jnp.where(lv == li + 1, _dot_nt(qbs[hd] * decb, kbs[hd] * decb), atts[hd])
                if li == 0:
                    for piece in mxu_fill:
                        piece()
            for hd in heads:
                o_ref[rows, hcs[hd]] = _dot(atts[hd].astype(BF16), vs[hd])
            for hd in heads:
                hc = hcs[hd]
                bs_ref = bsall_ref.at[hd]
                b = bs[hd]
                if C >= SUBLANES:
                    b_last = _row_bcast_blocks(bs_ref, C, C - 1)
                else:
                    b_last = b
                    for d in range(1, C):
                        b_last = jnp.where(tin == C - 1 - d, pltpu.roll(b, ROW_BLOCK - d, axis=0), b_last)
                qe_ref[rows, hc] = (qs[hd] * jnp.exp2(b)).astype(qe_ref.dtype)
                ke_ref[rows, hc] = (ks[hd] * jnp.exp2(b_last - b)).astype(ke_ref.dtype)
                v_ref[rows, hc] = vs[hd].astype(v_ref.dtype)
                for n in range(nbat):
                    last = bs_ref[n * C + C - 1:n * C + C, :]
                    dl_ref[rb * nbat + n, :, hc] = jnp.broadcast_to(jnp.exp2(last), (SUBLANES, HG_DK))

        SR = min(R, S5_INPUT_ROWS)
        sblocks = SR // ROW_BLOCK

        def merged(it, c):
            rows = rows_of(it, SR)
            hp = jnp.concatenate(
                [_dot(perm_ref[...], seq_major_block(hn_ref, it * sblocks + sb)).astype(BF16)
                 for sb in range(sblocks)], axis=0)
            uz = {}

            def uz_piece(sec):
                uz[sec] = _dot(hp, win_ref[:, W_S5 * sec:W_S5 * (sec + 1)])
                uz_ref[rows, W_S5 * sec:W_S5 * (sec + 1)] = uz[sec]

            def bu_piece(i):
                u_i = uz[0][:, W_S5 // 2 * i:W_S5 // 2 * (i + 1)].astype(BF16)
                bu_ref[rows, 2 * S5_HALF * i:2 * S5_HALF * (i + 1)] = _dot(u_i, bcat_ref[i])

            pieces = [functools.partial(uz_piece, 0), functools.partial(uz_piece, 1),
                      functools.partial(bu_piece, 0), functools.partial(bu_piece, 1)]
            per_block = len(pieces) // sblocks
            for sb in range(sblocks):
                hgrn_block(it * sblocks + sb, pieces[sb * per_block:(sb + 1) * per_block])
            for i in range(2):
                for off in range(0, S5_HALF, scan_width):
                    st = slice(S5_HALF * i + off, S5_HALF * i + off + scan_width)
                    re_c = 2 * S5_HALF * i + off
                    im_c = re_c + S5_HALF
                    ar = are_ref[:, st]
                    ai = aim_ref[:, st]
                    hr = hr_ref[:, st]
                    hi = hi_ref[:, st]
                    for tt in range(TB * sblocks):
                        trows = pl.ds(aligned(it * SR + tt * NB, NB), NB)
                        hr, hi = (ar * hr - ai * hi + bu_ref[trows, re_c:re_c + scan_width],
                                  ar * hi + ai * hr + bu_ref[trows, im_c:im_c + scan_width])
                        bu_ref[trows, re_c:re_c + scan_width] = hr
                        bu_ref[trows, im_c:im_c + scan_width] = hi
                    hr_ref[:, st] = hr
                    hi_ref[:, st] = hi
            return c
        assert R == SR
        loop(R // SR, merged)

        def inter_out(bi):
            srows = pl.ds(aligned(bi * C, C), C)
            for hd in heads:
                hc = hcs[hd]
                if has_state:
                    oi = _dot(qe_ref[srows, hc], s0_ref[bi, hd].astype(qe_ref.dtype))
                else:
                    oi = _dot_nt(qe_ref[srows, hc], s_ref[bi, hd].astype(qe_ref.dtype))
                o_ref[srows, hc] = o_ref[srows, hc] + oi

        def inter_state(bi):
            srows = pl.ds(aligned(bi * C, C), C)
            for hd in heads:
                hc = hcs[hd]
                if has_state:
                    upd = _dot_tn(ke_ref[srows, hc], v_ref[srows, hc])
                    s_ref[bi, hd] = s0_ref[bi, hd] * dl_ref[bi, :, hc].T[:, 0:1] + upd
                else:
                    upd = _dot_tn(v_ref[srows, hc], ke_ref[srows, hc])
                    s_ref[bi, hd] = s_ref[bi, hd] * dl_ref[bi, 0:1, hc] + upd

        RR = min(R, READOUT_ROWS)
        rbat = RR // C
        merge_inter = rbat <= 8

        def readout_inter(rb, c):
            rows = rows_of(rb, RR)
            ys = []
            for i in range(2):
                hs = bu_ref[rows, 2 * S5_HALF * i:2 * S5_HALF * (i + 1)].astype(BF16)
                ys.append(_dot(hs, ccat_ref[i]))
            if merge_inter:
                for n in range(rbat):
                    inter_out(rb * rbat + n)
            u = uz_ref[rows, 0:W_S5]
            y = jnp.concatenate(ys, axis=1) + d_ref[...] * u
            g = _gelu_tanh(y)
            glu = _dot(g.astype(BF16), wglu_ref[...]) + bglu_ref[...]
            if merge_inter:
                for n in range(rbat):
                    inter_state(rb * rbat + n)
            zs = uz_ref[rows, W_S5:2 * W_S5]
            s5 = g * _sigmoid(glu) * (zs * _sigmoid(zs))
            s5_ref[rows, :] = s5.astype(BF16)
            return c
        loop(R // RR, readout_inter)

        if not merge_inter and C >= SUBLANES:
            def inter(bi, c):
                inter_out(bi)
                inter_state(bi)
                return c
            lax.fori_loop(0, NB, inter, 0, unroll=2)

        if C < SUBLANES:
            assert has_state and not merge_inter
            per_tile = SUBLANES // C
            rid = lax.broadcasted_iota(jnp.int32, (SUBLANES, HG_DK), 0)
            cid = lax.broadcasted_iota(jnp.int32, (HG_DK, SUBLANES), 1)

            def inter_tile(ti, c):
                srows = pl.ds(pl.multiple_of(ti * SUBLANES, SUBLANES), SUBLANES)
                for hd in heads:
                    hc = hcs[hd]
                    qe8 = qe_ref[srows, hc]
                    ke_t = ke_ref[srows, hc].T
                    v8 = v_ref[srows, hc]
                    acc = o_ref[srows, hc]
                    dec = dl_ref[ti * per_tile + per_tile - 1, :, hc]
                    for s in range(per_tile - 1):
                        dec = jnp.where(rid == s, dl_ref[ti * per_tile + s, :, hc], dec)
                    dec_t = dec.T
                    for s in range(per_tile):
                        bi = ti * per_tile + s
                        st = s0_ref[bi, hd]
                        rows_s = (rid >= s * C) & (rid < (s + 1) * C)
                        cols_s = (cid >= s * C) & (cid < (s + 1) * C)
                        acc = acc + _dot(jnp.where(rows_s, qe8, 0.0), st)
                        upd = _dot(jnp.where(cols_s, ke_t, 0.0), v8)
                        s_ref[bi, hd] = st * dec_t[:, s:s + 1] + upd
                    o_ref[srows, hc] = acc
                return c
            lax.fori_loop(0, NB // per_tile, inter_tile, 0, unroll=4)

        for blk in range(n_rb):
            back = _dot(permt_ref[...], s5_ref[blk * ROW_BLOCK:(blk + 1) * ROW_BLOCK, :]).astype(BF16)
            if TB == C:
                cat_ref[blk * ROW_BLOCK:(blk + 1) * ROW_BLOCK, 0:W_S5] = back
            else:
                for b in range(NB):
                    cat_ref[b * C + blk * TB:b * C + (blk + 1) * TB, 0:W_S5] = back[b * TB:(b + 1) * TB, :]

        def phase5(rb, c):
            rows = rows_of(rb)
            for hd in range(HG_HEADS):
                hc = slice(HG_DV * hd, HG_DV * (hd + 1))
                o = o_ref[rows, hc]
                o = o * lax.rsqrt(jnp.mean(o * o, axis=-1, keepdims=True) + EPS)
                zh = proj_ref[rows, COL_ZH + HG_DV * hd:COL_ZH + HG_DV * (hd + 1)]
                hg = o * og_ref[:, hc] * (zh * _sigmoid(zh))
                cat_ref[rows, W_S5 + HG_DV * hd:W_S5 + HG_DV * (hd + 1)] = hg.astype(BF16)
            out = _dot(cat_ref[rows, :], wout_ref[...])
            xn = x_rows(rb) + out
            yv = xn * lax.rsqrt(jnp.mean(xn * xn, axis=-1, keepdims=True) + EPS) * fg_ref[...]
            if flat_rows:
                y_ref[rows, :] = yv
            else:
                y_ref[pl.ds(rb * mbat, mbat)] = yv.reshape(mbat, C, D_MODEL)
            return c
        loop(n_mb, phase5)

    project_chunk()
    step_body()

    if not has_state:
        @pl.when(j == nj - 1)
        def _fin():
            def tr(b, c):
                for hd in range(HG_HEADS):
                    s_ref[b, hd] = s_ref[b, hd].T
                return c
            lax.fori_loop(0, NB, tr, 0)


_VEC_WIDTHS = (D_MODEL, D_MODEL, W_S5, W_S5, W_HG)
_VEC_SLICES = tuple((sum(_VEC_WIDTHS[:n]), sum(_VEC_WIDTHS[:n + 1])) for n in range(len(_VEC_WIDTHS)))


def _pack_vectors(norm_g, final_norm_g, d, b_glu, onorm_g):
    parts = [norm_g, final_norm_g, d.reshape(-1), b_glu, onorm_g]
    assert tuple(p.shape[0] for p in parts) == _VEC_WIDTHS
    return jnp.concatenate([p.astype(F32) for p in parts]).reshape(1, sum(_VEC_WIDTHS))


def _time_major_perm(NB):
    steps = ROW_BLOCK // NB
    dst = np.arange(ROW_BLOCK)
    src = (dst % NB) * steps + dst // NB
    p = np.zeros((ROW_BLOCK, ROW_BLOCK), np.float32)
    p[dst, src] = 1.0
    return p


def _mixer_call(x, state, weights, *, NB, C, name):
    B, L, _ = x.shape
    has_state = state is not None
    assert not has_state or L == C, "a given state is consumed by a single chunk per sequence"
    grid = (B // NB, L // C)
    R = NB * C
    perm = _time_major_perm(NB)
    weights = (jnp.asarray(perm, BF16), jnp.asarray(perm.T, BF16)) + tuple(weights)

    def full(a):
        nd = a.ndim
        return pl.BlockSpec(a.shape, lambda i, j, nd=nd: (0,) * nd, pipeline_mode=pl.Buffered(1))

    flat_rows = C % SUBLANES != 0
    if flat_rows:
        assert L == C
        x = x.reshape(B * L, D_MODEL)
        x_spec = pl.BlockSpec((R, D_MODEL), lambda i, j: (i, 0))
    else:
        x_spec = pl.BlockSpec((NB, C, D_MODEL), lambda i, j: (i, j, 0))
    h_spec = pl.BlockSpec((NB, S5_LANES), lambda i, j: (i, 0))
    s_spec = pl.BlockSpec((NB, HG_HEADS, HG_DK, HG_DV), lambda i, j: (i, 0, 0, 0))
    in_specs = [x_spec] + ([h_spec, h_spec, s_spec] if has_state else []) + [full(w) for w in weights]
    args = [x] + (list(state) if has_state else []) + list(weights)
    out_shape = [
        jax.ShapeDtypeStruct(x.shape, x.dtype),
        jax.ShapeDtypeStruct((B, S5_LANES), F32),
        jax.ShapeDtypeStruct((B, S5_LANES), F32),
        jax.ShapeDtypeStruct((B, HG_HEADS, HG_DK, HG_DV), F32),
    ]
    chunk_dt = BF16 if C % BF16_TILE_ROWS == 0 else F32
    scratch = [
        pltpu.VMEM((R, D_MODEL), BF16),
        pltpu.VMEM((R, HG_COLS), F32),
        pltpu.VMEM((R, 2 * W_S5), F32),
        pltpu.VMEM((R, 2 * S5_LANES), F32),
        pltpu.VMEM((R, W_S5), BF16),
        pltpu.VMEM((R, W_HG), F32),
        pltpu.VMEM((R, W_HG), chunk_dt),
        pltpu.VMEM((R, W_HG), chunk_dt),
        pltpu.VMEM((R, W_HG), chunk_dt),
        pltpu.VMEM((HG_HEADS, ROW_BLOCK, HG_DK), F32),
        pltpu.VMEM((ROW_BLOCK, ROW_BLOCK), jnp.int32),
        pltpu.VMEM((ROW_BLOCK, ROW_BLOCK), BF16),
        pltpu.VMEM((NB, SUBLANES, W_HG), F32),
        pltpu.VMEM((R, D_MODEL), BF16),
    ]
    return pl.pallas_call(
        functools.partial(_mixer_kernel, NB=NB, C=C, has_state=has_state),
        grid=grid,
        in_specs=in_specs,
        out_specs=[x_spec, h_spec, h_spec, s_spec],
        out_shape=out_shape,
        scratch_shapes=scratch,
        compiler_params=pltpu.CompilerParams(
            dimension_semantics=("arbitrary", "arbitrary"),
            vmem_limit_bytes=V7X_VMEM_LIMIT),
        name=name,
    )(*args)


def kernel(x_prompt, x_sample, state_s5_re, state_s5_im, state_hgrn, norm_g, w_in, s5_lambda_re, s5_lambda_im,
           s5_log_step, s5_b_re, s5_b_im, s5_c_re, s5_c_im, s5_d, w_glu, b_glu, hgrn_lb_logits, hgrn_onorm_g,
           w_out, final_norm_g):
    assert norm_g.shape[0] == 1, "single-layer model"
    l = 0
    a_re, a_im, bbre, bbim, lb = _prepare_params(
        s5_lambda_re[l], s5_lambda_im[l], s5_log_step[l], s5_b_re[l], s5_b_im[l], hgrn_lb_logits)
    bcat, ccat = _block_diag_weights(bbre, bbim, s5_c_re[l], s5_c_im[l])
    weights = (
        _pack_vectors(norm_g[l], final_norm_g, s5_d[l], b_glu[l], hgrn_onorm_g[l]),
        w_in[l].astype(BF16),
        a_re, a_im, bcat, ccat,
        w_glu[l].astype(BF16),
        lb,
        w_out[l].astype(BF16),
    )

    Bp, Lp, _ = x_prompt.shape
    yp, p_re, p_im, p_hg = _mixer_call(x_prompt, None, weights, NB=Bp, C=PROMPT_CHUNK, name="mixer_prompt")

    Bs, Ls, _ = x_sample.shape
    state = (state_s5_re[l].reshape(Bs, S5_LANES).astype(F32),
             state_s5_im[l].reshape(Bs, S5_LANES).astype(F32),
             state_hgrn[l].astype(F32))
    ys, s_re, s_im, s_hg = _mixer_call(x_sample, state, weights, NB=ROW_BLOCK // Ls, C=Ls, name="mixer_sample")
    ys = ys.reshape(Bs, Ls, D_MODEL)

    sd = state_s5_re.dtype
    shp = (1, Bp, S5_GROUPS, S5_STATE)
    shs = (1, Bs, S5_GROUPS, S5_STATE)
    return (yp, ys,
            p_re.reshape(shp).astype(sd), p_im.reshape(shp).astype(sd), p_hg[None].astype(state_hgrn.dtype),
            s_re.reshape(shs).astype(sd), s_im.reshape(shs).astype(sd), s_hg[None].astype(state_hgrn.dtype))
```

```python
import functools
import math

import jax
import jax.numpy as jnp
import numpy as np
from jax import lax
from jax.experimental import pallas as pl
from jax.experimental.pallas import tpu as pltpu

F32 = jnp.float32
BF16 = jnp.bfloat16

D_MODEL = 1024
W_S5 = 512
W_HG = 512
S5_CH = 16
S5_GROUPS = 32
S5_STATE = 64
S5_LANES = S5_GROUPS * S5_STATE
S5_HALF = S5_LANES // 2
HG_DK = 128
HG_HEADS = 4
HG_DV = 128
EPS = 1e-6
LAMBDA_RE_MAX = -1e-4
PROJ_OUT = 3072
OFF_Q = 1024
HG_COLS = PROJ_OUT - OFF_Q
SECTION = 512
COL_Q, COL_F, COL_I, COL_ZH = (SECTION * n for n in range(4))
PROMPT_CHUNK = 64
PROMPT_CHUNKS_PER_STEP = 2
LOG2E = 1.4426950408889634

LANES = 128
SUBLANES = 8
BF16_TILE_ROWS = 16
ROW_BLOCK = 128
MATMUL_ROWS = 512
READOUT_ROWS = 512
S5_INPUT_ROWS = 512
V7X_VMEM_LIMIT = 60 * 1024 * 1024


def _prep_kernel(lre_ref, lim_ref, ls_ref, brt_ref, bit_ref, lg_ref,
                 are_ref, aim_ref, bbre_ref, bbim_ref, lb_ref):
    lr = jnp.minimum(lre_ref[...], LAMBDA_RE_MAX)
    li = lim_ref[...]
    dt = jnp.exp(ls_ref[...])
    er = jnp.exp(lr * dt)
    a_re = er * jnp.cos(li * dt)
    a_im = er * jnp.sin(li * dt)
    den = lr * lr + li * li
    zr = ((a_re - 1.0) * lr + a_im * li) / den
    zi = (a_im * lr - (a_re - 1.0) * li) / den
    for g in range(S5_GROUPS):
        cols = slice(S5_STATE * g, S5_STATE * (g + 1))
        are_ref[:, cols] = a_re[g:g + 1, :]
        aim_ref[:, cols] = a_im[g:g + 1, :]
        br = brt_ref[g]
        bi = bit_ref[g]
        zr_g = zr[g:g + 1, :]
        zi_g = zi[g:g + 1, :]
        bbre_ref[g] = zr_g * br - zi_g * bi
        bbim_ref[g] = zr_g * bi + zi_g * br
    lg = lg_ref[...]
    e = jnp.exp(lg - jnp.max(lg, axis=0, keepdims=True))
    lb_ref[...] = e[0:1, :] / jnp.sum(e, axis=0, keepdims=True)


def _prepare_params(lam_re, lam_im, log_step, b_re, b_im, lb_logits):
    gps = jax.ShapeDtypeStruct((1, S5_LANES), F32)
    gcs = jax.ShapeDtypeStruct((S5_GROUPS, S5_CH, S5_STATE), F32)
    return pl.pallas_call(
        _prep_kernel,
        out_shape=[gps, gps, gcs, gcs, jax.ShapeDtypeStruct((1, HG_HEADS * HG_DK), F32)],
        name="s5_hgrn_prep",
    )(lam_re.astype(F32), lam_im.astype(F32), log_step.astype(F32).reshape(S5_GROUPS, 1),
      jnp.transpose(b_re.astype(F32), (0, 2, 1)), jnp.transpose(b_im.astype(F32), (0, 2, 1)),
      lb_logits.astype(F32))


def _block_diag_weights(bbre, bbim, c_re, c_im):
    hg = S5_GROUPS // 2
    eye = jnp.eye(hg, dtype=BF16)[None, :, None, :, None]

    def in_half(bbt):
        x = bbt.astype(BF16).reshape(2, hg, S5_CH, 1, S5_STATE) * eye
        return x.reshape(2, hg * S5_CH, hg * S5_STATE)

    def out_half(c):
        ct = jnp.transpose(c.astype(BF16), (0, 2, 1)).reshape(2, hg, S5_STATE, 1, S5_CH)
        return (ct * eye).reshape(2, hg * S5_STATE, hg * S5_CH)

    bcat = jnp.concatenate([in_half(bbre), in_half(bbim)], axis=2)
    ccat = jnp.concatenate([out_half(c_re), out_half(-c_im)], axis=1)
    return bcat, ccat


def _sigmoid(x):
    return 1.0 / (1.0 + jnp.exp2(x * (-LOG2E)))


def _gelu_tanh(x):
    return 0.5 * x * (1.0 + jnp.tanh(math.sqrt(2.0 / math.pi) * (x + 0.044715 * (x * x * x))))


def _dot(a, b):
    return jnp.dot(a, b, preferred_element_type=F32)


def _dot_nt(a, b):
    return lax.dot_general(a, b, (((1,), (1,)), ((), ())), preferred_element_type=F32)


def _dot_tn(a, b):
    return lax.dot_general(a, b, (((0,), (0,)), ((), ())), preferred_element_type=F32)


def _row_bcast_blocks(src_ref, blk, off):
    parts = []
    for n in range(ROW_BLOCK // blk):
        row = src_ref[n * blk + off:n * blk + off + 1, :]
        parts.append(jnp.broadcast_to(row, (blk, HG_DK)))
    return parts[0] if len(parts) == 1 else jnp.concatenate(parts, axis=0)


def _mixer_kernel(*refs, NB, C, chunks, has_state):
    if has_state:
        x_ref, h0r_ref, h0i_ref, s0_ref = refs[:4]
        refs = refs[4:]
    else:
        x_ref = refs[0]
        refs = refs[1:]
    (perm_ref, permt_ref, vec_ref, win_ref, are_ref, aim_ref, bcat_ref, ccat_ref, wglu_ref, lb_ref, wout_ref,
     y_ref, hr_ref, hi_ref, s_ref,
     hn_ref, proj_ref, uz_ref, bu_ref, s5_ref, o_ref, qe_ref, ke_ref, v_ref, bsall_ref, lev_ref, tri_ref,
     dl_ref, cat_ref) = refs

    ng_ref, fg_ref, d_ref, bglu_ref, og_ref = (vec_ref.at[:, a:b] for a, b in _VEC_SLICES)

    j = pl.program_id(1)
    nj = pl.num_programs(1)
    R = NB * C
    n_rb = R // ROW_BLOCK
    nbat = ROW_BLOCK // C

    @pl.when(j == 0)
    def _init():
        if has_state:
            hr_ref[...] = h0r_ref[...]
            hi_ref[...] = h0i_ref[...]
        else:
            hr_ref[...] = jnp.zeros(hr_ref.shape, F32)
            hi_ref[...] = jnp.zeros(hi_ref.shape, F32)
            s_ref[...] = jnp.zeros(s_ref.shape, F32)

    MM = min(R, MATMUL_ROWS)
    n_mb = R // MM
    mbat = MM // C

    def aligned(start, n):
        return start if isinstance(start, int) else pl.multiple_of(start, n)

    def rows_of(rb, n=MM):
        return pl.ds(aligned(rb * n, n), n)

    def loop(trips, body):
        if trips == 1:
            body(0, 0)
        else:
            lax.fori_loop(0, trips, body, 0)

    TB = ROW_BLOCK // NB

    def seq_major_block(src_ref, blk):
        assert isinstance(blk, int)
        if TB == C:
            return src_ref[blk * ROW_BLOCK:(blk + 1) * ROW_BLOCK, :]
        return jnp.concatenate([src_ref[b * C + blk * TB:b * C + (blk + 1) * TB, :] for b in range(NB)], axis=0)

    flat_rows = len(x_ref.shape) == 2

    sub = [0]

    def tokens():
        return slice(sub[0] * C, (sub[0] + 1) * C)

    def x_rows(mb):
        if flat_rows:
            return x_ref[rows_of(mb), :]
        return x_ref[pl.ds(mb * mbat, mbat), tokens(), :].reshape(MM, D_MODEL)

    def prenorm(x):
        return (x * lax.rsqrt(jnp.mean(x * x, axis=-1, keepdims=True) + EPS) * ng_ref[...]).astype(BF16)

    def hg_proj(hn, sec):
        return _dot(hn, win_ref[:, OFF_Q + SECTION * sec:OFF_Q + SECTION * (sec + 1)])

    def project_chunk():
        def body(rb, c):
            rows = rows_of(rb)
            hn = prenorm(x_rows(rb))
            hn_ref[rows, :] = hn
            for sec in range(HG_COLS // SECTION):
                proj_ref[rows, SECTION * sec:SECTION * (sec + 1)] = hg_proj(hn, sec)
            return c
        loop(n_mb, body)

    row = lax.broadcasted_iota(jnp.int32, (ROW_BLOCK, ROW_BLOCK), 0)
    col = lax.broadcasted_iota(jnp.int32, (ROW_BLOCK, ROW_BLOCK), 1)
    tin = row & (C - 1)
    levels = []
    m = 1
    while m < C:
        levels.append(m)
        m *= 2

    @pl.when(j == 0)
    def _masks():
        lev = jnp.where(row == col, 0, -1)
        for li, m in enumerate(levels):
            same = (row & ~(2 * m - 1)) == (col & ~(2 * m - 1))
            own = same & ((row & m) != 0) & ((col & m) == 0)
            lev = jnp.where(own, li + 1, lev)
        lev_ref[...] = lev
        in_chunk = (row & ~(C - 1)) == (col & ~(C - 1))
        tri_ref[...] = jnp.where(in_chunk & (col <= row), 1.0, 0.0).astype(BF16)

    def step_body():
        heads = range(HG_HEADS)
        hcs = [slice(HG_DK * hd, HG_DK * (hd + 1)) for hd in heads]
        scan_width = min(S5_HALF, (SUBLANES * S5_HALF) // NB)

        def hgrn_block(rb, mxu_fill):
            rows = rows_of(rb, ROW_BLOCK)
            lv = lev_ref[...]
            qs, ks, fs, vs, gcats = [], [], [], [], []
            for hd in heads:
                lb = lb_ref[:, hcs[hd]]
                q = proj_ref[rows, COL_Q + HG_DK * hd:COL_Q + HG_DK * (hd + 1)]
                gf = proj_ref[rows, COL_F + HG_DK * hd:COL_F + HG_DK * (hd + 1)]
                v = proj_ref[rows, COL_I + HG_DV * hd:COL_I + HG_DV * (hd + 1)].astype(BF16)
                f = lb + (1.0 - lb) * _sigmoid(gf)
                g = jnp.log2(f)
                k = 1.0 - f
                g1 = g.astype(BF16)
                r1 = g - g1.astype(F32)
                g2 = r1.astype(BF16)
                g3 = (r1 - g2.astype(F32)).astype(BF16)
                p4 = row & 3
                dec1 = jnp.where((row & 1) == 1, f, 1.0)
                dec2 = jnp.where(p4 == 0, pltpu.roll(f, ROW_BLOCK - 1, axis=0),
                                 jnp.where(p4 == 1, 1.0,
                                           jnp.where(p4 == 2, f, f * pltpu.roll(f, 1, axis=0))))
                qs.append(q)
                ks.append(k)
                fs.append((dec1.astype(BF16), dec2.astype(BF16)))
                vs.append(v)
                gcats.append(jnp.concatenate([g1, g2, g3], axis=1))
            bparts = [_dot(tri_ref[...], gcats[hd]) for hd in heads]
            bs = []
            for hd in heads:
                bp = bparts[hd]
                b = (bp[:, 0:HG_DK] + bp[:, HG_DK:2 * HG_DK]) + bp[:, 2 * HG_DK:3 * HG_DK]
                bsall_ref[hd] = b
                bs.append(b)
            qbs = [q.astype(BF16) for q in qs]
            kbs = [k.astype(BF16) for k in ks]
            atts = [jnp.where(lv == 0, _dot_nt(qbs[hd], kbs[hd]), 0.0) for hd in heads]
            for li, m in enumerate(levels):
                two_m = 2 * m
                p2 = row & (two_m - 1)
                if m > 2:
                    sgn = jnp.where(p2 >= m, 1.0, -1.0)
                for hd in heads:
                    if m <= 2:
                        decb = fs[hd][li]
                    else:
                        ref = _row_bcast_blocks(bsall_ref.at[hd], two_m, m - 1)
                        decb = jnp.exp2((bs[hd] - ref) * sgn).astype(BF16)
                    atts[hd] = jnp.where(lv == li + 1, _dot_nt(qbs[hd] * decb, kbs[hd] * decb), atts[hd])
                if li == 0:
                    for piece in mxu_fill:
                        piece()
            for hd in heads:
                o_ref[rows, hcs[hd]] = _dot(atts[hd].astype(BF16), vs[hd])
            for hd in heads:
                hc = hcs[hd]
                bs_ref = bsall_ref.at[hd]
                b = bs[hd]
                if C >= SUBLANES:
                    b_last = _row_bcast_blocks(bs_ref, C, C - 1)
                else:
                    b_last = b
                    for d in range(1, C):
                        b_last = jnp.where(tin == C - 1 - d, pltpu.roll(b, ROW_BLOCK - d, axis=0), b_last)
                qe_ref[rows, hc] = (qs[hd] * jnp.exp2(b)).astype(qe_ref.dtype)
                ke_ref[rows, hc] = (ks[hd] * jnp.exp2(b_last - b)).astype(ke_ref.dtype)
                v_ref[rows, hc] = vs[hd].astype(v_ref.dtype)
                for n in range(nbat):
                    last = bs_ref[n * C + C - 1:n * C + C, :]
                    dl_ref[rb * nbat + n, :, hc] = jnp.broadcast_to(jnp.exp2(last), (SUBLANES, HG_DK))

        SR = min(R, S5_INPUT_ROWS)
        sblocks = SR // ROW_BLOCK

        def merged(it, c):
            rows = rows_of(it, SR)
            hp = jnp.concatenate(
                [_dot(perm_ref[...], seq_major_block(hn_ref, it * sblocks + sb)).astype(BF16)
                 for sb in range(sblocks)], axis=0)
            uz = {}

            def uz_piece(sec):
                uz[sec] = _dot(hp, win_ref[:, W_S5 * sec:W_S5 * (sec + 1)])
                uz_ref[rows, W_S5 * sec:W_S5 * (sec + 1)] = uz[sec]

            def bu_piece(i):
                u_i = uz[0][:, W_S5 // 2 * i:W_S5 // 2 * (i + 1)].astype(BF16)
                bu_ref[rows, 2 * S5_HALF * i:2 * S5_HALF * (i + 1)] = _dot(u_i, bcat_ref[i])

            pieces = [functools.partial(uz_piece, 0), functools.partial(uz_piece, 1),
                      functools.partial(bu_piece, 0), functools.partial(bu_piece, 1)]
            per_block = len(pieces) // sblocks
            for sb in range(sblocks):
                hgrn_block(it * sblocks + sb, pieces[sb * per_block:(sb + 1) * per_block])
            for i in range(2):
                for off in range(0, S5_HALF, scan_width):
                    st = slice(S5_HALF * i + off, S5_HALF * i + off + scan_width)
                    re_c = 2 * S5_HALF * i + off
                    im_c = re_c + S5_HALF
                    ar = are_ref[:, st]
                    ai = aim_ref[:, st]
                    hr = hr_ref[:, st]
                    hi = hi_ref[:, st]
                    for tt in range(TB * sblocks):
                        trows = pl.ds(aligned(it * SR + tt * NB, NB), NB)
                        hr, hi = (ar * hr - ai * hi + bu_ref[trows, re_c:re_c + scan_width],
                                  ar * hi + ai * hr + bu_ref[trows, im_c:im_c + scan_width])
                        bu_ref[trows, re_c:re_c + scan_width] = hr
                        bu_ref[trows, im_c:im_c + scan_width] = hi
                    hr_ref[:, st] = hr
                    hi_ref[:, st] = hi
            return c
        assert R == SR
        loop(R // SR, merged)

        def inter_out(bi):
            srows = pl.ds(aligned(bi * C, C), C)
            for hd in heads:
                hc = hcs[hd]
                if has_state:
                    oi = _dot(qe_ref[srows, hc], s0_ref[bi, hd].astype(qe_ref.dtype))
                else:
                    oi = _dot_nt(qe_ref[srows, hc], s_ref[bi, hd].astype(qe_ref.dtype))
                o_ref[srows, hc] = o_ref[srows, hc] + oi

        def inter_state(bi):
            srows = pl.ds(aligned(bi * C, C), C)
            for hd in heads:
                hc = hcs[hd]
                if has_state:
                    upd = _dot_tn(ke_ref[srows, hc], v_ref[srows, hc])
                    s_ref[bi, hd] = s0_ref[bi, hd] * dl_ref[bi, :, hc].T[:, 0:1] + upd
                else:
                    upd = _dot_tn(v_ref[srows, hc], ke_ref[srows, hc])
                    s_ref[bi, hd] = s_ref[bi, hd] * dl_ref[bi, 0:1, hc] + upd

        RR = min(R, READOUT_ROWS)
        rbat = RR // C
        merge_inter = rbat <= 8

        def readout_inter(rb, c):
            rows = rows_of(rb, RR)
            ys = []
            for i in range(2):
                hs = bu_ref[rows, 2 * S5_HALF * i:2 * S5_HALF * (i + 1)].astype(BF16)
                ys.append(_dot(hs, ccat_ref[i]))
            if merge_inter:
                for n in range(rbat):
                    inter_out(rb * rbat + n)
            u = uz_ref[rows, 0:W_S5]
            y = jnp.concatenate(ys, axis=1) + d_ref[...] * u
            g = _gelu_tanh(y)
            glu = _dot(g.astype(BF16), wglu_ref[...]) + bglu_ref[...]
            if merge_inter:
                for n in range(rbat):
                    inter_state(rb * rbat + n)
            zs = uz_ref[rows, W_S5:2 * W_S5]
            s5 = g * _sigmoid(glu) * (zs * _sigmoid(zs))
            s5_ref[rows, :] = s5.astype(BF16)
            return c
        loop(R // RR, readout_inter)

        if not merge_inter and C >= SUBLANES:
            def inter(bi, c):
                inter_out(bi)
                inter_state(bi)
                return c
            lax.fori_loop(0, NB, inter, 0, unroll=2)

        if C < SUBLANES:
            assert has_state and not merge_inter
            per_tile = SUBLANES // C
            rid = lax.broadcasted_iota(jnp.int32, (SUBLANES, HG_DK), 0)
            cid = lax.broadcasted_iota(jnp.int32, (HG_DK, SUBLANES), 1)

            def inter_tile(ti, c):
                srows = pl.ds(pl.multiple_of(ti * SUBLANES, SUBLANES), SUBLANES)
                for hd in heads:
                    hc = hcs[hd]
                    qe8 = qe_ref[srows, hc]
                    ke_t = ke_ref[srows, hc].T
                    v8 = v_ref[srows, hc]
                    acc = o_ref[srows, hc]
                    dec = dl_ref[ti * per_tile + per_tile - 1, :, hc]
                    for s in range(per_tile - 1):
                        dec = jnp.where(rid == s, dl_ref[ti * per_tile + s, :, hc], dec)
                    dec_t = dec.T
                    for s in range(per_tile):
                        bi = ti * per_tile + s
                        st = s0_ref[bi, hd]
                        rows_s = (rid >= s * C) & (rid < (s + 1) * C)
                        cols_s = (cid >= s * C) & (cid < (s + 1) * C)
                        acc = acc + _dot(jnp.where(rows_s, qe8, 0.0), st)
                        upd = _dot(jnp.where(cols_s, ke_t, 0.0), v8)
                        s_ref[bi, hd] = st * dec_t[:, s:s + 1] + upd
                    o_ref[srows, hc] = acc
                return c
            lax.fori_loop(0, NB // per_tile, inter_tile, 0, unroll=4)

        for blk in range(n_rb):
            back = _dot(permt_ref[...], s5_ref[blk * ROW_BLOCK:(blk + 1) * ROW_BLOCK, :]).astype(BF16)
            if TB == C:
                cat_ref[blk * ROW_BLOCK:(blk + 1) * ROW_BLOCK, 0:W_S5] = back
            else:
                for b in range(NB):
                    cat_ref[b * C + blk * TB:b * C + (blk + 1) * TB, 0:W_S5] = back[b * TB:(b + 1) * TB, :]

        def phase5(rb, c):
            rows = rows_of(rb)
            for hd in range(HG_HEADS):
                hc = slice(HG_DV * hd, HG_DV * (hd + 1))
                o = o_ref[rows, hc]
                o = o * lax.rsqrt(jnp.mean(o * o, axis=-1, keepdims=True) + EPS)
                zh = proj_ref[rows, COL_ZH + HG_DV * hd:COL_ZH + HG_DV * (hd + 1)]
                hg = o * og_ref[:, hc] * (zh * _sigmoid(zh))
                cat_ref[rows, W_S5 + HG_DV * hd:W_S5 + HG_DV * (hd + 1)] = hg.astype(BF16)
            out = _dot(cat_ref[rows, :], wout_ref[...])
            xn = x_rows(rb) + out
            yv = xn * lax.rsqrt(jnp.mean(xn * xn, axis=-1, keepdims=True) + EPS) * fg_ref[...]
            if flat_rows:
                y_ref[rows, :] = yv
            else:
                y_ref[pl.ds(rb * mbat, mbat), tokens(), :] = yv.reshape(mbat, C, D_MODEL)
            return c
        loop(n_mb, phase5)

    for k in range(chunks):
        sub[0] = k
        project_chunk()
        step_body()

    if not has_state:
        @pl.when(j == nj - 1)
        def _fin():
            def tr(b, c):
                for hd in range(HG_HEADS):
                    s_ref[b, hd] = s_ref[b, hd].T
                return c
            lax.fori_loop(0, NB, tr, 0)


_VEC_WIDTHS = (D_MODEL, D_MODEL, W_S5, W_S5, W_HG)
_VEC_SLICES = tuple((sum(_VEC_WIDTHS[:n]), sum(_VEC_WIDTHS[:n + 1])) for n in range(len(_VEC_WIDTHS)))


def _pack_vectors(norm_g, final_norm_g, d, b_glu, onorm_g):
    parts = [norm_g, final_norm_g, d.reshape(-1), b_glu, onorm_g]
    assert tuple(p.shape[0] for p in parts) == _VEC_WIDTHS
    return jnp.concatenate([p.astype(F32) for p in parts]).reshape(1, sum(_VEC_WIDTHS))


def _time_major_perm(NB):
    steps = ROW_BLOCK // NB
    dst = np.arange(ROW_BLOCK)
    src = (dst % NB) * steps + dst // NB
    p = np.zeros((ROW_BLOCK, ROW_BLOCK), np.float32)
    p[dst, src] = 1.0
    return p


def _mixer_call(x, state, weights, *, NB, C, name, chunks=1):
    B, L, _ = x.shape
    has_state = state is not None
    assert not has_state or L == C, "a given state is consumed by a single chunk per sequence"
    grid = (B // NB, L // (C * chunks))
    R = NB * C
    perm = _time_major_perm(NB)
    weights = (jnp.asarray(perm, BF16), jnp.asarray(perm.T, BF16)) + tuple(weights)

    def full(a):
        nd = a.ndim
        return pl.BlockSpec(a.shape, lambda i, j, nd=nd: (0,) * nd, pipeline_mode=pl.Buffered(1))

    flat_rows = C % SUBLANES != 0
    if flat_rows:
        assert L == C
        x = x.reshape(B * L, D_MODEL)
        x_spec = pl.BlockSpec((R, D_MODEL), lambda i, j: (i, 0))
    else:
        x_spec = pl.BlockSpec((NB, C * chunks, D_MODEL), lambda i, j: (i, j, 0))
    h_spec = pl.BlockSpec((NB, S5_LANES), lambda i, j: (i, 0))
    s_spec = pl.BlockSpec((NB, HG_HEADS, HG_DK, HG_DV), lambda i, j: (i, 0, 0, 0))
    in_specs = [x_spec] + ([h_spec, h_spec, s_spec] if has_state else []) + [full(w) for w in weights]
    args = [x] + (list(state) if has_state else []) + list(weights)
    out_shape = [
        jax.ShapeDtypeStruct(x.shape, x.dtype),
        jax.ShapeDtypeStruct((B, S5_LANES), F32),
        jax.ShapeDtypeStruct((B, S5_LANES), F32),
        jax.ShapeDtypeStruct((B, HG_HEADS, HG_DK, HG_DV), F32),
    ]
    chunk_dt = BF16 if C % BF16_TILE_ROWS == 0 else F32
    scratch = [
        pltpu.VMEM((R, D_MODEL), BF16),
        pltpu.VMEM((R, HG_COLS), F32),
        pltpu.VMEM((R, 2 * W_S5), F32),
        pltpu.VMEM((R, 2 * S5_LANES), F32),
        pltpu.VMEM((R, W_S5), BF16),
        pltpu.VMEM((R, W_HG), F32),
        pltpu.VMEM((R, W_HG), chunk_dt),
        pltpu.VMEM((R, W_HG), chunk_dt),
        pltpu.VMEM((R, W_HG), chunk_dt),
        pltpu.VMEM((HG_HEADS, ROW_BLOCK, HG_DK), F32),
        pltpu.VMEM((ROW_BLOCK, ROW_BLOCK), jnp.int32),
        pltpu.VMEM((ROW_BLOCK, ROW_BLOCK), BF16),
        pltpu.VMEM((NB, SUBLANES, W_HG), F32),
        pltpu.VMEM((R, D_MODEL), BF16),
    ]
    return pl.pallas_call(
        functools.partial(_mixer_kernel, NB=NB, C=C, chunks=chunks, has_state=has_state),
        grid=grid,
        in_specs=in_specs,
        out_specs=[x_spec, h_spec, h_spec, s_spec],
        out_shape=out_shape,
        scratch_shapes=scratch,
        compiler_params=pltpu.CompilerParams(
            dimension_semantics=("arbitrary", "arbitrary"),
            vmem_limit_bytes=V7X_VMEM_LIMIT),
        name=name,
    )(*args)


def kernel(x_prompt, x_sample, state_s5_re, state_s5_im, state_hgrn, norm_g, w_in, s5_lambda_re, s5_lambda_im,
           s5_log_step, s5_b_re, s5_b_im, s5_c_re, s5_c_im, s5_d, w_glu, b_glu, hgrn_lb_logits, hgrn_onorm_g,
           w_out, final_norm_g):
    assert norm_g.shape[0] == 1, "single-layer model"
    l = 0
    a_re, a_im, bbre, bbim, lb = _prepare_params(
        s5_lambda_re[l], s5_lambda_im[l], s5_log_step[l], s5_b_re[l], s5_b_im[l], hgrn_lb_logits)
    bcat, ccat = _block_diag_weights(bbre, bbim, s5_c_re[l], s5_c_im[l])
    weights = (
        _pack_vectors(norm_g[l], final_norm_g, s5_d[l], b_glu[l], hgrn_onorm_g[l]),
        w_in[l].astype(BF16),
        a_re, a_im, bcat, ccat,
        w_glu[l].astype(BF16),
        lb,
        w_out[l].astype(BF16),
    )

    Bp, Lp, _ = x_prompt.shape
    yp, p_re, p_im, p_hg = _mixer_call(x_prompt, None, weights, NB=Bp, C=PROMPT_CHUNK, chunks=PROMPT_CHUNKS_PER_STEP,
                                       name="mixer_prompt")

    Bs, Ls, _ = x_sample.shape
    state = (state_s5_re[l].reshape(Bs, S5_LANES).astype(F32),
             state_s5_im[l].reshape(Bs, S5_LANES).astype(F32),
             state_hgrn[l].astype(F32))
    ys, s_re, s_im, s_hg = _mixer_call(x_sample, state, weights, NB=ROW_BLOCK // Ls, C=Ls, name="mixer_sample")
    ys = ys.reshape(Bs, Ls, D_MODEL)

    sd = state_s5_re.dtype
    shp = (1, Bp, S5_GROUPS, S5_STATE)
    shs = (1, Bs, S5_GROUPS, S5_STATE)
    return (yp, ys,
            p_re.reshape(shp).astype(sd), p_im.reshape(shp).astype(sd), p_hg[None].astype(state_hgrn.dtype),
            s_re.reshape(shs).astype(sd), s_im.reshape(shs).astype(sd), s_hg[None].astype(state_hgrn.dtype))
```

```python
import functools
import math

import jax
import jax.numpy as jnp
import numpy as np
from jax import lax
from jax.experimental import pallas as pl
from jax.experimental.pallas import tpu as pltpu

F32 = jnp.float32
BF16 = jnp.bfloat16

D_MODEL = 1024
W_S5 = 512
W_HG = 512
S5_CH = 16
S5_GROUPS = 32
S5_STATE = 64
S5_LANES = S5_GROUPS * S5_STATE
S5_HALF = S5_LANES // 2
HG_DK = 128
HG_HEADS = 4
HG_DV = 128
EPS = 1e-6
LAMBDA_RE_MAX = -1e-4
PROJ_OUT = 3072
OFF_Q = 1024
HG_COLS = PROJ_OUT - OFF_Q
SECTION = 512
COL_Q, COL_F, COL_I, COL_ZH = (SECTION * n for n in range(4))
PROMPT_CHUNK = 32
LOG2E = 1.4426950408889634

LANES = 128
SUBLANES = 8
BF16_TILE_ROWS = 16
ROW_BLOCK = 128
MATMUL_ROWS = 512
READOUT_ROWS = 512
S5_INPUT_ROWS = 512
V7X_VMEM_LIMIT = 60 * 1024 * 1024


def _prep_kernel(lre_ref, lim_ref, ls_ref, brt_ref, bit_ref, lg_ref,
                 are_ref, aim_ref, bbre_ref, bbim_ref, lb_ref):
    lr = jnp.minimum(lre_ref[...], LAMBDA_RE_MAX)
    li = lim_ref[...]
    dt = jnp.exp(ls_ref[...])
    er = jnp.exp(lr * dt)
    a_re = er * jnp.cos(li * dt)
    a_im = er * jnp.sin(li * dt)
    den = lr * lr + li * li
    zr = ((a_re - 1.0) * lr + a_im * li) / den
    zi = (a_im * lr - (a_re - 1.0) * li) / den
    for g in range(S5_GROUPS):
        cols = slice(S5_STATE * g, S5_STATE * (g + 1))
        are_ref[:, cols] = a_re[g:g + 1, :]
        aim_ref[:, cols] = a_im[g:g + 1, :]
        br = brt_ref[g]
        bi = bit_ref[g]
        zr_g = zr[g:g + 1, :]
        zi_g = zi[g:g + 1, :]
        bbre_ref[g] = zr_g * br - zi_g * bi
        bbim_ref[g] = zr_g * bi + zi_g * br
    lg = lg_ref[...]
    e = jnp.exp(lg - jnp.max(lg, axis=0, keepdims=True))
    lb_ref[...] = e[0:1, :] / jnp.sum(e, axis=0, keepdims=True)


def _prepare_params(lam_re, lam_im, log_step, b_re, b_im, lb_logits):
    gps = jax.ShapeDtypeStruct((1, S5_LANES), F32)
    gcs = jax.ShapeDtypeStruct((S5_GROUPS, S5_CH, S5_STATE), F32)
    return pl.pallas_call(
        _prep_kernel,
        out_shape=[gps, gps, gcs, gcs, jax.ShapeDtypeStruct((1, HG_HEADS * HG_DK), F32)],
        name="s5_hgrn_prep",
    )(lam_re.astype(F32), lam_im.astype(F32), log_step.astype(F32).reshape(S5_GROUPS, 1),
      jnp.transpose(b_re.astype(F32), (0, 2, 1)), jnp.transpose(b_im.astype(F32), (0, 2, 1)),
      lb_logits.astype(F32))


def _block_diag_weights(bbre, bbim, c_re, c_im):
    hg = S5_GROUPS // 2
    eye = jnp.eye(hg, dtype=BF16)[None, :, None, :, None]

    def in_half(bbt):
        x = bbt.astype(BF16).reshape(2, hg, S5_CH, 1, S5_STATE) * eye
        return x.reshape(2, hg * S5_CH, hg * S5_STATE)

    def out_half(c):
        ct = jnp.transpose(c.astype(BF16), (0, 2, 1)).reshape(2, hg, S5_STATE, 1, S5_CH)
        return (ct * eye).reshape(2, hg * S5_STATE, hg * S5_CH)

    bcat = jnp.concatenate([in_half(bbre), in_half(bbim)], axis=2)
    ccat = jnp.concatenate([out_half(c_re), out_half(-c_im)], axis=1)
    return bcat, ccat


def _sigmoid(x):
    return 1.0 / (1.0 + jnp.exp2(x * (-LOG2E)))


def _gelu_tanh(x):
    return 0.5 * x * (1.0 + jnp.tanh(math.sqrt(2.0 / math.pi) * (x + 0.044715 * (x * x * x))))


def _dot(a, b):
    return jnp.dot(a, b, preferred_element_type=F32)


def _dot_nt(a, b):
    return lax.dot_general(a, b, (((1,), (1,)), ((), ())), preferred_element_type=F32)


def _dot_tn(a, b):
    return lax.dot_general(a, b, (((0,), (0,)), ((), ())), preferred_element_type=F32)


def _row_bcast_blocks(src_ref, blk, off):
    parts = []
    for n in range(ROW_BLOCK // blk):
        row = src_ref[n * blk + off:n * blk + off + 1, :]
        parts.append(jnp.broadcast_to(row, (blk, HG_DK)))
    return parts[0] if len(parts) == 1 else jnp.concatenate(parts, axis=0)


def _mixer_kernel(*refs, NB, C, has_state):
    if has_state:
        x_ref, h0r_ref, h0i_ref, s0_ref = refs[:4]
        refs = refs[4:]
    else:
        x_ref = refs[0]
        refs = refs[1:]
    (perm_ref, permt_ref, vec_ref, win_ref, are_ref, aim_ref, bcat_ref, ccat_ref, wglu_ref, lb_ref, wout_ref,
     y_ref, hr_ref, hi_ref, s_ref,
     hn_ref, proj_ref, uz_ref, bu_ref, s5_ref, o_ref, qe_ref, ke_ref, v_ref, bsall_ref, lev_ref, tri_ref,
     dl_ref, cat_ref) = refs

    ng_ref, fg_ref, d_ref, bglu_ref, og_ref = (vec_ref.at[:, a:b] for a, b in _VEC_SLICES)

    j = pl.program_id(1)
    nj = pl.num_programs(1)
    R = NB * C
    n_rb = R // ROW_BLOCK
    nbat = ROW_BLOCK // C

    @pl.when(j == 0)
    def _init():
        if has_state:
            hr_ref[...] = h0r_ref[...]
            hi_ref[...] = h0i_ref[...]
        else:
            hr_ref[...] = jnp.zeros(hr_ref.shape, F32)
            hi_ref[...] = jnp.zeros(hi_ref.shape, F32)
            s_ref[...] = jnp.zeros(s_ref.shape, F32)

    MM = min(R, MATMUL_ROWS)
    n_mb = R // MM
    mbat = MM // C

    def aligned(start, n):
        return start if isinstance(start, int) else pl.multiple_of(start, n)

    def rows_of(rb, n=MM):
        return pl.ds(aligned(rb * n, n), n)

    def loop(trips, body):
        if trips == 1:
            body(0, 0)
        else:
            lax.fori_loop(0, trips, body, 0)

    TB = ROW_BLOCK // NB

    def seq_major_block(src_ref, blk):
        assert isinstance(blk, int)
        if TB == C:
            return src_ref[blk * ROW_BLOCK:(blk + 1) * ROW_BLOCK, :]
        return jnp.concatenate([src_ref[b * C + blk * TB:b * C + (blk + 1) * TB, :] for b in range(NB)], axis=0)

    flat_rows = len(x_ref.shape) == 2

    def x_rows(mb):
        if flat_rows:
            return x_ref[rows_of(mb), :]
        return x_ref[pl.ds(mb * mbat, mbat)].reshape(MM, D_MODEL)

    def prenorm(x):
        return (x * lax.rsqrt(jnp.mean(x * x, axis=-1, keepdims=True) + EPS) * ng_ref[...]).astype(BF16)

    def hg_proj(hn, sec):
        return _dot(hn, win_ref[:, OFF_Q + SECTION * sec:OFF_Q + SECTION * (sec + 1)])

    def project_chunk():
        def body(rb, c):
            rows = rows_of(rb)
            hn = prenorm(x_rows(rb))
            hn_ref[rows, :] = hn
            for sec in range(HG_COLS // SECTION):
                proj_ref[rows, SECTION * sec:SECTION * (sec + 1)] = hg_proj(hn, sec)
            return c
        loop(n_mb, body)

    row = lax.broadcasted_iota(jnp.int32, (ROW_BLOCK, ROW_BLOCK), 0)
    col = lax.broadcasted_iota(jnp.int32, (ROW_BLOCK, ROW_BLOCK), 1)
    tin = row & (C - 1)
    levels = []
    m = 1
    while m < C:
        levels.append(m)
        m *= 2

    @pl.when(j == 0)
    def _masks():
        lev = jnp.where(row == col, 0, -1)
        for li, m in enumerate(levels):
            same = (row & ~(2 * m - 1)) == (col & ~(2 * m - 1))
            own = same & ((row & m) != 0) & ((col & m) == 0)
            lev = jnp.where(own, li + 1, lev)
        lev_ref[...] = lev
        in_chunk = (row & ~(C - 1)) == (col & ~(C - 1))
        tri_ref[...] = jnp.where(in_chunk & (col <= row), 1.0, 0.0).astype(BF16)

    def step_body():
        heads = range(HG_HEADS)
        hcs = [slice(HG_DK * hd, HG_DK * (hd + 1)) for hd in heads]
        scan_width = min(S5_HALF, (SUBLANES * S5_HALF) // NB)

        def hgrn_block(rb, mxu_fill):
            rows = rows_of(rb, ROW_BLOCK)
            lv = lev_ref[...]
            qs, ks, fs, vs, gcats = [], [], [], [], []
            for hd in heads:
                lb = lb_ref[:, hcs[hd]]
                q = proj_ref[rows, COL_Q + HG_DK * hd:COL_Q + HG_DK * (hd + 1)]
                gf = proj_ref[rows, COL_F + HG_DK * hd:COL_F + HG_DK * (hd + 1)]
                v = proj_ref[rows, COL_I + HG_DV * hd:COL_I + HG_DV * (hd + 1)].astype(BF16)
                f = lb + (1.0 - lb) * _sigmoid(gf)
                g = jnp.log2(f)
                k = 1.0 - f
                g1 = g.astype(BF16)
                r1 = g - g1.astype(F32)
                g2 = r1.astype(BF16)
                g3 = (r1 - g2.astype(F32)).astype(BF16)
                p4 = row & 3
                dec1 = jnp.where((row & 1) == 1, f, 1.0)
                dec2 = jnp.where(p4 == 0, pltpu.roll(f, ROW_BLOCK - 1, axis=0),
                                 jnp.where(p4 == 1, 1.0,
                                           jnp.where(p4 == 2, f, f * pltpu.roll(f, 1, axis=0))))
                qs.append(q)
                ks.append(k)
                fs.append((dec1.astype(BF16), dec2.astype(BF16)))
                vs.append(v)
                gcats.append(jnp.concatenate([g1, g2, g3], axis=1))
            bparts = [_dot(tri_ref[...], gcats[hd]) for hd in heads]
            bs = []
            for hd in heads:
                bp = bparts[hd]
                b = (bp[:, 0:HG_DK] + bp[:, HG_DK:2 * HG_DK]) + bp[:, 2 * HG_DK:3 * HG_DK]
                bsall_ref[hd] = b
                bs.append(b)
            qbs = [q.astype(BF16) for q in qs]
            kbs = [k.astype(BF16) for k in ks]
            atts = [jnp.where(lv == 0, _dot_nt(qbs[hd], kbs[hd]), 0.0) for hd in heads]
            for li, m in enumerate(levels):
                two_m = 2 * m
                p2 = row & (two_m - 1)
                if m > 2:
                    sgn = jnp.where(p2 >= m, 1.0, -1.0)
                for hd in heads:
                    if m <= 2:
                        decb = fs[hd][li]
                    else:
                        ref = _row_bcast_blocks(bsall_ref.at[hd], two_m, m - 1)
                        decb = jnp.exp2((bs[hd] - ref) * sgn).astype(BF16)
                    atts[hd] = jnp.where(lv == li + 1, _dot_nt(qbs[hd] * decb, kbs[hd] * decb), atts[hd])
                if li == 0:
                    for piece in mxu_fill:
                        piece()
            for hd in heads:
                o_ref[rows, hcs[hd]] = _dot(atts[hd].astype(BF16), vs[hd])
            for hd in heads:
                hc = hcs[hd]
                bs_ref = bsall_ref.at[hd]
                b = bs[hd]
                if C >= SUBLANES:
                    b_last = _row_bcast_blocks(bs_ref, C, C - 1)
                else:
                    b_last = b
                    for d in range(1, C):
                        b_last = jnp.where(tin == C - 1 - d, pltpu.roll(b, ROW_BLOCK - d, axis=0), b_last)
                qe_ref[rows, hc] = (qs[hd] * jnp.exp2(b)).astype(qe_ref.dtype)
                ke_ref[rows, hc] = (ks[hd] * jnp.exp2(b_last - b)).astype(ke_ref.dtype)
                v_ref[rows, hc] = vs[hd].astype(v_ref.dtype)
                for n in range(nbat):
                    last = bs_ref[n * C + C - 1:n * C + C, :]
                    dl_ref[rb * nbat + n, :, hc] = jnp.broadcast_to(jnp.exp2(last), (SUBLANES, HG_DK))

        SR = min(R, S5_INPUT_ROWS)
        sblocks = SR // ROW_BLOCK

        def merged(it, c):
            rows = rows_of(it, SR)
            hp = jnp.concatenate(
                [_dot(perm_ref[...], seq_major_block(hn_ref, it * sblocks + sb)).astype(BF16)
                 for sb in range(sblocks)], axis=0)
            uz = {}

            def uz_piece(sec):
                uz[sec] = _dot(hp, win_ref[:, W_S5 * sec:W_S5 * (sec + 1)])
                uz_ref[rows, W_S5 * sec:W_S5 * (sec + 1)] = uz[sec]

            def bu_piece(i):
                u_i = uz[0][:, W_S5 // 2 * i:W_S5 // 2 * (i + 1)].astype(BF16)
                bu_ref[rows, 2 * S5_HALF * i:2 * S5_HALF * (i + 1)] = _dot(u_i, bcat_ref[i])

            pieces = [functools.partial(uz_piece, 0), functools.partial(uz_piece, 1),
                      functools.partial(bu_piece, 0), functools.partial(bu_piece, 1)]
            per_block = len(pieces) // sblocks
            for sb in range(sblocks):
                hgrn_block(it * sblocks + sb, pieces[sb * per_block:(sb + 1) * per_block])
            for i in range(2):
                for off in range(0, S5_HALF, scan_width):
                    st = slice(S5_HALF * i + off, S5_HALF * i + off + scan_width)
                    re_c = 2 * S5_HALF * i + off
                    im_c = re_c + S5_HALF
                    ar = are_ref[:, st]
                    ai = aim_ref[:, st]
                    hr = hr_ref[:, st]
                    hi = hi_ref[:, st]
                    for tt in range(TB * sblocks):
                        trows = pl.ds(aligned(it * SR + tt * NB, NB), NB)
                        hr, hi = (ar * hr - ai * hi + bu_ref[trows, re_c:re_c + scan_width],
                                  ar * hi + ai * hr + bu_ref[trows, im_c:im_c + scan_width])
                        bu_ref[trows, re_c:re_c + scan_width] = hr
                        bu_ref[trows, im_c:im_c + scan_width] = hi
                    hr_ref[:, st] = hr
                    hi_ref[:, st] = hi
            return c
        assert R == SR
        loop(R // SR, merged)

        def inter_out(bi):
            srows = pl.ds(aligned(bi * C, C), C)
            for hd in heads:
                hc = hcs[hd]
                if has_state:
                    oi = _dot(qe_ref[srows, hc], s0_ref[bi, hd].astype(qe_ref.dtype))
                else:
                    oi = _dot_nt(qe_ref[srows, hc], s_ref[bi, hd].astype(qe_ref.dtype))
                o_ref[srows, hc] = o_ref[srows, hc] + oi

        def inter_state(bi):
            srows = pl.ds(aligned(bi * C, C), C)
            for hd in heads:
                hc = hcs[hd]
                if has_state:
                    upd = _dot_tn(ke_ref[srows, hc], v_ref[srows, hc])
                    s_ref[bi, hd] = s0_ref[bi, hd] * dl_ref[bi, :, hc].T[:, 0:1] + upd
                else:
                    upd = _dot_tn(v_ref[srows, hc], ke_ref[srows, hc])
                    s_ref[bi, hd] = s_ref[bi, hd] * dl_ref[bi, 0:1, hc] + upd

        RR = min(R, READOUT_ROWS)
        rbat = RR // C
        merge_inter = rbat <= 8

        def readout_inter(rb, c):
            rows = rows_of(rb, RR)
            ys = []
            for i in range(2):
                hs = bu_ref[rows, 2 * S5_HALF * i:2 * S5_HALF * (i + 1)].astype(BF16)
                ys.append(_dot(hs, ccat_ref[i]))
            if merge_inter:
                for n in range(rbat):
                    inter_out(rb * rbat + n)
            u = uz_ref[rows, 0:W_S5]
            y = jnp.concatenate(ys, axis=1) + d_ref[...] * u
            g = _gelu_tanh(y)
            glu = _dot(g.astype(BF16), wglu_ref[...]) + bglu_ref[...]
            if merge_inter:
                for n in range(rbat):
                    inter_state(rb * rbat + n)
            zs = uz_ref[rows, W_S5:2 * W_S5]
            s5 = g * _sigmoid(glu) * (zs * _sigmoid(zs))
            s5_ref[rows, :] = s5.astype(BF16)
            return c
        loop(R // RR, readout_inter)

        if not merge_inter and C >= SUBLANES:
            def inter(bi, c):
                inter_out(bi)
                inter_state(bi)
                return c
            lax.fori_loop(0, NB, inter, 0, unroll=2)

        if C < SUBLANES:
            assert has_state and not merge_inter
            per_tile = SUBLANES // C
            rid = lax.broadcasted_iota(jnp.int32, (SUBLANES, HG_DK), 0)
            cid = lax.broadcasted_iota(jnp.int32, (HG_DK, SUBLANES), 1)

            def inter_tile(ti, c):
                srows = pl.ds(pl.multiple_of(ti * SUBLANES, SUBLANES), SUBLANES)
                for hd in heads:
                    hc = hcs[hd]
                    qe8 = qe_ref[srows, hc]
                    ke_t = ke_ref[srows, hc].T
                    v8 = v_ref[srows, hc]
                    acc = o_ref[srows, hc]
                    dec = dl_ref[ti * per_tile + per_tile - 1, :, hc]
                    for s in range(per_tile - 1):
                        dec = jnp.where(rid == s, dl_ref[ti * per_tile + s, :, hc], dec)
                    dec_t = dec.T
                    for s in range(per_tile):
                        bi = ti * per_tile + s
                        st = s0_ref[bi, hd]
                        rows_s = (rid >= s * C) & (rid < (s + 1) * C)
                        cols_s = (cid >= s * C) & (cid < (s + 1) * C)
                        acc = acc + _dot(jnp.where(rows_s, qe8, 0.0), st)
                        upd = _dot(jnp.where(cols_s, ke_t, 0.0), v8)
                        s_ref[bi, hd] = st * dec_t[:, s:s + 1] + upd
                    o_ref[srows, hc] = acc
                return c
            lax.fori_loop(0, NB // per_tile, inter_tile, 0, unroll=4)

        for blk in range(n_rb):
            back = _dot(permt_ref[...], s5_ref[blk * ROW_BLOCK:(blk + 1) * ROW_BLOCK, :]).astype(BF16)
            if TB == C:
                cat_ref[blk * ROW_BLOCK:(blk + 1) * ROW_BLOCK, 0:W_S5] = back
            else:
                for b in range(NB):
                    cat_ref[b * C + blk * TB:b * C + (blk + 1) * TB, 0:W_S5] = back[b * TB:(b + 1) * TB, :]

        def phase5(rb, c):
            rows = rows_of(rb)
            for hd in range(HG_HEADS):
                hc = slice(HG_DV * hd, HG_DV * (hd + 1))
                o = o_ref[rows, hc]
                o = o * lax.rsqrt(jnp.mean(o * o, axis=-1, keepdims=True) + EPS)
                zh = proj_ref[rows, COL_ZH + HG_DV * hd:COL_ZH + HG_DV * (hd + 1)]
                hg = o * og_ref[:, hc] * (zh * _sigmoid(zh))
                cat_ref[rows, W_S5 + HG_DV * hd:W_S5 + HG_DV * (hd + 1)] = hg.astype(BF16)
            out = _dot(cat_ref[rows, :], wout_ref[...])
            xn = x_rows(rb) + out
            yv = xn * lax.rsqrt(jnp.mean(xn * xn, axis=-1, keepdims=True) + EPS) * fg_ref[...]
            if flat_rows:
                y_ref[rows, :] = yv
            else:
                y_ref[pl.ds(rb * mbat, mbat)] = yv.reshape(mbat, C, D_MODEL)
            return c
        loop(n_mb, phase5)

    project_chunk()
    step_body()

    if not has_state:
        @pl.when(j == nj - 1)
        def _fin():
            def tr(b, c):
                for hd in range(HG_HEADS):
                    s_ref[b, hd] = s_ref[b, hd].T
                return c
            lax.fori_loop(0, NB, tr, 0)


_VEC_WIDTHS = (D_MODEL, D_MODEL, W_S5, W_S5, W_HG)
_VEC_SLICES = tuple((sum(_VEC_WIDTHS[:n]), sum(_VEC_WIDTHS[:n + 1])) for n in range(len(_VEC_WIDTHS)))


def _pack_vectors(norm_g, final_norm_g, d, b_glu, onorm_g):
    parts = [norm_g, final_norm_g, d.reshape(-1), b_glu, onorm_g]
    assert tuple(p.shape[0] for p in parts) == _VEC_WIDTHS
    return jnp.concatenate([p.astype(F32) for p in parts]).reshape(1, sum(_VEC_WIDTHS))


def _time_major_perm(NB):
    steps = ROW_BLOCK // NB
    dst = np.arange(ROW_BLOCK)
    src = (dst % NB) * steps + dst // NB
    p = np.zeros((ROW_BLOCK, ROW_BLOCK), np.float32)
    p[dst, src] = 1.0
    return p


def _mixer_call(x, state, weights, *, NB, C, name):
    B, L, _ = x.shape
    has_state = state is not None
    assert not has_state or L == C, "a given state is consumed by a single chunk per sequence"
    grid = (B // NB, L // C)
    R = NB * C
    perm = _time_major_perm(NB)
    weights = (jnp.asarray(perm, BF16), jnp.asarray(perm.T, BF16)) + tuple(weights)

    def full(a):
        nd = a.ndim
        return pl.BlockSpec(a.shape, lambda i, j, nd=nd: (0,) * nd, pipeline_mode=pl.Buffered(1))

    flat_rows = C % SUBLANES != 0
    if flat_rows:
        assert L == C
        x = x.reshape(B * L, D_MODEL)
        x_spec = pl.BlockSpec((R, D_MODEL), lambda i, j: (i, 0))
    else:
        x_spec = pl.BlockSpec((NB, C, D_MODEL), lambda i, j: (i, j, 0))
    h_spec = pl.BlockSpec((NB, S5_LANES), lambda i, j: (i, 0))
    s_spec = pl.BlockSpec((NB, HG_HEADS, HG_DK, HG_DV), lambda i, j: (i, 0, 0, 0))
    in_specs = [x_spec] + ([h_spec, h_spec, s_spec] if has_state else []) + [full(w) for w in weights]
    args = [x] + (list(state) if has_state else []) + list(weights)
    out_shape = [
        jax.ShapeDtypeStruct(x.shape, x.dtype),
        jax.ShapeDtypeStruct((B, S5_LANES), F32),
        jax.ShapeDtypeStruct((B, S5_LANES), F32),
        jax.ShapeDtypeStruct((B, HG_HEADS, HG_DK, HG_DV), F32),
    ]
    chunk_dt = BF16 if C % BF16_TILE_ROWS == 0 else F32
    scratch = [
        pltpu.VMEM((R, D_MODEL), BF16),
        pltpu.VMEM((R, HG_COLS), F32),
        pltpu.VMEM((R, 2 * W_S5), F32),
        pltpu.VMEM((R, 2 * S5_LANES), F32),
        pltpu.VMEM((R, W_S5), BF16),
        pltpu.VMEM((R, W_HG), F32),
        pltpu.VMEM((R, W_HG), chunk_dt),
        pltpu.VMEM((R, W_HG), chunk_dt),
        pltpu.VMEM((R, W_HG), chunk_dt),
        pltpu.VMEM((HG_HEADS, ROW_BLOCK, HG_DK), F32),
        pltpu.VMEM((ROW_BLOCK, ROW_BLOCK), jnp.int32),
        pltpu.VMEM((ROW_BLOCK, ROW_BLOCK), BF16),
        pltpu.VMEM((NB, SUBLANES, W_HG), F32),
        pltpu.VMEM((R, D_MODEL), BF16),
    ]
    return pl.pallas_call(
        functools.partial(_mixer_kernel, NB=NB, C=C, has_state=has_state),
        grid=grid,
        in_specs=in_specs,
        out_specs=[x_spec, h_spec, h_spec, s_spec],
        out_shape=out_shape,
        scratch_shapes=scratch,
        compiler_params=pltpu.CompilerParams(
            dimension_semantics=("arbitrary", "arbitrary"),
            vmem_limit_bytes=V7X_VMEM_LIMIT),
        name=name,
    )(*args)


def kernel(x_prompt, x_sample, state_s5_re, state_s5_im, state_hgrn, norm_g, w_in, s5_lambda_re, s5_lambda_im,
           s5_log_step, s5_b_re, s5_b_im, s5_c_re, s5_c_im, s5_d, w_glu, b_glu, hgrn_lb_logits, hgrn_onorm_g,
           w_out, final_norm_g):
    assert norm_g.shape[0] == 1, "single-layer model"
    l = 0
    a_re, a_im, bbre, bbim, lb = _prepare_params(
        s5_lambda_re[l], s5_lambda_im[l], s5_log_step[l], s5_b_re[l], s5_b_im[l], hgrn_lb_logits)
    bcat, ccat = _block_diag_weights(bbre, bbim, s5_c_re[l], s5_c_im[l])
    weights = (
        _pack_vectors(norm_g[l], final_norm_g, s5_d[l], b_glu[l], hgrn_onorm_g[l]),
        w_in[l].astype(BF16),
        a_re, a_im, bcat, ccat,
        w_glu[l].astype(BF16),
        lb,
        w_out[l].astype(BF16),
    )

    Bp, Lp, _ = x_prompt.shape
    yp, p_re, p_im, p_hg = _mixer_call(x_prompt, None, weights, NB=Bp, C=PROMPT_CHUNK, name="mixer_prompt")

    Bs, Ls, _ = x_sample.shape
    state = (state_s5_re[l].reshape(Bs, S5_LANES).astype(F32),
             state_s5_im[l].reshape(Bs, S5_LANES).astype(F32),
             state_hgrn[l].astype(F32))
    ys, s_re, s_im, s_hg = _mixer_call(x_sample, state, weights, NB=ROW_BLOCK // Ls, C=Ls, name="mixer_sample")
    ys = ys.reshape(Bs, Ls, D_MODEL)

    sd = state_s5_re.dtype
    shp = (1, Bp, S5_GROUPS, S5_STATE)
    shs = (1, Bs, S5_GROUPS, S5_STATE)
    return (yp, ys,
            p_re.reshape(shp).astype(sd), p_im.reshape(shp).astype(sd), p_hg[None].astype(state_hgrn.dtype),
            s_re.reshape(shs).astype(sd), s_im.reshape(shs).astype(sd), s_hg[None].astype(state_hgrn.dtype))
```

```python
import functools
import math

import jax
import jax.numpy as jnp
import numpy as np
from jax import lax
from jax.experimental import pallas as pl
from jax.experimental.pallas import tpu as pltpu

F32 = jnp.float32
BF16 = jnp.bfloat16

D_MODEL = 1024
W_S5 = 512
W_HG = 512
S5_CH = 16
S5_GROUPS = 32
S5_STATE = 64
S5_LANES = S5_GROUPS * S5_STATE
S5_HALF = S5_LANES // 2
HG_DK = 128
HG_HEADS = 4
HG_DV = 128
EPS = 1e-6
LAMBDA_RE_MAX = -1e-4
PROJ_OUT = 3072
OFF_Q = 1024
HG_COLS = PROJ_OUT - OFF_Q
SECTION = 512
COL_Q, COL_F, COL_I, COL_ZH = (SECTION * n for n in range(4))
PROMPT_CHUNK = 64
LOG2E = 1.4426950408889634

LANES = 128
SUBLANES = 8
BF16_TILE_ROWS = 16
ROW_BLOCK = 128
MATMUL_ROWS = 512
READOUT_ROWS = 512
S5_INPUT_ROWS = 512
V7X_VMEM_LIMIT = 60 * 1024 * 1024


def _prep_kernel(lre_ref, lim_ref, ls_ref, brt_ref, bit_ref, lg_ref,
                 are_ref, aim_ref, bbre_ref, bbim_ref, lb_ref):
    lr = jnp.minimum(lre_ref[...], LAMBDA_RE_MAX)
    li = lim_ref[...]
    dt = jnp.exp(ls_ref[...])
    er = jnp.exp(lr * dt)
    a_re = er * jnp.cos(li * dt)
    a_im = er * jnp.sin(li * dt)
    den = lr * lr + li * li
    zr = ((a_re - 1.0) * lr + a_im * li) / den
    zi = (a_im * lr - (a_re - 1.0) * li) / den
    for g in range(S5_GROUPS):
        cols = slice(S5_STATE * g, S5_STATE * (g + 1))
        are_ref[:, cols] = a_re[g:g + 1, :]
        aim_ref[:, cols] = a_im[g:g + 1, :]
        br = brt_ref[g]
        bi = bit_ref[g]
        zr_g = zr[g:g + 1, :]
        zi_g = zi[g:g + 1, :]
        bbre_ref[g] = zr_g * br - zi_g * bi
        bbim_ref[g] = zr_g * bi + zi_g * br
    lg = lg_ref[...]
    e = jnp.exp(lg - jnp.max(lg, axis=0, keepdims=True))
    lb_ref[...] = e[0:1, :] / jnp.sum(e, axis=0, keepdims=True)


def _prepare_params(lam_re, lam_im, log_step, b_re, b_im, lb_logits):
    gps = jax.ShapeDtypeStruct((1, S5_LANES), F32)
    gcs = jax.ShapeDtypeStruct((S5_GROUPS, S5_CH, S5_STATE), F32)
    return pl.pallas_call(
        _prep_kernel,
        out_shape=[gps, gps, gcs, gcs, jax.ShapeDtypeStruct((1, HG_HEADS * HG_DK), F32)],
        name="s5_hgrn_prep",
    )(lam_re.astype(F32), lam_im.astype(F32), log_step.astype(F32).reshape(S5_GROUPS, 1),
      jnp.transpose(b_re.astype(F32), (0, 2, 1)), jnp.transpose(b_im.astype(F32), (0, 2, 1)),
      lb_logits.astype(F32))


def _block_diag_weights(bbre, bbim, c_re, c_im):
    hg = S5_GROUPS // 2
    eye = jnp.eye(hg, dtype=BF16)[None, :, None, :, None]

    def in_half(bbt):
        x = bbt.astype(BF16).reshape(2, hg, S5_CH, 1, S5_STATE) * eye
        return x.reshape(2, hg * S5_CH, hg * S5_STATE)

    def out_half(c):
        ct = jnp.transpose(c.astype(BF16), (0, 2, 1)).reshape(2, hg, S5_STATE, 1, S5_CH)
        return (ct * eye).reshape(2, hg * S5_STATE, hg * S5_CH)

    bcat = jnp.concatenate([in_half(bbre), in_half(bbim)], axis=2)
    ccat = jnp.concatenate([out_half(c_re), out_half(-c_im)], axis=1)
    return bcat, ccat


def _sigmoid(x):
    return 1.0 / (1.0 + jnp.exp2(x * (-LOG2E)))


def _gelu_tanh(x):
    return 0.5 * x * (1.0 + jnp.tanh(math.sqrt(2.0 / math.pi) * (x + 0.044715 * (x * x * x))))


def _dot(a, b):
    return jnp.dot(a, b, preferred_element_type=F32)


def _dot_nt(a, b):
    return lax.dot_general(a, b, (((1,), (1,)), ((), ())), preferred_element_type=F32)


def _dot_tn(a, b):
    return lax.dot_general(a, b, (((0,), (0,)), ((), ())), preferred_element_type=F32)


def _row_bcast_blocks(src_ref, blk, off):
    parts = []
    for n in range(ROW_BLOCK // blk):
        row = src_ref[n * blk + off:n * blk + off + 1, :]
        parts.append(jnp.broadcast_to(row, (blk, HG_DK)))
    return parts[0] if len(parts) == 1 else jnp.concatenate(parts, axis=0)


def _mixer_kernel(*refs, NB, C, has_state):
    if has_state:
        x_ref, h0r_ref, h0i_ref, s0_ref = refs[:4]
        refs = refs[4:]
    else:
        x_ref = refs[0]
        refs = refs[1:]
    (perm_ref, permt_ref, vec_ref, win_ref, are_ref, aim_ref, bcat_ref, ccat_ref, wglu_ref, lb_ref, wout_ref,
     y_ref, hr_ref, hi_ref, s_ref,
     hn_ref, proj_ref, uz_ref, bu_ref, s5_ref, o_ref, qe_ref, ke_ref, v_ref, bsall_ref, lev_ref, tri_ref,
     dl_ref, cat_ref) = refs

    ng_ref, fg_ref, d_ref, bglu_ref, og_ref = (vec_ref.at[:, a:b] for a, b in _VEC_SLICES)

    j = pl.program_id(1)
    nj = pl.num_programs(1)
    R = NB * C
    n_rb = R // ROW_BLOCK
    nbat = ROW_BLOCK // C

    @pl.when(j == 0)
    def _init():
        if has_state:
            hr_ref[...] = h0r_ref[...]
            hi_ref[...] = h0i_ref[...]
        else:
            hr_ref[...] = jnp.zeros(hr_ref.shape, F32)
            hi_ref[...] = jnp.zeros(hi_ref.shape, F32)
            s_ref[...] = jnp.zeros(s_ref.shape, F32)

    MM = min(R, MATMUL_ROWS)
    n_mb = R // MM
    mbat = MM // C

    def aligned(start, n):
        return start if isinstance(start, int) else pl.multiple_of(start, n)

    def rows_of(rb, n=MM):
        return pl.ds(aligned(rb * n, n), n)

    def loop(trips, body):
        if trips == 1:
            body(0, 0)
        else:
            lax.fori_loop(0, trips, body, 0)

    TB = ROW_BLOCK // NB

    def seq_major_block(src_ref, blk):
        assert isinstance(blk, int)
        if TB == C:
            return src_ref[blk * ROW_BLOCK:(blk + 1) * ROW_BLOCK, :]
        return jnp.concatenate([src_ref[b * C + blk * TB:b * C + (blk + 1) * TB, :] for b in range(NB)], axis=0)

    flat_rows = len(x_ref.shape) == 2

    def x_rows(mb):
        if flat_rows:
            return x_ref[rows_of(mb), :]
        return x_ref[pl.ds(mb * mbat, mbat)].reshape(MM, D_MODEL)

    def prenorm(x):
        return (x * lax.rsqrt(jnp.mean(x * x, axis=-1, keepdims=True) + EPS) * ng_ref[...]).astype(BF16)

    def hg_proj(hn, sec):
        return _dot(hn, win_ref[:, OFF_Q + SECTION * sec:OFF_Q + SECTION * (sec + 1)])

    def project_chunk():
        def body(rb, c):
            rows = rows_of(rb)
            hn = prenorm(x_rows(rb))
            hn_ref[rows, :] = hn
            for sec in range(HG_COLS // SECTION):
                proj_ref[rows, SECTION * sec:SECTION * (sec + 1)] = hg_proj(hn, sec)
            return c
        loop(n_mb, body)

    row = lax.broadcasted_iota(jnp.int32, (ROW_BLOCK, ROW_BLOCK), 0)
    col = lax.broadcasted_iota(jnp.int32, (ROW_BLOCK, ROW_BLOCK), 1)
    tin = row & (C - 1)
    levels = []
    m = 1
    while m < C:
        levels.append(m)
        m *= 2

    @pl.when(j == 0)
    def _masks():
        lev = jnp.where(row == col, 0, -1)
        for li, m in enumerate(levels):
            same = (row & ~(2 * m - 1)) == (col & ~(2 * m - 1))
            own = same & ((row & m) != 0) & ((col & m) == 0)
            lev = jnp.where(own, li + 1, lev)
        lev_ref[...] = lev
        in_chunk = (row & ~(C - 1)) == (col & ~(C - 1))
        tri_ref[...] = jnp.where(in_chunk & (col <= row), 1.0, 0.0).astype(BF16)

    def step_body():
        heads = range(HG_HEADS)
        hcs = [slice(HG_DK * hd, HG_DK * (hd + 1)) for hd in heads]
        scan_width = min(S5_HALF // 2, (SUBLANES * S5_HALF) // NB)

        def hgrn_block(rb, mxu_fill):
            rows = rows_of(rb, ROW_BLOCK)
            lv = lev_ref[...]
            qs, ks, fs, vs, gcats = [], [], [], [], []
            for hd in heads:
                lb = lb_ref[:, hcs[hd]]
                q = proj_ref[rows, COL_Q + HG_DK * hd:COL_Q + HG_DK * (hd + 1)]
                gf = proj_ref[rows, COL_F + HG_DK * hd:COL_F + HG_DK * (hd + 1)]
                v = proj_ref[rows, COL_I + HG_DV * hd:COL_I + HG_DV * (hd + 1)].astype(BF16)
                f = lb + (1.0 - lb) * _sigmoid(gf)
                g = jnp.log2(f)
                k = 1.0 - f
                g1 = g.astype(BF16)
                r1 = g - g1.astype(F32)
                g2 = r1.astype(BF16)
                g3 = (r1 - g2.astype(F32)).astype(BF16)
                p4 = row & 3
                dec1 = jnp.where((row & 1) == 1, f, 1.0)
                dec2 = jnp.where(p4 == 0, pltpu.roll(f, ROW_BLOCK - 1, axis=0),
                                 jnp.where(p4 == 1, 1.0,
                                           jnp.where(p4 == 2, f, f * pltpu.roll(f, 1, axis=0))))
                qs.append(q)
                ks.append(k)
                fs.append((dec1.astype(BF16), dec2.astype(BF16)))
                vs.append(v)
                gcats.append(jnp.concatenate([g1, g2, g3], axis=1))
            bparts = [_dot(tri_ref[...], gcats[hd]) for hd in heads]
            bs = []
            for hd in heads:
                bp = bparts[hd]
                b = (bp[:, 0:HG_DK] + bp[:, HG_DK:2 * HG_DK]) + bp[:, 2 * HG_DK:3 * HG_DK]
                bsall_ref[hd] = b
                bs.append(b)
            qbs = [q.astype(BF16) for q in qs]
            kbs = [k.astype(BF16) for k in ks]
            atts = [jnp.where(lv == 0, _dot_nt(qbs[hd], kbs[hd]), 0.0) for hd in heads]
            for li, m in enumerate(levels):
                two_m = 2 * m
                p2 = row & (two_m - 1)
                if m > 2:
                    sgn = jnp.where(p2 >= m, 1.0, -1.0)
                for hd in heads:
                    if m <= 2:
                        decb = fs[hd][li]
                    else:
                        ref = _row_bcast_blocks(bsall_ref.at[hd], two_m, m - 1)
                        decb = jnp.exp2((bs[hd] - ref) * sgn).astype(BF16)
                    atts[hd] = jnp.where(lv == li + 1, _dot_nt(qbs[hd] * decb, kbs[hd] * decb), atts[hd])
                if li == 0:
                    for piece in mxu_fill:
                        piece()
            for hd in heads:
                o_ref[rows, hcs[hd]] = _dot(atts[hd].astype(BF16), vs[hd])
            for hd in heads:
                hc = hcs[hd]
                bs_ref = bsall_ref.at[hd]
                b = bs[hd]
                if C >= SUBLANES:
                    b_last = _row_bcast_blocks(bs_ref, C, C - 1)
                else:
                    b_last = b
                    for d in range(1, C):
                        b_last = jnp.where(tin == C - 1 - d, pltpu.roll(b, ROW_BLOCK - d, axis=0), b_last)
                qe_ref[rows, hc] = (qs[hd] * jnp.exp2(b)).astype(qe_ref.dtype)
                ke_ref[rows, hc] = (ks[hd] * jnp.exp2(b_last - b)).astype(ke_ref.dtype)
                v_ref[rows, hc] = vs[hd].astype(v_ref.dtype)
                for n in range(nbat):
                    last = bs_ref[n * C + C - 1:n * C + C, :]
                    dl_ref[rb * nbat + n, :, hc] = jnp.broadcast_to(jnp.exp2(last), (SUBLANES, HG_DK))

        SR = min(R, S5_INPUT_ROWS)
        sblocks = SR // ROW_BLOCK

        def merged(it, c):
            rows = rows_of(it, SR)
            hp = jnp.concatenate(
                [_dot(perm_ref[...], seq_major_block(hn_ref, it * sblocks + sb)).astype(BF16)
                 for sb in range(sblocks)], axis=0)
            uz = {}

            def uz_piece(sec):
                uz[sec] = _dot(hp, win_ref[:, W_S5 * sec:W_S5 * (sec + 1)])
                uz_ref[rows, W_S5 * sec:W_S5 * (sec + 1)] = uz[sec]

            def bu_piece(i):
                u_i = uz[0][:, W_S5 // 2 * i:W_S5 // 2 * (i + 1)].astype(BF16)
                bu_ref[rows, 2 * S5_HALF * i:2 * S5_HALF * (i + 1)] = _dot(u_i, bcat_ref[i])

            pieces = [functools.partial(uz_piece, 0), functools.partial(uz_piece, 1),
                      functools.partial(bu_piece, 0), functools.partial(bu_piece, 1)]
            per_block = len(pieces) // sblocks
            for sb in range(sblocks):
                hgrn_block(it * sblocks + sb, pieces[sb * per_block:(sb + 1) * per_block])
            for i in range(2):
                for off in range(0, S5_HALF, scan_width):
                    st = slice(S5_HALF * i + off, S5_HALF * i + off + scan_width)
                    re_c = 2 * S5_HALF * i + off
                    im_c = re_c + S5_HALF
                    ar = are_ref[:, st]
                    ai = aim_ref[:, st]
                    hr = hr_ref[:, st]
                    hi = hi_ref[:, st]
                    for tt in range(TB * sblocks):
                        trows = pl.ds(aligned(it * SR + tt * NB, NB), NB)
                        hr, hi = (ar * hr - ai * hi + bu_ref[trows, re_c:re_c + scan_width],
                                  ar * hi + ai * hr + bu_ref[trows, im_c:im_c + scan_width])
                        bu_ref[trows, re_c:re_c + scan_width] = hr
                        bu_ref[trows, im_c:im_c + scan_width] = hi
                    hr_ref[:, st] = hr
                    hi_ref[:, st] = hi
            return c
        assert R == SR
        loop(R // SR, merged)

        def inter_out(bi):
            srows = pl.ds(aligned(bi * C, C), C)
            for hd in heads:
                hc = hcs[hd]
                if has_state:
                    oi = _dot(qe_ref[srows, hc], s0_ref[bi, hd].astype(qe_ref.dtype))
                else:
                    oi = _dot_nt(qe_ref[srows, hc], s_ref[bi, hd].astype(qe_ref.dtype))
                o_ref[srows, hc] = o_ref[srows, hc] + oi

        def inter_state(bi):
            srows = pl.ds(aligned(bi * C, C), C)
            for hd in heads:
                hc = hcs[hd]
                if has_state:
                    upd = _dot_tn(ke_ref[srows, hc], v_ref[srows, hc])
                    s_ref[bi, hd] = s0_ref[bi, hd] * dl_ref[bi, :, hc].T[:, 0:1] + upd
                else:
                    upd = _dot_tn(v_ref[srows, hc], ke_ref[srows, hc])
                    s_ref[bi, hd] = s_ref[bi, hd] * dl_ref[bi, 0:1, hc] + upd

        RR = min(R, READOUT_ROWS)
        rbat = RR // C
        merge_inter = rbat <= 8

        def readout_inter(rb, c):
            rows = rows_of(rb, RR)
            ys = []
            for i in range(2):
                hs = bu_ref[rows, 2 * S5_HALF * i:2 * S5_HALF * (i + 1)].astype(BF16)
                ys.append(_dot(hs, ccat_ref[i]))
            if merge_inter:
                for n in range(rbat):
                    inter_out(rb * rbat + n)
            u = uz_ref[rows, 0:W_S5]
            y = jnp.concatenate(ys, axis=1) + d_ref[...] * u
            g = _gelu_tanh(y)
            glu = _dot(g.astype(BF16), wglu_ref[...]) + bglu_ref[...]
            if merge_inter:
                for n in range(rbat):
                    inter_state(rb * rbat + n)
            zs = uz_ref[rows, W_S5:2 * W_S5]
            s5 = g * _sigmoid(glu) * (zs * _sigmoid(zs))
            s5_ref[rows, :] = s5.astype(BF16)
            return c
        loop(R // RR, readout_inter)

        if not merge_inter and C >= SUBLANES:
            def inter(bi, c):
                inter_out(bi)
                inter_state(bi)
                return c
            lax.fori_loop(0, NB, inter, 0, unroll=2)

        if C < SUBLANES:
            assert has_state and not merge_inter
            per_tile = SUBLANES // C
            rid = lax.broadcasted_iota(jnp.int32, (SUBLANES, HG_DK), 0)
            cid = lax.broadcasted_iota(jnp.int32, (HG_DK, SUBLANES), 1)

            def inter_tile(ti, c):
                srows = pl.ds(pl.multiple_of(ti * SUBLANES, SUBLANES), SUBLANES)
                for hd in heads:
                    hc = hcs[hd]
                    qe8 = qe_ref[srows, hc]
                    ke_t = ke_ref[srows, hc].T
                    v8 = v_ref[srows, hc]
                    acc = o_ref[srows, hc]
                    dec = dl_ref[ti * per_tile + per_tile - 1, :, hc]
                    for s in range(per_tile - 1):
                        dec = jnp.where(rid == s, dl_ref[ti * per_tile + s, :, hc], dec)
                    dec_t = dec.T
                    for s in range(per_tile):
                        bi = ti * per_tile + s
                        st = s0_ref[bi, hd]
                        rows_s = (rid >= s * C) & (rid < (s + 1) * C)
                        cols_s = (cid >= s * C) & (cid < (s + 1) * C)
                        acc = acc + _dot(jnp.where(rows_s, qe8, 0.0), st)
                        upd = _dot(jnp.where(cols_s, ke_t, 0.0), v8)
                        s_ref[bi, hd] = st * dec_t[:, s:s + 1] + upd
                    o_ref[srows, hc] = acc
                return c
            lax.fori_loop(0, NB // per_tile, inter_tile, 0, unroll=4)

        for blk in range(n_rb):
            back = _dot(permt_ref[...], s5_ref[blk * ROW_BLOCK:(blk + 1) * ROW_BLOCK, :]).astype(BF16)
            if TB == C:
                cat_ref[blk * ROW_BLOCK:(blk + 1) * ROW_BLOCK, 0:W_S5] = back
            else:
                for b in range(NB):
                    cat_ref[b * C + blk * TB:b * C + (blk + 1) * TB, 0:W_S5] = back[b * TB:(b + 1) * TB, :]

        def phase5(rb, c):
            rows = rows_of(rb)
            for hd in range(HG_HEADS):
                hc = slice(HG_DV * hd, HG_DV * (hd + 1))
                o = o_ref[rows, hc]
                o = o * lax.rsqrt(jnp.mean(o * o, axis=-1, keepdims=True) + EPS)
                zh = proj_ref[rows, COL_ZH + HG_DV * hd:COL_ZH + HG_DV * (hd + 1)]
                hg = o * og_ref[:, hc] * (zh * _sigmoid(zh))
                cat_ref[rows, W_S5 + HG_DV * hd:W_S5 + HG_DV * (hd + 1)] = hg.astype(BF16)
            out = _dot(cat_ref[rows, :], wout_ref[...])
            xn = x_rows(rb) + out
            yv = xn * lax.rsqrt(jnp.mean(xn * xn, axis=-1, keepdims=True) + EPS) * fg_ref[...]
            if flat_rows:
                y_ref[rows, :] = yv
            else:
                y_ref[pl.ds(rb * mbat, mbat)] = yv.reshape(mbat, C, D_MODEL)
            return c
        loop(n_mb, phase5)

    project_chunk()
    step_body()

    if not has_state:
        @pl.when(j == nj - 1)
        def _fin():
            def tr(b, c):
                for hd in range(HG_HEADS):
                    s_ref[b, hd] = s_ref[b, hd].T
                return c
            lax.fori_loop(0, NB, tr, 0)


_VEC_WIDTHS = (D_MODEL, D_MODEL, W_S5, W_S5, W_HG)
_VEC_SLICES = tuple((sum(_VEC_WIDTHS[:n]), sum(_VEC_WIDTHS[:n + 1])) for n in range(len(_VEC_WIDTHS)))


def _pack_vectors(norm_g, final_norm_g, d, b_glu, onorm_g):
    parts = [norm_g, final_norm_g, d.reshape(-1), b_glu, onorm_g]
    assert tuple(p.shape[0] for p in parts) == _VEC_WIDTHS
    return jnp.concatenate([p.astype(F32) for p in parts]).reshape(1, sum(_VEC_WIDTHS))


def _time_major_perm(NB):
    steps = ROW_BLOCK // NB
    dst = np.arange(ROW_BLOCK)
    src = (dst % NB) * steps + dst // NB
    p = np.zeros((ROW_BLOCK, ROW_BLOCK), np.float32)
    p[dst, src] = 1.0
    return p


def _mixer_call(x, state, weights, *, NB, C, name):
    B, L, _ = x.shape
    has_state = state is not None
    assert not has_state or L == C, "a given state is consumed by a single chunk per sequence"
    grid = (B // NB, L // C)
    R = NB * C
    perm = _time_major_perm(NB)
    weights = (jnp.asarray(perm, BF16), jnp.asarray(perm.T, BF16)) + tuple(weights)

    def full(a):
        nd = a.ndim
        return pl.BlockSpec(a.shape, lambda i, j, nd=nd: (0,) * nd, pipeline_mode=pl.Buffered(1))

    flat_rows = C % SUBLANES != 0
    if flat_rows:
        assert L == C
        x = x.reshape(B * L, D_MODEL)
        x_spec = pl.BlockSpec((R, D_MODEL), lambda i, j: (i, 0))
    else:
        x_spec = pl.BlockSpec((NB, C, D_MODEL), lambda i, j: (i, j, 0))
    h_spec = pl.BlockSpec((NB, S5_LANES), lambda i, j: (i, 0))
    s_spec = pl.BlockSpec((NB, HG_HEADS, HG_DK, HG_DV), lambda i, j: (i, 0, 0, 0))
    in_specs = [x_spec] + ([h_spec, h_spec, s_spec] if has_state else []) + [full(w) for w in weights]
    args = [x] + (list(state) if has_state else []) + list(weights)
    out_shape = [
        jax.ShapeDtypeStruct(x.shape, x.dtype),
        jax.ShapeDtypeStruct((B, S5_LANES), F32),
        jax.ShapeDtypeStruct((B, S5_LANES), F32),
        jax.ShapeDtypeStruct((B, HG_HEADS, HG_DK, HG_DV), F32),
    ]
    chunk_dt = BF16 if C % BF16_TILE_ROWS == 0 else F32
    scratch = [
        pltpu.VMEM((R, D_MODEL), BF16),
        pltpu.VMEM((R, HG_COLS), F32),
        pltpu.VMEM((R, 2 * W_S5), F32),
        pltpu.VMEM((R, 2 * S5_LANES), F32),
        pltpu.VMEM((R, W_S5), BF16),
        pltpu.VMEM((R, W_HG), F32),
        pltpu.VMEM((R, W_HG), chunk_dt),
        pltpu.VMEM((R, W_HG), chunk_dt),
        pltpu.VMEM((R, W_HG), chunk_dt),
        pltpu.VMEM((HG_HEADS, ROW_BLOCK, HG_DK), F32),
        pltpu.VMEM((ROW_BLOCK, ROW_BLOCK), jnp.int32),
        pltpu.VMEM((ROW_BLOCK, ROW_BLOCK), BF16),
        pltpu.VMEM((NB, SUBLANES, W_HG), F32),
        pltpu.VMEM((R, D_MODEL), BF16),
    ]
    return pl.pallas_call(
        functools.partial(_mixer_kernel, NB=NB, C=C, has_state=has_state),
        grid=grid,
        in_specs=in_specs,
        out_specs=[x_spec, h_spec, h_spec, s_spec],
        out_shape=out_shape,
        scratch_shapes=scratch,
        compiler_params=pltpu.CompilerParams(
            dimension_semantics=("arbitrary", "arbitrary"),
            vmem_limit_bytes=V7X_VMEM_LIMIT),
        name=name,
    )(*args)


def kernel(x_prompt, x_sample, state_s5_re, state_s5_im, state_hgrn, norm_g, w_in, s5_lambda_re, s5_lambda_im,
           s5_log_step, s5_b_re, s5_b_im, s5_c_re, s5_c_im, s5_d, w_glu, b_glu, hgrn_lb_logits, hgrn_onorm_g,
           w_out, final_norm_g):
    assert norm_g.shape[0] == 1, "single-layer model"
    l = 0
    a_re, a_im, bbre, bbim, lb = _prepare_params(
        s5_lambda_re[l], s5_lambda_im[l], s5_log_step[l], s5_b_re[l], s5_b_im[l], hgrn_lb_logits)
    bcat, ccat = _block_diag_weights(bbre, bbim, s5_c_re[l], s5_c_im[l])
    weights = (
        _pack_vectors(norm_g[l], final_norm_g, s5_d[l], b_glu[l], hgrn_onorm_g[l]),
        w_in[l].astype(BF16),
        a_re, a_im, bcat, ccat,
        w_glu[l].astype(BF16),
        lb,
        w_out[l].astype(BF16),
    )

    Bp, Lp, _ = x_prompt.shape
    yp, p_re, p_im, p_hg = _mixer_call(x_prompt, None, weights, NB=Bp, C=PROMPT_CHUNK, name="mixer_prompt")

    Bs, Ls, _ = x_sample.shape
    state = (state_s5_re[l].reshape(Bs, S5_LANES).astype(F32),
             state_s5_im[l].reshape(Bs, S5_LANES).astype(F32),
             state_hgrn[l].astype(F32))
    ys, s_re, s_im, s_hg = _mixer_call(x_sample, state, weights, NB=ROW_BLOCK // Ls, C=Ls, name="mixer_sample")
    ys = ys.reshape(Bs, Ls, D_MODEL)

    sd = state_s5_re.dtype
    shp = (1, Bp, S5_GROUPS, S5_STATE)
    shs = (1, Bs, S5_GROUPS, S5_STATE)
    return (yp, ys,
            p_re.reshape(shp).astype(sd), p_im.reshape(shp).astype(sd), p_hg[None].astype(state_hgrn.dtype),
            s_re.reshape(shs).astype(sd), s_im.reshape(shs).astype(sd), s_hg[None].astype(state_hgrn.dtype))
```

```python
import functools
import math

import jax
import jax.numpy as jnp
import numpy as np
from jax import lax
from jax.experimental import pallas as pl
from jax.experimental.pallas import tpu as pltpu

F32 = jnp.float32
BF16 = jnp.bfloat16

D_MODEL = 1024
W_S5 = 512
W_HG = 512
S5_CH = 16
S5_GROUPS = 32
S5_STATE = 64
S5_LANES = S5_GROUPS * S5_STATE
S5_HALF = S5_LANES // 2
HG_DK = 128
HG_HEADS = 4
HG_DV = 128
EPS = 1e-6
LAMBDA_RE_MAX = -1e-4
PROJ_OUT = 3072
OFF_Q = 1024
HG_COLS = PROJ_OUT - OFF_Q
SECTION = 512
COL_Q, COL_F, COL_I, COL_ZH = (SECTION * n for n in range(4))
PROMPT_CHUNK = 64
LOG2E = 1.4426950408889634

LANES = 128
SUBLANES = 8
BF16_TILE_ROWS = 16
ROW_BLOCK = 128
MATMUL_ROWS = 512
READOUT_ROWS = 512
S5_INPUT_ROWS = 512
V7X_VMEM_LIMIT = 60 * 1024 * 1024


def _prep_kernel(lre_ref, lim_ref, ls_ref, brt_ref, bit_ref, lg_ref,
                 are_ref, aim_ref, bbre_ref, bbim_ref, lb_ref):
    lr = jnp.minimum(lre_ref[...], LAMBDA_RE_MAX)
    li = lim_ref[...]
    dt = jnp.exp(ls_ref[...])
    er = jnp.exp(lr * dt)
    a_re = er * jnp.cos(li * dt)
    a_im = er * jnp.sin(li * dt)
    den = lr * lr + li * li
    zr = ((a_re - 1.0) * lr + a_im * li) / den
    zi = (a_im * lr - (a_re - 1.0) * li) / den
    for g in range(S5_GROUPS):
        cols = slice(S5_STATE * g, S5_STATE * (g + 1))
        are_ref[:, cols] = a_re[g:g + 1, :]
        aim_ref[:, cols] = a_im[g:g + 1, :]
        br = brt_ref[g]
        bi = bit_ref[g]
        zr_g = zr[g:g + 1, :]
        zi_g = zi[g:g + 1, :]
        bbre_ref[g] = zr_g * br - zi_g * bi
        bbim_ref[g] = zr_g * bi + zi_g * br
    lg = lg_ref[...]
    e = jnp.exp(lg - jnp.max(lg, axis=0, keepdims=True))
    lb_ref[...] = e[0:1, :] / jnp.sum(e, axis=0, keepdims=True)


def _prepare_params(lam_re, lam_im, log_step, b_re, b_im, lb_logits):
    gps = jax.ShapeDtypeStruct((1, S5_LANES), F32)
    gcs = jax.ShapeDtypeStruct((S5_GROUPS, S5_CH, S5_STATE), F32)
    return pl.pallas_call(
        _prep_kernel,
        out_shape=[gps, gps, gcs, gcs, jax.ShapeDtypeStruct((1, HG_HEADS * HG_DK), F32)],
        name="s5_hgrn_prep",
    )(lam_re.astype(F32), lam_im.astype(F32), log_step.astype(F32).reshape(S5_GROUPS, 1),
      jnp.transpose(b_re.astype(F32), (0, 2, 1)), jnp.transpose(b_im.astype(F32), (0, 2, 1)),
      lb_logits.astype(F32))


def _block_diag_weights(bbre, bbim, c_re, c_im):
    hg = S5_GROUPS // 2
    eye = jnp.eye(hg, dtype=BF16)[None, :, None, :, None]

    def in_half(bbt):
        x = bbt.astype(BF16).reshape(2, hg, S5_CH, 1, S5_STATE) * eye
        return x.reshape(2, hg * S5_CH, hg * S5_STATE)

    def out_half(c):
        ct = jnp.transpose(c.astype(BF16), (0, 2, 1)).reshape(2, hg, S5_STATE, 1, S5_CH)
        return (ct * eye).reshape(2, hg * S5_STATE, hg * S5_CH)

    bcat = jnp.concatenate([in_half(bbre), in_half(bbim)], axis=2)
    ccat = jnp.concatenate([out_half(c_re), out_half(-c_im)], axis=1)
    return bcat, ccat


def _sigmoid(x):
    return 1.0 / (1.0 + jnp.exp2(x * (-LOG2E)))


def _gelu_tanh(x):
    return 0.5 * x * (1.0 + jnp.tanh(math.sqrt(2.0 / math.pi) * (x + 0.044715 * (x * x * x))))


def _dot(a, b):
    return jnp.dot(a, b, preferred_element_type=F32)


def _dot_nt(a, b):
    return lax.dot_general(a, b, (((1,), (1,)), ((), ())), preferred_element_type=F32)


def _dot_tn(a, b):
    return lax.dot_general(a, b, (((0,), (0,)), ((), ())), preferred_element_type=F32)


def _row_bcast_blocks(src_ref, blk, off):
    parts = []
    for n in range(ROW_BLOCK // blk):
        row = src_ref[n * blk + off:n * blk + off + 1, :]
        parts.append(jnp.broadcast_to(row, (blk, HG_DK)))
    return parts[0] if len(parts) == 1 else jnp.concatenate(parts, axis=0)


def _mixer_kernel(*refs, NB, C, has_state):
    if has_state:
        x_ref, h0r_ref, h0i_ref, s0_ref = refs[:4]
        refs = refs[4:]
    else:
        x_ref = refs[0]
        refs = refs[1:]
    (perm_ref, permt_ref, vec_ref, win_ref, are_ref, aim_ref, bcat_ref, ccat_ref, wglu_ref, lb_ref, wout_ref,
     y_ref, hr_ref, hi_ref, s_ref,
     hn_ref, proj_ref, uz_ref, bu_ref, s5_ref, o_ref, qe_ref, ke_ref, v_ref, bsall_ref, lev_ref, tri_ref,
     dl_ref, cat_ref) = refs

    ng_ref, fg_ref, d_ref, bglu_ref, og_ref = (vec_ref.at[:, a:b] for a, b in _VEC_SLICES)

    j = pl.program_id(1)
    nj = pl.num_programs(1)
    R = NB * C
    n_rb = R // ROW_BLOCK
    nbat = ROW_BLOCK // C

    @pl.when(j == 0)
    def _init():
        if has_state:
            hr_ref[...] = h0r_ref[...]
            hi_ref[...] = h0i_ref[...]
        else:
            hr_ref[...] = jnp.zeros(hr_ref.shape, F32)
            hi_ref[...] = jnp.zeros(hi_ref.shape, F32)
            s_ref[...] = jnp.zeros(s_ref.shape, F32)

    MM = min(R, MATMUL_ROWS)
    n_mb = R // MM
    mbat = MM // C

    def aligned(start, n):
        return start if isinstance(start, int) else pl.multiple_of(start, n)

    def rows_of(rb, n=MM):
        return pl.ds(aligned(rb * n, n), n)

    def loop(trips, body):
        if trips == 1:
            body(0, 0)
        else:
            lax.fori_loop(0, trips, body, 0)

    TB = ROW_BLOCK // NB

    def seq_major_block(src_ref, blk):
        assert isinstance(blk, int)
        if TB == C:
            return src_ref[blk * ROW_BLOCK:(blk + 1) * ROW_BLOCK, :]
        return jnp.concatenate([src_ref[b * C + blk * TB:b * C + (blk + 1) * TB, :] for b in range(NB)], axis=0)

    flat_rows = len(x_ref.shape) == 2

    def x_rows(mb):
        if flat_rows:
            return x_ref[rows_of(mb), :]
        return x_ref[pl.ds(mb * mbat, mbat)].reshape(MM, D_MODEL)

    def prenorm(x):
        return (x * lax.rsqrt(jnp.mean(x * x, axis=-1, keepdims=True) + EPS) * ng_ref[...]).astype(BF16)

    def hg_proj(hn, sec):
        return _dot(hn, win_ref[:, OFF_Q + SECTION * sec:OFF_Q + SECTION * (sec + 1)])

    def project_chunk():
        def body(rb, c):
            rows = rows_of(rb)
            hn = prenorm(x_rows(rb))
            hn_ref[rows, :] = hn
            for sec in range(HG_COLS // SECTION):
                proj_ref[rows, SECTION * sec:SECTION * (sec + 1)] = hg_proj(hn, sec)
            return c
        loop(n_mb, body)

    row = lax.broadcasted_iota(jnp.int32, (ROW_BLOCK, ROW_BLOCK), 0)
    col = lax.broadcasted_iota(jnp.int32, (ROW_BLOCK, ROW_BLOCK), 1)
    tin = row & (C - 1)
    levels = []
    m = 1
    while m < C:
        levels.append(m)
        m *= 2

    @pl.when(j == 0)
    def _masks():
        lev = jnp.where(row == col, 0, -1)
        for li, m in enumerate(levels):
            same = (row & ~(2 * m - 1)) == (col & ~(2 * m - 1))
            own = same & ((row & m) != 0) & ((col & m) == 0)
            lev = jnp.where(own, li + 1, lev)
        lev_ref[...] = lev
        in_chunk = (row & ~(C - 1)) == (col & ~(C - 1))
        tri_ref[...] = jnp.where(in_chunk & (col <= row), 1.0, 0.0).astype(BF16)

    def step_body():
        heads = range(HG_HEADS)
        hcs = [slice(HG_DK * hd, HG_DK * (hd + 1)) for hd in heads]
        scan_width = min(S5_HALF, (SUBLANES * S5_HALF) // NB)

        def hgrn_block(rb, mxu_fill):
            rows = rows_of(rb, ROW_BLOCK)
            lv = lev_ref[...]
            qs, ks, fs, vs, gcats = [], [], [], [], []
            for hd in heads:
                lb = lb_ref[:, hcs[hd]]
                q = proj_ref[rows, COL_Q + HG_DK * hd:COL_Q + HG_DK * (hd + 1)]
                gf = proj_ref[rows, COL_F + HG_DK * hd:COL_F + HG_DK * (hd + 1)]
                v = proj_ref[rows, COL_I + HG_DV * hd:COL_I + HG_DV * (hd + 1)].astype(BF16)
                f = lb + (1.0 - lb) * _sigmoid(gf)
                g = jnp.log2(f)
                k = 1.0 - f
                g1 = g.astype(BF16)
                r1 = g - g1.astype(F32)
                g2 = r1.astype(BF16)
                g3 = (r1 - g2.astype(F32)).astype(BF16)
                p4 = row & 3
                dec1 = jnp.where((row & 1) == 1, f, 1.0)
                dec2 = jnp.where(p4 == 0, pltpu.roll(f, ROW_BLOCK - 1, axis=0),
                                 jnp.where(p4 == 1, 1.0,
                                           jnp.where(p4 == 2, f, f * pltpu.roll(f, 1, axis=0))))
                qs.append(q)
                ks.append(k)
                fs.append((dec1.astype(BF16), dec2.astype(BF16)))
                vs.append(v)
                gcats.append(jnp.concatenate([g1, g2, g3], axis=1))
            bparts = [_dot(tri_ref[...], gcats[hd]) for hd in heads]
            bs = []
            for hd in heads:
                bp = bparts[hd]
                b = (bp[:, 0:HG_DK] + bp[:, HG_DK:2 * HG_DK]) + bp[:, 2 * HG_DK:3 * HG_DK]
                bsall_ref[hd] = b
                bs.append(b)
            qbs = [q.astype(BF16) for q in qs]
            kbs = [k.astype(BF16) for k in ks]
            atts = [jnp.where(lv == 0, _dot_nt(qbs[hd], kbs[hd]), 0.0) for hd in heads]
            for li, m in enumerate(levels):
                two_m = 2 * m
                p2 = row & (two_m - 1)
                if m > 2:
                    sgn = jnp.where(p2 >= m, 1.0, -1.0)
                for hd in heads:
                    if m <= 2:
                        decb = fs[hd][li]
                    else:
                        ref = _row_bcast_blocks(bsall_ref.at[hd], two_m, m - 1)
                        decb = jnp.exp2((bs[hd] - ref) * sgn).astype(BF16)
                    atts[hd] = jnp.where(lv == li + 1, _dot_nt(qbs[hd] * decb, kbs[hd] * decb), atts[hd])
                if li == 0:
                    for piece in mxu_fill:
                        piece()
            for hd in heads:
                o_ref[rows, hcs[hd]] = _dot(atts[hd].astype(BF16), vs[hd])
            for hd in heads:
                hc = hcs[hd]
                bs_ref = bsall_ref.at[hd]
                b = bs[hd]
                if C >= SUBLANES:
                    b_last = _row_bcast_blocks(bs_ref, C, C - 1)
                else:
                    b_last = b
                    for d in range(1, C):
                        b_last = jnp.where(tin == C - 1 - d, pltpu.roll(b, ROW_BLOCK - d, axis=0), b_last)
                qe_ref[rows, hc] = (qs[hd] * jnp.exp2(b)).astype(qe_ref.dtype)
                ke_ref[rows, hc] = (ks[hd] * jnp.exp2(b_last - b)).astype(ke_ref.dtype)
                v_ref[rows, hc] = vs[hd].astype(v_ref.dtype)
                for n in range(nbat):
                    last = bs_ref[n * C + C - 1:n * C + C, :]
                    dl_ref[rb * nbat + n, :, hc] = jnp.broadcast_to(jnp.exp2(last), (SUBLANES, HG_DK))

        SR = min(R, S5_INPUT_ROWS)
        sblocks = SR // ROW_BLOCK

        def merged(it, c):
            rows = rows_of(it, SR)
            hp = jnp.concatenate(
                [_dot(perm_ref[...], seq_major_block(hn_ref, it * sblocks + sb)).astype(BF16)
                 for sb in range(sblocks)], axis=0)
            uz = {}

            def uz_piece(sec):
                uz[sec] = _dot(hp, win_ref[:, W_S5 * sec:W_S5 * (sec + 1)])
                uz_ref[rows, W_S5 * sec:W_S5 * (sec + 1)] = uz[sec]

            def bu_piece(i):
                u_i = uz[0][:, W_S5 // 2 * i:W_S5 // 2 * (i + 1)].astype(BF16)
                bu_ref[rows, 2 * S5_HALF * i:2 * S5_HALF * (i + 1)] = _dot(u_i, bcat_ref[i])

            pieces = [functools.partial(uz_piece, 0), functools.partial(uz_piece, 1),
                      functools.partial(bu_piece, 0), functools.partial(bu_piece, 1)]
            per_block = len(pieces) // sblocks
            for sb in range(sblocks):
                hgrn_block(it * sblocks + sb, pieces[sb * per_block:(sb + 1) * per_block])
            for i in range(2):
                for off in range(0, S5_HALF, scan_width):
                    st = slice(S5_HALF * i + off, S5_HALF * i + off + scan_width)
                    re_c = 2 * S5_HALF * i + off
                    im_c = re_c + S5_HALF
                    ar = are_ref[:, st]
                    ai = aim_ref[:, st]
                    hr = hr_ref[:, st]
                    hi = hi_ref[:, st]
                    for tt in range(TB * sblocks):
                        trows = pl.ds(aligned(it * SR + tt * NB, NB), NB)
                        hr, hi = (ar * hr - ai * hi + bu_ref[trows, re_c:re_c + scan_width],
                                  ar * hi + ai * hr + bu_ref[trows, im_c:im_c + scan_width])
                        bu_ref[trows, re_c:re_c + scan_width] = hr
                        bu_ref[trows, im_c:im_c + scan_width] = hi
                    hr_ref[:, st] = hr
                    hi_ref[:, st] = hi
            return c
        assert R == SR
        loop(R // SR, merged)

        def inter_out(bi):
            srows = pl.ds(aligned(bi * C, C), C)
            for hd in heads:
                hc = hcs[hd]
                if has_state:
                    oi = _dot(qe_ref[srows, hc], s0_ref[bi, hd].astype(qe_ref.dtype))
                else:
                    oi = _dot_nt(qe_ref[srows, hc], s_ref[bi, hd].astype(qe_ref.dtype))
                o_ref[srows, hc] = o_ref[srows, hc] + oi

        def inter_state(bi):
            srows = pl.ds(aligned(bi * C, C), C)
            for hd in heads:
                hc = hcs[hd]
                if has_state:
                    upd = _dot_tn(ke_ref[srows, hc], v_ref[srows, hc])
                    s_ref[bi, hd] = s0_ref[bi, hd] * dl_ref[bi, :, hc].T[:, 0:1] + upd
                else:
                    upd = _dot_tn(v_ref[srows, hc], ke_ref[srows, hc])
                    s_ref[bi, hd] = s_ref[bi, hd] * dl_ref[bi, 0:1, hc] + upd

        RR = min(R, READOUT_ROWS)
        rbat = RR // C
        merge_inter = rbat <= 8

        def readout_inter(rb, c):
            rows = rows_of(rb, RR)
            ys = []
            for i in range(2):
                hs = bu_ref[rows, 2 * S5_HALF * i:2 * S5_HALF * (i + 1)].astype(BF16)
                ys.append(_dot(hs, ccat_ref[i]))
            if merge_inter:
                for n in range(rbat):
                    inter_out(rb * rbat + n)
            u = uz_ref[rows, 0:W_S5]
            y = jnp.concatenate(ys, axis=1) + d_ref[...] * u
            g = _gelu_tanh(y)
            glu = _dot(g.astype(BF16), wglu_ref[...]) + bglu_ref[...]
            if merge_inter:
                for n in range(rbat):
                    inter_state(rb * rbat + n)
            zs = uz_ref[rows, W_S5:2 * W_S5]
            s5 = g * _sigmoid(glu) * (zs * _sigmoid(zs))
            s5_ref[rows, :] = s5.astype(BF16)
            return c
        loop(R // RR, readout_inter)

        if not merge_inter and C >= SUBLANES:
            def inter(bi, c):
                inter_out(bi)
                inter_state(bi)
                return c
            lax.fori_loop(0, NB, inter, 0, unroll=2)

        if C < SUBLANES:
            assert has_state and not merge_inter
            per_tile = SUBLANES // C
            rid = lax.broadcasted_iota(jnp.int32, (SUBLANES, HG_DK), 0)
            cid = lax.broadcasted_iota(jnp.int32, (HG_DK, SUBLANES), 1)

            def inter_tile(ti, c):
                srows = pl.ds(pl.multiple_of(ti * SUBLANES, SUBLANES), SUBLANES)
                for hd in heads:
                    hc = hcs[hd]
                    qe8 = qe_ref[srows, hc]
                    ke_t = ke_ref[srows, hc].T
                    v8 = v_ref[srows, hc]
                    acc = o_ref[srows, hc]
                    dec = dl_ref[ti * per_tile + per_tile - 1, :, hc]
                    for s in range(per_tile - 1):
                        dec = jnp.where(rid == s, dl_ref[ti * per_tile + s, :, hc], dec)
                    dec_t = dec.T
                    for s in range(per_tile):
                        bi = ti * per_tile + s
                        st = s0_ref[bi, hd]
                        rows_s = (rid >= s * C) & (rid < (s + 1) * C)
                        cols_s = (cid >= s * C) & (cid < (s + 1) * C)
                        acc = acc + _dot(jnp.where(rows_s, qe8, 0.0), st)
                        upd = _dot(jnp.where(cols_s, ke_t, 0.0), v8)
                        s_ref[bi, hd] = st * dec_t[:, s:s + 1] + upd
                    o_ref[srows, hc] = acc
                return c
            lax.fori_loop(0, NB // per_tile, inter_tile, 0, unroll=4)

        for blk in range(n_rb):
            back = _dot(permt_ref[...], s5_ref[blk * ROW_BLOCK:(blk + 1) * ROW_BLOCK, :]).astype(BF16)
            if TB == C:
                cat_ref[blk * ROW_BLOCK:(blk + 1) * ROW_BLOCK, 0:W_S5] = back
            else:
                for b in range(NB):
                    cat_ref[b * C + blk * TB:b * C + (blk + 1) * TB, 0:W_S5] = back[b * TB:(b + 1) * TB, :]

        def phase5(rb, c):
            rows = rows_of(rb)
            for hd in range(HG_HEADS):
                hc = slice(HG_DV * hd, HG_DV * (hd + 1))
                o = o_ref[rows, hc]
                o = o * lax.rsqrt(jnp.mean(o * o, axis=-1, keepdims=True) + EPS)
                zh = proj_ref[rows, COL_ZH + HG_DV * hd:COL_ZH + HG_DV * (hd + 1)]
                hg = o * og_ref[:, hc] * (zh * _sigmoid(zh))
                cat_ref[rows, W_S5 + HG_DV * hd:W_S5 + HG_DV * (hd + 1)] = hg.astype(BF16)
            out = _dot(cat_ref[rows, :], wout_ref[...])
            xn = x_rows(rb) + out
            yv = xn * lax.rsqrt(jnp.mean(xn * xn, axis=-1, keepdims=True) + EPS) * fg_ref[...]
            if flat_rows:
                y_ref[rows, :] = yv
            else:
                y_ref[pl.ds(rb * mbat, mbat)] = yv.reshape(mbat, C, D_MODEL)
            return c
        loop(n_mb, phase5)

    project_chunk()
    step_body()

    if not has_state:
        @pl.when(j == nj - 1)
        def _fin():
            def tr(b, c):
                for hd in range(HG_HEADS):
                    s_ref[b, hd] = s_ref[b, hd].T
                return c
            lax.fori_loop(0, NB, tr, 0)


_VEC_WIDTHS = (D_MODEL, D_MODEL, W_S5, W_S5, W_HG)
_VEC_SLICES = tuple((sum(_VEC_WIDTHS[:n]), sum(_VEC_WIDTHS[:n + 1])) for n in range(len(_VEC_WIDTHS)))


def _pack_vectors(norm_g, final_norm_g, d, b_glu, onorm_g):
    parts = [norm_g, final_norm_g, d.reshape(-1), b_glu, onorm_g]
    assert tuple(p.shape[0] for p in parts) == _VEC_WIDTHS
    return jnp.concatenate([p.astype(F32) for p in parts]).reshape(1, sum(_VEC_WIDTHS))


def _time_major_perm(NB):
    steps = ROW_BLOCK // NB
    dst = np.arange(ROW_BLOCK)
    src = (dst % NB) * steps + dst // NB
    p = np.zeros((ROW_BLOCK, ROW_BLOCK), np.float32)
    p[dst, src] = 1.0
    return p


def _mixer_call(x, state, weights, *, NB, C, name):
    B, L, _ = x.shape
    has_state = state is not None
    assert not has_state or L == C, "a given state is consumed by a single chunk per sequence"
    grid = (B // NB, L // C)
    R = NB * C
    perm = _time_major_perm(NB)
    weights = (jnp.asarray(perm, BF16), jnp.asarray(perm.T, BF16)) + tuple(weights)

    def full(a):
        nd = a.ndim
        return pl.BlockSpec(a.shape, lambda i, j, nd=nd: (0,) * nd, pipeline_mode=pl.Buffered(1))

    flat_rows = C % SUBLANES != 0
    if flat_rows:
        assert L == C
        x = x.reshape(B * L, D_MODEL)
        x_spec = pl.BlockSpec((R, D_MODEL), lambda i, j: (i, 0))
    else:
        x_spec = pl.BlockSpec((NB, C, D_MODEL), lambda i, j: (i, j, 0))
    h_spec = pl.BlockSpec((NB, S5_LANES), lambda i, j: (i, 0))
    s_spec = pl.BlockSpec((NB, HG_HEADS, HG_DK, HG_DV), lambda i, j: (i, 0, 0, 0))
    in_specs = [x_spec] + ([h_spec, h_spec, s_spec] if has_state else []) + [full(w) for w in weights]
    args = [x] + (list(state) if has_state else []) + list(weights)
    out_shape = [
        jax.ShapeDtypeStruct(x.shape, x.dtype),
        jax.ShapeDtypeStruct((B, S5_LANES), F32),
        jax.ShapeDtypeStruct((B, S5_LANES), F32),
        jax.ShapeDtypeStruct((B, HG_HEADS, HG_DK, HG_DV), F32),
    ]
    chunk_dt = BF16 if C % BF16_TILE_ROWS == 0 else F32
    scratch = [
        pltpu.VMEM((R, D_MODEL), BF16),
        pltpu.VMEM((R, HG_COLS), F32),
        pltpu.VMEM((R, 2 * W_S5), F32),
        pltpu.VMEM((R, 2 * S5_LANES), F32),
        pltpu.VMEM((R, W_S5), BF16),
        pltpu.VMEM((R, W_HG), F32),
        pltpu.VMEM((R, W_HG), chunk_dt),
        pltpu.VMEM((R, W_HG), chunk_dt),
        pltpu.VMEM((R, W_HG), chunk_dt),
        pltpu.VMEM((HG_HEADS, ROW_BLOCK, HG_DK), F32),
        pltpu.VMEM((ROW_BLOCK, ROW_BLOCK), jnp.int32),
        pltpu.VMEM((ROW_BLOCK, ROW_BLOCK), BF16),
        pltpu.VMEM((NB, SUBLANES, W_HG), F32),
        pltpu.VMEM((R, D_MODEL), BF16),
    ]
    return pl.pallas_call(
        functools.partial(_mixer_kernel, NB=NB, C=C, has_state=has_state),
        grid=grid,
        in_specs=in_specs,
        out_specs=[x_spec, h_spec, h_spec, s_spec],
        out_shape=out_shape,
        scratch_shapes=scratch,
        compiler_params=pltpu.CompilerParams(
            dimension_semantics=("arbitrary", "arbitrary"),
            allow_input_fusion=[True] * len(args),
            vmem_limit_bytes=V7X_VMEM_LIMIT),
        name=name,
    )(*args)


def kernel(x_prompt, x_sample, state_s5_re, state_s5_im, state_hgrn, norm_g, w_in, s5_lambda_re, s5_lambda_im,
           s5_log_step, s5_b_re, s5_b_im, s5_c_re, s5_c_im, s5_d, w_glu, b_glu, hgrn_lb_logits, hgrn_onorm_g,
           w_out, final_norm_g):
    assert norm_g.shape[0] == 1, "single-layer model"
    l = 0
    a_re, a_im, bbre, bbim, lb = _prepare_params(
        s5_lambda_re[l], s5_lambda_im[l], s5_log_step[l], s5_b_re[l], s5_b_im[l], hgrn_lb_logits)
    bcat, ccat = _block_diag_weights(bbre, bbim, s5_c_re[l], s5_c_im[l])
    weights = (
        _pack_vectors(norm_g[l], final_norm_g, s5_d[l], b_glu[l], hgrn_onorm_g[l]),
        w_in[l].astype(BF16),
        a_re, a_im, bcat, ccat,
        w_glu[l].astype(BF16),
        lb,
        w_out[l].astype(BF16),
    )

    Bp, Lp, _ = x_prompt.shape
    yp, p_re, p_im, p_hg = _mixer_call(x_prompt, None, weights, NB=Bp, C=PROMPT_CHUNK, name="mixer_prompt")

    Bs, Ls, _ = x_sample.shape
    state = (state_s5_re[l].reshape(Bs, S5_LANES).astype(F32),
             state_s5_im[l].reshape(Bs, S5_LANES).astype(F32),
             state_hgrn[l].astype(F32))
    ys, s_re, s_im, s_hg = _mixer_call(x_sample, state, weights, NB=ROW_BLOCK // Ls, C=Ls, name="mixer_sample")
    ys = ys.reshape(Bs, Ls, D_MODEL)

    sd = state_s5_re.dtype
    shp = (1, Bp, S5_GROUPS, S5_STATE)
    shs = (1, Bs, S5_GROUPS, S5_STATE)
    return (yp, ys,
            p_re.reshape(shp).astype(sd), p_im.reshape(shp).astype(sd), p_hg[None].astype(state_hgrn.dtype),
            s_re.reshape(shs).astype(sd), s_im.reshape(shs).astype(sd), s_hg[None].astype(state_hgrn.dtype))
```
